```python
import jax, jax.numpy as jnp
from jax import lax
import numpy as np

D_MODEL = 1024
BATCH = 32
SEQ = 2048
DEPTH = 1

RW_HEADS = 8
RW_HEAD = 64
RW_WIDTH = RW_HEADS * RW_HEAD
DECAY_LORA = 64
AAA_LORA = 64
GATE_LORA = 128
SHIFT_WIDTH = 3
MLA_HEADS = 8
QK_NOPE = 64
QK_ROPE = 32
V_HEAD = 64
Q_LORA = 384
KV_LORA = 256
ROPE_THETA = 10000.0
Q_BLOCK = 128
D_FF = 4 * D_MODEL
LN_EPS = 1e-5
RMS_EPS = 1e-6
GN_EPS = 64e-5
L2_EPS = 1e-12
DN_ALPHA = (2.0 * DEPTH) ** 0.25
DN_BETA = (8.0 * DEPTH) ** -0.25

RW_SPLITS = (RW_WIDTH, RW_WIDTH, RW_WIDTH, DECAY_LORA, DECAY_LORA, AAA_LORA, AAA_LORA, GATE_LORA)
RW_COLS = sum(RW_SPLITS)
IN_SPLITS = (RW_COLS, Q_LORA, KV_LORA, QK_ROPE, D_MODEL, D_MODEL)
D_IN = sum(IN_SPLITS)

kernel_name = "hybrid_rwkv7_mla_gated_deepnorm_layer"


def _offsets(sizes):
    return [sum(sizes[:i + 1]) for i in range(len(sizes) - 1)]


def layer_norm(x, g, b):
    xf = x.astype(jnp.float32)
    mu = jnp.mean(xf, -1, keepdims=True)
    var = jnp.mean(jnp.square(xf - mu), -1, keepdims=True)
    return ((xf - mu) * lax.rsqrt(var + LN_EPS)).astype(x.dtype) * g + b


def rms_norm(x, g):
    xf = x.astype(jnp.float32)
    return (xf * lax.rsqrt(jnp.mean(xf * xf, -1, keepdims=True) + RMS_EPS)).astype(x.dtype) * g


def rope(x, pos):
    half = x.shape[-1] // 2
    inv = ROPE_THETA ** (-jnp.arange(half, dtype=jnp.float32) / half)
    ang = pos.astype(jnp.float32)[..., None] * inv
    ang = ang.reshape(ang.shape[:2] + (1,) * (x.ndim - 3) + (half,))
    cos, sin = jnp.cos(ang), jnp.sin(ang)
    xf = x.astype(jnp.float32)
    x1, x2 = xf[..., :half], xf[..., half:]
    return jnp.concatenate([x1 * cos - x2 * sin, x1 * sin + x2 * cos], -1).astype(x.dtype)


def centred_dwconv(x, w):
    pad = SHIFT_WIDTH // 2
    return lax.conv_general_dilated(
        x, w[:, None, :].astype(x.dtype), window_strides=(1,), padding=[(pad, pad)],
        dimension_numbers=("NWC", "WIO", "NWC"), feature_group_count=x.shape[-1])


def rwkv7_scan(r, w, k, v, a, b, reverse):
    def step(state, inp):
        r_t, w_t, k_t, v_t, a_t, b_t = inp
        sa = jnp.einsum("bhij,bhj->bhi", state, a_t)
        state = (state * w_t[:, :, None, :] + sa[..., None] * b_t[:, :, None, :]
                 + v_t[..., None] * k_t[:, :, None, :])
        return state, jnp.einsum("bhij,bhj->bhi", state, r_t)
    xs = tuple(jnp.moveaxis(t, 1, 0) for t in (r, w, k, v, a, b))
    s0 = jnp.zeros((r.shape[0], r.shape[2], r.shape[3], r.shape[3]), jnp.float32)
    _, y = lax.scan(step, s0, xs, reverse=reverse)
    return jnp.moveaxis(y, 0, 1)


def rwkv7_branch(z, conv_w, w0, w2, a0, a2, k_k, k_a, r_k, g2, lnx_g, lnx_b):
    B, S, _ = z.shape
    z = centred_dwconv(z, conv_w)
    r, k, v, zwf, zwb, zaf, zab, zg = jnp.split(z, _offsets(RW_SPLITS), axis=-1)
    heads = lambda t: t.astype(jnp.float32).reshape(B, S, RW_HEADS, RW_HEAD)
    pheads = lambda t: t.astype(jnp.float32).reshape(RW_HEADS, RW_HEAD)
    r_h, k_h, v_h = heads(r), heads(k), heads(v)
    kk = k_h * pheads(k_k)
    kk = kk / jnp.maximum(jnp.sqrt(jnp.sum(kk * kk, -1, keepdims=True)), L2_EPS)
    k_a_h = pheads(k_a)
    outs = []
    for d, (zw, za, rev) in enumerate(((zwf, zaf, False), (zwb, zab, True))):
        w_log = -jax.nn.softplus(-(pheads(w0[d]) + heads(jnp.tanh(zw.astype(jnp.float32)) @ w2[d]))) - 0.5
        decay = jnp.exp(-jnp.exp(w_log))
        a = jax.nn.sigmoid(pheads(a0[d]) + heads(za @ a2[d]))
        k_d = k_h * (1.0 + (a - 1.0) * k_a_h)
        outs.append(rwkv7_scan(r_h, decay, k_d, v_h, -kk, kk * a, rev))
    y = outs[0] + outs[1]
    mu = jnp.mean(y, -1, keepdims=True)
    var = jnp.mean(jnp.square(y - mu), -1, keepdims=True)
    y = ((y - mu) * lax.rsqrt(var + GN_EPS)).reshape(B, S, RW_WIDTH) * lnx_g + lnx_b
    bonus = (jnp.sum(r_h * k_h * pheads(r_k), -1, keepdims=True) * v_h).reshape(B, S, RW_WIDTH)
    g = jax.nn.sigmoid(zg) @ g2
    return ((y + bonus) * g).astype(z.dtype)


def mla_branch(z_q, z_kv, z_kr, pos, q_norm_g, kv_norm_g, w_uq, w_ukv):
    B, S, _ = z_q.shape
    q = (rms_norm(z_q, q_norm_g) @ w_uq).reshape(B, S, MLA_HEADS, QK_NOPE + QK_ROPE)
    q = jnp.concatenate([q[..., :QK_NOPE], rope(q[..., QK_NOPE:], pos)], -1)
    kv = (rms_norm(z_kv, kv_norm_g) @ w_ukv).reshape(B, S, MLA_HEADS, QK_NOPE + V_HEAD)
    k_pe = rope(z_kr, pos)
    k = jnp.concatenate(
        [kv[..., :QK_NOPE], jnp.broadcast_to(k_pe[:, :, None, :], (B, S, MLA_HEADS, QK_ROPE))], -1)
    v = kv[..., QK_NOPE:]
    scale = (QK_NOPE + QK_ROPE) ** -0.5
    n_blk = S // Q_BLOCK
    q_blocks = jnp.moveaxis(q.reshape(B, n_blk, Q_BLOCK, MLA_HEADS, QK_NOPE + QK_ROPE), 1, 0)

    def attend(qb):
        s = jnp.einsum("bqhd,bkhd->bhqk", qb, k).astype(jnp.float32) * scale
        p = jax.nn.softmax(s, axis=-1).astype(v.dtype)
        return jnp.einsum("bhqk,bkhv->bqhv", p, v)

    o = lax.map(attend, q_blocks)
    return jnp.moveaxis(o, 0, 1).reshape(B, S, MLA_HEADS * V_HEAD)


def setup_inputs(seed: int = 0) -> dict:
    key = jax.random.key(seed)
    ks = jax.random.split(key, 32)
    L = DEPTH
    nrm = lambda k, shape, s: jax.random.normal(k, shape, jnp.float32) * s

    x = nrm(ks[0], (BATCH, SEQ, D_MODEL), 1.0)
    c = nrm(ks[1], (BATCH, D_MODEL), 1.0)
    offs = jax.random.randint(ks[2], (BATCH, 1), 0, 64, dtype=jnp.int32)
    positions = jnp.arange(SEQ, dtype=jnp.int32)[None, :] + offs

    w_ada = nrm(ks[3], (L, D_MODEL, 6 * D_MODEL), 0.2 * D_MODEL ** -0.5)
    b_ada = nrm(ks[4], (L, 6 * D_MODEL), 0.01)

    w_in = nrm(ks[5], (L, D_MODEL, D_IN), D_MODEL ** -0.5)
    w_in = w_in.at[:, :, 2 * RW_WIDTH:3 * RW_WIDTH].multiply(DN_BETA)

    rw_conv = jnp.broadcast_to(jnp.array([0.25, 0.5, 0.25], jnp.float32)[None, :, None],
                               (L, SHIFT_WIDTH, RW_COLS)) + nrm(ks[6], (L, SHIFT_WIDTH, RW_COLS), 0.02)
    lin = jnp.linspace(0.0, 1.0, RW_WIDTH, dtype=jnp.float32) ** 0.9
    rw_w0 = (-6.0 + 5.0 * lin)[None, None, :] + nrm(ks[7], (L, 2, RW_WIDTH), 0.1)
    rw_w2 = nrm(ks[8], (L, 2, DECAY_LORA, RW_WIDTH), 0.1 * DECAY_LORA ** -0.5)
    rw_a0 = nrm(ks[9], (L, 2, RW_WIDTH), 0.1)
    rw_a2 = nrm(ks[10], (L, 2, AAA_LORA, RW_WIDTH), 0.5 * AAA_LORA ** -0.5)
    rw_k_k = 0.85 + nrm(ks[11], (L, RW_WIDTH), 0.02)
    rw_k_a = 1.0 + nrm(ks[12], (L, RW_WIDTH), 0.02)
    rw_r_k = nrm(ks[13], (L, RW_WIDTH), 0.1)
    rw_g2 = nrm(ks[14], (L, GATE_LORA, RW_WIDTH), GATE_LORA ** -0.5)
    rw_lnx_g = 1.0 + nrm(ks[15], (L, RW_WIDTH), 0.02)
    rw_lnx_b = nrm(ks[16], (L, RW_WIDTH), 0.02)

    mla_q_norm_g = 1.0 + nrm(ks[17], (L, Q_LORA), 0.02)
    mla_kv_norm_g = 1.0 + nrm(ks[18], (L, KV_LORA), 0.02)
    mla_w_uq = nrm(ks[19], (L, Q_LORA, MLA_HEADS * (QK_NOPE + QK_ROPE)), Q_LORA ** -0.5)
    mla_w_ukv = nrm(ks[20], (L, KV_LORA, MLA_HEADS, QK_NOPE + V_HEAD), KV_LORA ** -0.5)
    mla_w_ukv = mla_w_ukv.at[..., QK_NOPE:].multiply(DN_BETA).reshape(L, KV_LORA, MLA_HEADS * (QK_NOPE + V_HEAD))

    w_br_rwkv = nrm(ks[21], (L, RW_WIDTH, D_MODEL), DN_BETA * RW_WIDTH ** -0.5)
    w_br_mla = nrm(ks[22], (L, MLA_HEADS * V_HEAD, D_MODEL), DN_BETA * (MLA_HEADS * V_HEAD) ** -0.5)
    w_out = nrm(ks[23], (L, D_MODEL, D_MODEL), DN_BETA * D_MODEL ** -0.5)
    ln1_g = 1.0 + nrm(ks[24], (L, D_MODEL), 0.02)
    ln1_b = nrm(ks[25], (L, D_MODEL), 0.02)

    w_ff1 = nrm(ks[26], (L, D_MODEL, D_FF), D_MODEL ** -0.5)
    w_ff2 = nrm(ks[27], (L, D_FF, D_MODEL), DN_BETA * D_FF ** -0.5)
    ln2_g = 1.0 + nrm(ks[28], (L, D_MODEL), 0.02)
    ln2_b = nrm(ks[29], (L, D_MODEL), 0.02)

    return {"x": x, "c": c, "positions": positions, "w_ada": w_ada, "b_ada": b_ada, "w_in": w_in,
            "rw_conv": rw_conv, "rw_w0": rw_w0, "rw_w2": rw_w2, "rw_a0": rw_a0, "rw_a2": rw_a2,
            "rw_k_k": rw_k_k, "rw_k_a": rw_k_a, "rw_r_k": rw_r_k, "rw_g2": rw_g2,
            "rw_lnx_g": rw_lnx_g, "rw_lnx_b": rw_lnx_b, "mla_q_norm_g": mla_q_norm_g,
            "mla_kv_norm_g": mla_kv_norm_g, "mla_w_uq": mla_w_uq, "mla_w_ukv": mla_w_ukv,
            "w_br_rwkv": w_br_rwkv, "w_br_mla": w_br_mla, "w_out": w_out, "ln1_g": ln1_g, "ln1_b": ln1_b,
            "w_ff1": w_ff1, "w_ff2": w_ff2, "ln2_g": ln2_g, "ln2_b": ln2_b}


def reference(x, c, positions, w_ada, b_ada, w_in, rw_conv, rw_w0, rw_w2, rw_a0, rw_a2, rw_k_k, rw_k_a,
              rw_r_k, rw_g2, rw_lnx_g, rw_lnx_b, mla_q_norm_g, mla_kv_norm_g, mla_w_uq, mla_w_ukv,
              w_br_rwkv, w_br_mla, w_out, ln1_g, ln1_b, w_ff1, w_ff2, ln2_g, ln2_b):
    c_act = jax.nn.silu(c)
    for l in range(DEPTH):
        mod = (c_act @ w_ada[l] + b_ada[l])[:, None, :]
        shift1, scale1, gate1, shift2, scale2, gate2 = jnp.split(mod, 6, axis=-1)

        h = x * (1.0 + scale1) + shift1
        z = h @ w_in[l]
        z_rw, z_q, z_kv, z_kr, z_ga, z_gb = jnp.split(z, _offsets(IN_SPLITS), axis=-1)
        y_rw = rwkv7_branch(z_rw, rw_conv[l], rw_w0[l], rw_w2[l], rw_a0[l], rw_a2[l], rw_k_k[l],
                            rw_k_a[l], rw_r_k[l], rw_g2[l], rw_lnx_g[l], rw_lnx_b[l])
        y_mla = mla_branch(z_q, z_kv, z_kr, positions, mla_q_norm_g[l], mla_kv_norm_g[l],
                           mla_w_uq[l], mla_w_ukv[l])
        mixed = (jax.nn.sigmoid(z_ga) * (y_rw @ w_br_rwkv[l])
                 + jax.nn.sigmoid(z_gb) * (y_mla @ w_br_mla[l]))
        x = layer_norm(DN_ALPHA * x + (1.0 + gate1) * (mixed @ w_out[l]), ln1_g[l], ln1_b[l])

        h = x * (1.0 + scale2) + shift2
        ff = jnp.square(jax.nn.relu(h @ w_ff1[l])) @ w_ff2[l]
        x = layer_norm(DN_ALPHA * x + (1.0 + gate2) * ff, ln2_g[l], ln2_b[l])
    return x
```

```python
import functools

import numpy as np
import jax
import jax.numpy as jnp
from jax import lax
from jax.experimental import pallas as pl
from jax.experimental.pallas import tpu as pltpu

F32 = jnp.float32
BF16 = jnp.bfloat16

D_MODEL = 1024
RW_HEADS = 8
RW_HEAD = 64
RW_WIDTH = RW_HEADS * RW_HEAD
DECAY_LORA = 64
AAA_LORA = 64
GATE_LORA = 128
RW_COLS = 3 * RW_WIDTH + 2 * DECAY_LORA + 2 * AAA_LORA + GATE_LORA
MLA_HEADS = 8
QK_NOPE = 64
QK_ROPE = 32
V_HEAD = 64
Q_LORA = 384
KV_LORA = 256
ROPE_THETA = 10000.0
D_FF = 4 * D_MODEL
LN_EPS = 1e-5
RMS_EPS = 1e-6
GN_EPS = 64e-5
L2_EPS = 1e-12
DEPTH = 1
DN_ALPHA = (2.0 * DEPTH) ** 0.25

LANE = 128
CHUNK = 64
SCAN_ROWS = 256
HALO_ROWS = 16
HEAD_PAD = 128
ZKV_COLS = KV_LORA + 2 * LANE
VMEM_LIMIT = 56 * 1024 * 1024


def _dot(a, b):
    return jnp.dot(a, b, preferred_element_type=F32)


def _dot_nt(a, b):
    return lax.dot_general(a, b, (((1,), (1,)), ((), ())), preferred_element_type=F32)


def _dot_tn(a, b):
    return lax.dot_general(a, b, (((0,), (0,)), ((), ())), preferred_element_type=F32)


def _const_spec(shape):
    zeros = (0,) * len(shape)
    return pl.BlockSpec(shape, lambda *_: zeros)


def _layer_norm(t, g, b):
    mu = jnp.mean(t, axis=-1, keepdims=True)
    d = t - mu
    var = jnp.mean(d * d, axis=-1, keepdims=True)
    return d * lax.rsqrt(var + LN_EPS) * g + b


def _ada_kernel(c_ref, w_ref, b_ref, o_ref):
    c = c_ref[...]
    act = c * jax.nn.sigmoid(c)
    o_ref[...] = _dot(act.astype(BF16), w_ref[...].astype(BF16)) + b_ref[...]


def _ada(c, w, b):
    bsz = c.shape[0]
    n = w.shape[1]
    tn = 1536
    return pl.pallas_call(
        _ada_kernel,
        grid=(n // tn,),
        in_specs=[_const_spec((bsz, D_MODEL)),
                  pl.BlockSpec((D_MODEL, tn), lambda j: (0, j)),
                  pl.BlockSpec((1, tn), lambda j: (0, j))],
        out_specs=pl.BlockSpec((bsz, tn), lambda j: (0, j)),
        out_shape=jax.ShapeDtypeStruct((bsz, n), F32),
        name="ada",
    )(c, w, b.reshape(1, n))


def _rope_kernel(pos_ref, inv_ref, cos_ref, sin_ref):
    ang = pos_ref[...].astype(F32) * inv_ref[...]
    cos_ref[...] = jnp.cos(ang)
    sin_ref[...] = jnp.sin(ang)


def _rope_tables(positions):
    bsz, seq = positions.shape
    half = QK_ROPE // 2
    inv = ROPE_THETA ** (-np.arange(half, dtype=np.float32) / half)
    inv128 = np.zeros((1, LANE), np.float32)
    inv128[0, QK_NOPE:QK_NOPE + half] = inv
    inv128[0, QK_NOPE + half:QK_NOPE + QK_ROPE] = inv
    spec = pl.BlockSpec((None, seq, LANE), lambda b: (b, 0, 0))
    return pl.pallas_call(
        _rope_kernel,
        grid=(bsz,),
        in_specs=[pl.BlockSpec((None, seq, 1), lambda b: (b, 0, 0)), _const_spec((1, LANE))],
        out_specs=[spec, spec],
        out_shape=[jax.ShapeDtypeStruct((bsz, seq, LANE), F32)] * 2,
        name="rope",
    )(positions.reshape(bsz, seq, 1), jnp.asarray(inv128))


def _inproj_kernel(x_ref, mod_ref, w_ref, zrw_ref, zkv_ref, zq_ref, zg_ref):
    shift = mod_ref[0:1, :]
    scale = mod_ref[1:2, :]
    h = (x_ref[...] * (1.0 + scale) + shift).astype(BF16)
    col = 0
    for o_ref in (zrw_ref, zkv_ref, zq_ref, zg_ref):
        n = o_ref.shape[-1]
        o_ref[...] = _dot(h, w_ref[:, col:col + n]).astype(o_ref.dtype)
        col += n


def _inproj(x, mod3, w_all):
    bsz, seq, _ = x.shape
    tm = 512
    widths = (RW_COLS, ZKV_COLS, Q_LORA, 2 * D_MODEL)
    return pl.pallas_call(
        _inproj_kernel,
        grid=(bsz, seq // tm),
        in_specs=[pl.BlockSpec((None, tm, D_MODEL), lambda b, i: (b, i, 0)),
                  pl.BlockSpec((None, 6, D_MODEL), lambda b, i: (b, 0, 0)),
                  _const_spec(w_all.shape)],
        out_specs=[pl.BlockSpec((None, tm, n), lambda b, i: (b, i, 0)) for n in widths],
        out_shape=[jax.ShapeDtypeStruct((bsz, seq, n), BF16) for n in widths],
        compiler_params=pltpu.CompilerParams(dimension_semantics=("parallel", "parallel"),
                                             vmem_limit_bytes=VMEM_LIMIT),
        name="inproj",
    )(x, mod3, w_all)


def _softplus(x):
    return jnp.maximum(x, 0.0) + jnp.log(1.0 + jnp.exp(-jnp.abs(x)))


def _scan_kernel(*refs, rev):
    if rev:
        (zm_ref, zp_ref, zn_ref, cw_ref, vec_ref, w2_ref, a2_ref, bd_ref, tri_ref,
         yf_ref, g2_ref, o_ref,
         at_s, rt_s, bh_s, kh_s, be_s, ke_s, v_s, eg_s, h_s, y_s) = refs
    else:
        (zm_ref, zp_ref, zn_ref, cw_ref, vec_ref, w2_ref, a2_ref, bd_ref, tri_ref,
         o_ref,
         at_s, rt_s, bh_s, kh_s, be_s, ke_s, v_s, eg_s, h_s) = refs
        y_s = o_ref
    rows = zm_ref.shape[0]
    n_chunks = rows // CHUNK
    i = pl.program_id(1)
    nb = pl.num_programs(1)
    blk = nb - 1 - i if rev else i

    @pl.when(i == 0)
    def _():
        h_s[...] = jnp.zeros_like(h_s)

    z = zm_ref[...].astype(F32)
    prev_row = zp_ref[...].astype(F32)[HALO_ROWS - 1:HALO_ROWS, :] * jnp.where(blk > 0, 1.0, 0.0)
    next_row = zn_ref[...].astype(F32)[0:1, :] * jnp.where(blk < nb - 1, 1.0, 0.0)
    row_id = lax.broadcasted_iota(jnp.int32, (rows, 1), 0)
    z_dn = jnp.where(row_id == 0, prev_row, pltpu.roll(z, 1, 0))
    z_up = jnp.where(row_id == rows - 1, next_row, pltpu.roll(z, rows - 1, 0))
    zc = cw_ref[0:1, :] * z_dn + cw_ref[1:2, :] * z + cw_ref[2:3, :] * z_up

    r = zc[:, 0:RW_WIDTH]
    k = zc[:, RW_WIDTH:2 * RW_WIDTH]
    v = zc[:, 2 * RW_WIDTH:3 * RW_WIDTH]
    zw = zc[:, 3 * RW_WIDTH:3 * RW_WIDTH + LANE]
    za = zc[:, 3 * RW_WIDTH + LANE:3 * RW_WIDTH + 2 * LANE]
    w0 = vec_ref[0:1, :]
    a0 = vec_ref[1:2, :]
    k_k = vec_ref[2:3, :]
    k_a = vec_ref[3:4, :]

    w_lin = w0 + _dot(jnp.tanh(zw).astype(BF16), w2_ref[...])
    w_log = -_softplus(-w_lin) - 0.5
    lw = -jnp.exp(w_log)
    rate = jax.nn.sigmoid(a0 + _dot(za.astype(BF16), a2_ref[...]))
    kkv = k * k_k
    ssq = _dot((kkv * kkv).astype(BF16), bd_ref[...])
    kk = kkv / jnp.maximum(jnp.sqrt(ssq), L2_EPS)
    kd = k * (1.0 + (rate - 1.0) * k_a)
    av = -kk
    bv = kk * rate

    lw_hi = lw.astype(BF16)
    lw_lo = (lw - lw_hi.astype(F32)).astype(BF16)
    g_in = _dot(tri_ref[...], lw_hi) + _dot(tri_ref[...], lw_lo)
    g_ex = g_in - lw
    last = 0 if rev else CHUNK - 1
    g_tot_rows = [g_in[c * CHUNK + last:c * CHUNK + last + 1, :] for c in range(n_chunks)]
    g_tot = jnp.concatenate([jnp.broadcast_to(t, (CHUNK, RW_WIDTH)) for t in g_tot_rows], axis=0)
    e_neg = jnp.exp(-g_in)
    e_end = jnp.exp(g_tot - g_in)
    at_s[...] = av * jnp.exp(g_ex)
    rt_s[...] = r * jnp.exp(g_in)
    bh_s[...] = bv * e_neg
    kh_s[...] = kd * e_neg
    be_s[...] = bv * e_end
    ke_s[...] = kd * e_end
    v_s[...] = v
    for c in range(n_chunks):
        eg_s[c] = jnp.broadcast_to(jnp.exp(g_tot_rows[c]), (8, RW_WIDTH))

    ri = lax.broadcasted_iota(jnp.int32, (CHUNK, CHUNK), 0)
    ci = lax.broadcasted_iota(jnp.int32, (CHUNK, CHUNK), 1)
    m_strict = (ci > ri) if rev else (ci < ri)
    m_incl = (ci >= ri) if rev else (ci <= ri)
    eye = jnp.where(ri == ci, 1.0, 0.0).astype(F32)

    def chunk_body(j, carry):
        c = n_chunks - 1 - j if rev else j
        r0 = pl.multiple_of(c * CHUNK, CHUNK)
        eg_all = eg_s[c]
        for h in range(RW_HEADS):
            sl = slice(h * RW_HEAD, (h + 1) * RW_HEAD)
            at = at_s[pl.ds(r0, CHUNK), sl]
            rt = rt_s[pl.ds(r0, CHUNK), sl]
            bh = bh_s[pl.ds(r0, CHUNK), sl]
            kh = kh_s[pl.ds(r0, CHUNK), sl]
            be = be_s[pl.ds(r0, CHUNK), sl].astype(BF16)
            ke = ke_s[pl.ds(r0, CHUNK), sl].astype(BF16)
            vv = v_s[pl.ds(r0, CHUNK), sl].astype(BF16)
            eg_row = eg_all[0:1, sl]
            hst = h_s[h]

            atb = at.astype(BF16)
            rtb = rt.astype(BF16)
            bhb = bh.astype(BF16)
            khb = kh.astype(BF16)
            a_ab = jnp.where(m_strict, _dot_nt(atb, bhb), 0.0)
            a_ak = jnp.where(m_strict, _dot_nt(atb, khb), 0.0)
            a_rb = jnp.where(m_incl, _dot_nt(rtb, bhb), 0.0).astype(BF16)
            a_rk = jnp.where(m_incl, _dot_nt(rtb, khb), 0.0).astype(BF16)

            t_inv = eye + a_ab
            l_pow = a_ab
            for _ in range(5):
                lb = l_pow.astype(BF16)
                l_pow = _dot(lb, lb)
                t_inv = t_inv + _dot(t_inv.astype(BF16), l_pow.astype(BF16))
            tb = t_inv.astype(BF16)

            p_m = _dot(tb, atb)
            q_m = _dot(tb, _dot(a_ak.astype(BF16), vv).astype(BF16))
            pb = p_m.astype(BF16)
            qb = q_m.astype(BF16)
            ry = rt + _dot(a_rb, pb)
            y0 = _dot(a_rb, qb) + _dot(a_rk, vv)
            m_m = _dot_tn(be, pb)
            h0 = _dot_tn(be, qb) + _dot_tn(ke, vv)

            hb = hst.astype(BF16)
            y_s[pl.ds(r0, CHUNK), sl] = _dot(ry.astype(BF16), hb) + y0
            eg_col = jnp.sum(eye * eg_row, axis=1, keepdims=True)
            h_s[h] = hst * eg_col + _dot(m_m.astype(BF16), hb) + h0
        return carry

    lax.fori_loop(0, n_chunks, chunk_body, 0)

    if rev:
        r_k = vec_ref[4:5, :]
        lnx_g = vec_ref[5:6, :]
        lnx_b = vec_ref[6:7, :]
        y = yf_ref[...] + y_s[...]
        inv_n = 1.0 / RW_HEAD
        mu = _dot(y.astype(BF16), bd_ref[...]) * inv_n
        d = y - mu
        var = _dot((d * d).astype(BF16), bd_ref[...]) * inv_n
        yn = d * lax.rsqrt(var + GN_EPS) * lnx_g + lnx_b
        rk = r * k * r_k
        rk_hi = rk.astype(BF16)
        rk_lo = (rk - rk_hi.astype(F32)).astype(BF16)
        bonus = (_dot(rk_hi, bd_ref[...]) + _dot(rk_lo, bd_ref[...])) * v
        zg = zc[:, 3 * RW_WIDTH + 2 * LANE:RW_COLS]
        gate = _dot(jax.nn.sigmoid(zg).astype(BF16), g2_ref[...])
        o_ref[...] = ((yn + bonus) * gate).astype(o_ref.dtype)


def _scan(z_rw, conv_w, vecs, w2p, a2p, bd, tri, rev, y_fwd=None, g2=None):
    bsz, seq, _ = z_rw.shape
    rows = SCAN_ROWS
    nb = seq // rows
    hpb = rows // HALO_ROWS
    n_halo = seq // HALO_ROWS
    n_chunks = rows // CHUNK

    def blk(i):
        return nb - 1 - i if rev else i

    in_specs = [
        pl.BlockSpec((None, rows, RW_COLS), lambda b, i: (b, blk(i), 0)),
        pl.BlockSpec((None, HALO_ROWS, RW_COLS), lambda b, i: (b, jnp.maximum(blk(i) * hpb - 1, 0), 0)),
        pl.BlockSpec((None, HALO_ROWS, RW_COLS),
                     lambda b, i: (b, jnp.minimum((blk(i) + 1) * hpb, n_halo - 1), 0)),
        _const_spec(conv_w.shape), _const_spec(vecs.shape), _const_spec(w2p.shape),
        _const_spec(a2p.shape), _const_spec(bd.shape), _const_spec(tri.shape),
    ]
    args = [z_rw, z_rw, z_rw, conv_w, vecs, w2p, a2p, bd, tri]
    scratch = [pltpu.VMEM((rows, RW_WIDTH), F32) for _ in range(7)]
    scratch += [pltpu.VMEM((n_chunks, 8, RW_WIDTH), F32),
                pltpu.VMEM((RW_HEADS, RW_HEAD, RW_HEAD), F32)]
    if rev:
        in_specs += [pl.BlockSpec((None, rows, RW_WIDTH), lambda b, i: (b, blk(i), 0)),
                     _const_spec(g2.shape)]
        args += [y_fwd, g2]
        scratch += [pltpu.VMEM((rows, RW_WIDTH), F32)]
        out_dtype = BF16
    else:
        out_dtype = F32
    return pl.pallas_call(
        functools.partial(_scan_kernel, rev=rev),
        grid=(bsz, nb),
        in_specs=in_specs,
        out_specs=pl.BlockSpec((None, rows, RW_WIDTH), lambda b, i: (b, blk(i), 0)),
        out_shape=jax.ShapeDtypeStruct((bsz, seq, RW_WIDTH), out_dtype),
        scratch_shapes=scratch,
        compiler_params=pltpu.CompilerParams(dimension_semantics=("parallel", "arbitrary"),
                                             vmem_limit_bytes=VMEM_LIMIT),
        name="scan_bwd" if rev else "scan_fwd",
    )(*args)


def _rms(xf, g):
    return xf * lax.rsqrt(jnp.mean(xf * xf, axis=-1, keepdims=True) + RMS_EPS) * g


def _attn_kernel(zkv_ref, zq_ref, cos_ref, sin_ref, gq_ref, gkv_ref, wq1_ref, wq2_ref, wkv_ref,
                 o_ref, k_s, v_s):
    seq = zkv_ref.shape[0]
    tq = zq_ref.shape[0]
    kt = 256
    i = pl.program_id(1)
    kw = MLA_HEADS * HEAD_PAD

    @pl.when(i == 0)
    def _():
        def kv_body(t, carry):
            r0 = pl.multiple_of(t * kt, kt)
            zk = zkv_ref[pl.ds(r0, kt), :].astype(F32)
            kvn = _rms(zk[:, 0:KV_LORA], gkv_ref[...])
            kv = _dot(kvn.astype(BF16), wkv_ref[...])
            cs = cos_ref[pl.ds(r0, kt), :]
            sn = sin_ref[pl.ds(r0, kt), :]
            kpe = zk[:, KV_LORA:KV_LORA + LANE] * cs + zk[:, KV_LORA + LANE:KV_LORA + 2 * LANE] * sn
            for h in range(MLA_HEADS):
                sl = slice(h * HEAD_PAD, (h + 1) * HEAD_PAD)
                k_s[pl.ds(r0, kt), sl] = (kv[:, sl] + kpe).astype(BF16)
            v_s[pl.ds(r0, kt), :] = kv[:, kw:2 * kw].astype(BF16)
            return carry
        lax.fori_loop(0, seq // kt, kv_body, 0)

    q0 = pl.multiple_of(i * tq, tq)
    qn = _rms(zq_ref[...].astype(F32), gq_ref[...]).astype(BF16)
    q1 = _dot(qn, wq1_ref[...])
    q2 = _dot(qn, wq2_ref[...])
    cs = cos_ref[pl.ds(q0, tq), :]
    sn = sin_ref[pl.ds(q0, tq), :]
    scale = (QK_NOPE + QK_ROPE) ** -0.5
    for h in range(MLA_HEADS):
        sl = slice(h * HEAD_PAD, (h + 1) * HEAD_PAD)
        qh = ((q1[:, sl] * cs + q2[:, sl] * sn) * scale).astype(BF16)
        s = _dot_nt(qh, k_s[:, sl])
        m = jnp.max(s, axis=-1, keepdims=True)
        p = jnp.exp(s - m)
        l = jnp.sum(p, axis=-1, keepdims=True)
        o = _dot(p.astype(BF16), v_s[:, sl])
        o_ref[:, h * V_HEAD:(h + 1) * V_HEAD] = (o[:, 0:V_HEAD] / l).astype(o_ref.dtype)


def _attn(z_kv, z_q, cos_t, sin_t, gq, gkv, wq1, wq2, wkv):
    bsz, seq, _ = z_kv.shape
    tq = 256
    kw = MLA_HEADS * HEAD_PAD
    return pl.pallas_call(
        _attn_kernel,
        grid=(bsz, seq // tq),
        in_specs=[pl.BlockSpec((None, seq, ZKV_COLS), lambda b, i: (b, 0, 0)),
                  pl.BlockSpec((None, tq, Q_LORA), lambda b, i: (b, i, 0)),
                  pl.BlockSpec((None, seq, LANE), lambda b, i: (b, 0, 0)),
                  pl.BlockSpec((None, seq, LANE), lambda b, i: (b, 0, 0)),
                  _const_spec(gq.shape), _const_spec(gkv.shape), _const_spec(wq1.shape),
                  _const_spec(wq2.shape), _const_spec(wkv.shape)],
        out_specs=pl.BlockSpec((None, tq, MLA_HEADS * V_HEAD), lambda b, i: (b, i, 0)),
        out_shape=jax.ShapeDtypeStruct((bsz, seq, MLA_HEADS * V_HEAD), BF16),
        scratch_shapes=[pltpu.VMEM((seq, kw), BF16), pltpu.VMEM((seq, kw), BF16)],
        compiler_params=pltpu.CompilerParams(dimension_semantics=("parallel", "arbitrary"),
                                             vmem_limit_bytes=VMEM_LIMIT),
        name="attn",
    )(z_kv, z_q, cos_t, sin_t, gq, gkv, wq1, wq2, wkv)


def _mix_kernel(yrw_ref, ymla_ref, ga_ref, gb_ref, x_ref, mod_ref, wr_ref, wm_ref, wo_ref, ln_ref, o_ref):
    gate1 = mod_ref[2:3, :]
    br_rw = _dot(yrw_ref[...], wr_ref[...])
    br_mla = _dot(ymla_ref[...], wm_ref[...])
    mixed = (jax.nn.sigmoid(ga_ref[...].astype(F32)) * br_rw
             + jax.nn.sigmoid(gb_ref[...].astype(F32)) * br_mla)
    out = _dot(mixed.astype(BF16), wo_ref[...])
    t = DN_ALPHA * x_ref[...] + (1.0 + gate1) * out
    o_ref[...] = _layer_norm(t, ln_ref[0:1, :], ln_ref[1:2, :])


def _mix(y_rw, y_mla, z_g, x, mod3, wr, wm, wo, ln):
    bsz, seq, _ = x.shape
    tm = 512
    row = lambda n: pl.BlockSpec((None, tm, n), lambda b, i: (b, i, 0))
    return pl.pallas_call(
        _mix_kernel,
        grid=(bsz, seq // tm),
        in_specs=[row(RW_WIDTH), row(MLA_HEADS * V_HEAD),
                  pl.BlockSpec((None, tm, D_MODEL), lambda b, i: (b, i, 0)),
                  pl.BlockSpec((None, tm, D_MODEL), lambda b, i: (b, i, 1)),
                  row(D_MODEL),
                  pl.BlockSpec((None, 6, D_MODEL), lambda b, i: (b, 0, 0)),
                  _const_spec(wr.shape), _const_spec(wm.shape), _const_spec(wo.shape),
                  _const_spec(ln.shape)],
        out_specs=row(D_MODEL),
        out_shape=jax.ShapeDtypeStruct((bsz, seq, D_MODEL), F32),
        compiler_params=pltpu.CompilerParams(dimension_semantics=("parallel", "parallel"),
                                             vmem_limit_bytes=VMEM_LIMIT),
        name="mix",
    )(y_rw, y_mla, z_g, z_g, x, mod3, wr, wm, wo, ln)


def _ffn_kernel(x_ref, mod_ref, w1_ref, w2_ref, ln_ref, o_ref):
    shift = mod_ref[3:4, :]
    scale = mod_ref[4:5, :]
    gate2 = mod_ref[5:6, :]
    x1 = x_ref[...]
    h = (x1 * (1.0 + scale) + shift).astype(BF16)
    kc = 1024
    acc = jnp.zeros(x1.shape, F32)
    for c in range(D_FF // kc):
        u = jnp.maximum(_dot(h, w1_ref[:, c * kc:(c + 1) * kc]), 0.0)
        acc = acc + _dot((u * u).astype(BF16), w2_ref[c * kc:(c + 1) * kc, :])
    t = DN_ALPHA * x1 + (1.0 + gate2) * acc
    o_ref[...] = _layer_norm(t, ln_ref[0:1, :], ln_ref[1:2, :])


def _ffn(x1, mod3, w1, w2, ln):
    bsz, seq, _ = x1.shape
    tm = 512
    row = pl.BlockSpec((None, tm, D_MODEL), lambda b, i: (b, i, 0))
    return pl.pallas_call(
        _ffn_kernel,
        grid=(bsz, seq // tm),
        in_specs=[row, pl.BlockSpec((None, 6, D_MODEL), lambda b, i: (b, 0, 0)),
                  _const_spec(w1.shape), _const_spec(w2.shape), _const_spec(ln.shape)],
        out_specs=row,
        out_shape=jax.ShapeDtypeStruct((bsz, seq, D_MODEL), F32),
        compiler_params=pltpu.CompilerParams(dimension_semantics=("parallel", "parallel"),
                                             vmem_limit_bytes=VMEM_LIMIT),
        name="ffn",
    )(x1, mod3, w1, w2, ln)


def _pad_cols(w, left, total):
    return jnp.pad(w, ((0, 0), (left, total - left - w.shape[1])))


def _inproj_weight(w_in):
    o = 0
    w_rw = w_in[:, o:o + RW_COLS]; o += RW_COLS
    w_q = w_in[:, o:o + Q_LORA]; o += Q_LORA
    w_kv = w_in[:, o:o + KV_LORA]; o += KV_LORA
    w_kr = w_in[:, o:o + QK_ROPE]; o += QK_ROPE
    w_g = w_in[:, o:o + 2 * D_MODEL]
    half = QK_ROPE // 2
    w_kr_rot = jnp.concatenate([-w_kr[:, half:], w_kr[:, :half]], axis=1)
    w_all = jnp.concatenate([w_rw, w_kv, _pad_cols(w_kr, QK_NOPE, LANE), _pad_cols(w_kr_rot, QK_NOPE, LANE),
                             w_q, w_g], axis=1)
    return w_all.astype(BF16)


def _mla_weights(w_uq, w_ukv):
    half = QK_ROPE // 2
    q = w_uq.reshape(Q_LORA, MLA_HEADS, QK_NOPE + QK_ROPE)
    q_nope, q_1, q_2 = q[..., :QK_NOPE], q[..., QK_NOPE:QK_NOPE + half], q[..., QK_NOPE + half:]
    zpad = jnp.zeros((Q_LORA, MLA_HEADS, HEAD_PAD - QK_NOPE - QK_ROPE), F32)
    wq1 = jnp.concatenate([q_nope, q_1, q_2, zpad], axis=-1).reshape(Q_LORA, MLA_HEADS * HEAD_PAD)
    wq2 = jnp.concatenate([jnp.zeros_like(q_nope), -q_2, q_1, zpad], axis=-1).reshape(Q_LORA, MLA_HEADS * HEAD_PAD)
    kv = w_ukv.reshape(KV_LORA, MLA_HEADS, QK_NOPE + V_HEAD)
    zhalf = jnp.zeros((KV_LORA, MLA_HEADS, HEAD_PAD - QK_NOPE), F32)
    wk = jnp.concatenate([kv[..., :QK_NOPE], zhalf], axis=-1).reshape(KV_LORA, MLA_HEADS * HEAD_PAD)
    wv = jnp.concatenate([kv[..., QK_NOPE:], zhalf], axis=-1).reshape(KV_LORA, MLA_HEADS * HEAD_PAD)
    return wq1.astype(BF16), wq2.astype(BF16), jnp.concatenate([wk, wv], axis=1).astype(BF16)


def _scan_constants(rev):
    rows = np.arange(SCAN_ROWS)
    same = (rows[:, None] // CHUNK) == (rows[None, :] // CHUNK)
    tri = same & ((rows[None, :] >= rows[:, None]) if rev else (rows[None, :] <= rows[:, None]))
    ch = np.arange(RW_WIDTH)
    bd = (ch[:, None] // RW_HEAD) == (ch[None, :] // RW_HEAD)
    return jnp.asarray(tri, BF16), jnp.asarray(bd, BF16)


def _lora_pair(w, d):
    zero = jnp.zeros_like(w[0])
    return jnp.concatenate([zero, w[1]] if d else [w[0], zero], axis=0).astype(BF16)


def kernel(x, c, positions, w_ada, b_ada, w_in, rw_conv, rw_w0, rw_w2, rw_a0, rw_a2, rw_k_k, rw_k_a, rw_r_k, rw_g2, rw_lnx_g, rw_lnx_b, mla_q_norm_g, mla_kv_norm_g, mla_w_uq, mla_w_ukv, w_br_rwkv, w_br_mla, w_out, ln1_g, ln1_b, w_ff1, w_ff2, ln2_g, ln2_b):
    bsz, seq, _ = x.shape
    cos_t, sin_t = _rope_tables(positions)
    for l in range(DEPTH):
        mod3 = _ada(c, w_ada[l], b_ada[l]).reshape(bsz, 6, D_MODEL)
        z_rw, z_kv, z_q, z_g = _inproj(x, mod3, _inproj_weight(w_in[l]))

        zero = jnp.zeros((RW_WIDTH,), F32)
        y_rw = None
        for d in (0, 1):
            vecs = jnp.stack([rw_w0[l, d], rw_a0[l, d], rw_k_k[l], rw_k_a[l], rw_r_k[l],
                              rw_lnx_g[l], rw_lnx_b[l], zero])
            tri, bd = _scan_constants(bool(d))
            y_rw = _scan(z_rw, rw_conv[l], vecs, _lora_pair(rw_w2[l], d), _lora_pair(rw_a2[l], d), bd, tri,
                         rev=bool(d), y_fwd=y_rw, g2=rw_g2[l].astype(BF16))

        wq1, wq2, wkv = _mla_weights(mla_w_uq[l], mla_w_ukv[l])
        y_mla = _attn(z_kv, z_q, cos_t, sin_t, mla_q_norm_g[l].reshape(1, Q_LORA),
                      mla_kv_norm_g[l].reshape(1, KV_LORA), wq1, wq2, wkv)

        x = _mix(y_rw, y_mla, z_g, x, mod3, w_br_rwkv[l].astype(BF16), w_br_mla[l].astype(BF16),
                 w_out[l].astype(BF16), jnp.stack([ln1_g[l], ln1_b[l]]))
        x = _ffn(x, mod3, w_ff1[l].astype(BF16), w_ff2[l].astype(BF16), jnp.stack([ln2_g[l], ln2_b[l]]))
    return x
```

```python
import functools

import numpy as np
import jax
import jax.numpy as jnp
from jax import lax
from jax.experimental import pallas as pl
from jax.experimental.pallas import tpu as pltpu

F32 = jnp.float32
BF16 = jnp.bfloat16

D_MODEL = 1024
RW_HEADS = 8
RW_HEAD = 64
RW_WIDTH = RW_HEADS * RW_HEAD
DECAY_LORA = 64
AAA_LORA = 64
GATE_LORA = 128
RW_COLS = 3 * RW_WIDTH + 2 * DECAY_LORA + 2 * AAA_LORA + GATE_LORA
MLA_HEADS = 8
QK_NOPE = 64
QK_ROPE = 32
V_HEAD = 64
Q_LORA = 384
KV_LORA = 256
ROPE_THETA = 10000.0
D_FF = 4 * D_MODEL
LN_EPS = 1e-5
RMS_EPS = 1e-6
GN_EPS = 64e-5
L2_EPS = 1e-12
DEPTH = 1
DN_ALPHA = (2.0 * DEPTH) ** 0.25

LANE = 128
CHUNK = 64
SCAN_ROWS = 256
SCAN_PAR_CHUNKS = 4
HALO_ROWS = 16
HEAD_PAD = 128
ZKV_COLS = KV_LORA + 2 * LANE
VMEM_LIMIT = 56 * 1024 * 1024


def _dot(a, b):
    return jnp.dot(a, b, preferred_element_type=F32)


def _dot_nt(a, b):
    return lax.dot_general(a, b, (((1,), (1,)), ((), ())), preferred_element_type=F32)


def _dot_tn(a, b):
    return lax.dot_general(a, b, (((0,), (0,)), ((), ())), preferred_element_type=F32)


def _const_spec(shape):
    zeros = (0,) * len(shape)
    return pl.BlockSpec(shape, lambda *_: zeros)


def _layer_norm(t, g, b):
    mu = jnp.mean(t, axis=-1, keepdims=True)
    d = t - mu
    var = jnp.mean(d * d, axis=-1, keepdims=True)
    return d * lax.rsqrt(var + LN_EPS) * g + b


def _ada_kernel(c_ref, w_ref, b_ref, o_ref):
    c = c_ref[...]
    act = c * jax.nn.sigmoid(c)
    o_ref[...] = _dot(act.astype(BF16), w_ref[...].astype(BF16)) + b_ref[...]


def _ada(c, w, b):
    bsz = c.shape[0]
    n = w.shape[1]
    tn = 1536
    return pl.pallas_call(
        _ada_kernel,
        grid=(n // tn,),
        in_specs=[_const_spec((bsz, D_MODEL)),
                  pl.BlockSpec((D_MODEL, tn), lambda j: (0, j)),
                  pl.BlockSpec((1, tn), lambda j: (0, j))],
        out_specs=pl.BlockSpec((bsz, tn), lambda j: (0, j)),
        out_shape=jax.ShapeDtypeStruct((bsz, n), F32),
        name="ada",
    )(c, w, b.reshape(1, n))


def _rope_kernel(pos_ref, inv_ref, cos_ref, sin_ref):
    ang = pos_ref[...].astype(F32) * inv_ref[...]
    cos_ref[...] = jnp.cos(ang)
    sin_ref[...] = jnp.sin(ang)


def _rope_tables(positions):
    bsz, seq = positions.shape
    half = QK_ROPE // 2
    inv = ROPE_THETA ** (-np.arange(half, dtype=np.float32) / half)
    inv128 = np.zeros((1, LANE), np.float32)
    inv128[0, QK_NOPE:QK_NOPE + half] = inv
    inv128[0, QK_NOPE + half:QK_NOPE + QK_ROPE] = inv
    spec = pl.BlockSpec((None, seq, LANE), lambda b: (b, 0, 0))
    return pl.pallas_call(
        _rope_kernel,
        grid=(bsz,),
        in_specs=[pl.BlockSpec((None, seq, 1), lambda b: (b, 0, 0)), _const_spec((1, LANE))],
        out_specs=[spec, spec],
        out_shape=[jax.ShapeDtypeStruct((bsz, seq, LANE), F32)] * 2,
        name="rope",
    )(positions.reshape(bsz, seq, 1), jnp.asarray(inv128))


def _inproj_kernel(x_ref, mod_ref, w_ref, zrw_ref, zkv_ref, zq_ref, zg_ref):
    shift = mod_ref[0:1, :]
    scale = mod_ref[1:2, :]
    h = (x_ref[...] * (1.0 + scale) + shift).astype(BF16)
    col = 0
    for o_ref in (zrw_ref, zkv_ref, zq_ref, zg_ref):
        n = o_ref.shape[-1]
        o_ref[...] = _dot(h, w_ref[:, col:col + n]).astype(o_ref.dtype)
        col += n


def _inproj(x, mod3, w_all):
    bsz, seq, _ = x.shape
    tm = 512
    widths = (RW_COLS, ZKV_COLS, Q_LORA, 2 * D_MODEL)
    return pl.pallas_call(
        _inproj_kernel,
        grid=(bsz, seq // tm),
        in_specs=[pl.BlockSpec((None, tm, D_MODEL), lambda b, i: (b, i, 0)),
                  pl.BlockSpec((None, 6, D_MODEL), lambda b, i: (b, 0, 0)),
                  _const_spec(w_all.shape)],
        out_specs=[pl.BlockSpec((None, tm, n), lambda b, i: (b, i, 0)) for n in widths],
        out_shape=[jax.ShapeDtypeStruct((bsz, seq, n), BF16) for n in widths],
        compiler_params=pltpu.CompilerParams(dimension_semantics=("parallel", "parallel"),
                                             vmem_limit_bytes=VMEM_LIMIT),
        name="inproj",
    )(x, mod3, w_all)


def _softplus(x):
    return jnp.maximum(x, 0.0) + jnp.log(1.0 + jnp.exp(-jnp.abs(x)))


def _scan_kernel(*refs, rev):
    if rev:
        (zm_ref, zp_ref, zn_ref, cw_ref, vec_ref, w2_ref, a2_ref, bd_ref, tri_ref,
         yf_ref, g2_ref, o_ref,
         at_s, rt_s, bh_s, kh_s, be_s, ke_s, v_s, eg_s, h_s, y_s) = refs
    else:
        (zm_ref, zp_ref, zn_ref, cw_ref, vec_ref, w2_ref, a2_ref, bd_ref, tri_ref,
         o_ref,
         at_s, rt_s, bh_s, kh_s, be_s, ke_s, v_s, eg_s, h_s) = refs
        y_s = o_ref
    rows = zm_ref.shape[0]
    n_chunks = rows // CHUNK
    i = pl.program_id(1)
    nb = pl.num_programs(1)
    blk = nb - 1 - i if rev else i

    @pl.when(i == 0)
    def _():
        h_s[...] = jnp.zeros_like(h_s)

    z = zm_ref[...].astype(F32)
    prev_row = zp_ref[...].astype(F32)[HALO_ROWS - 1:HALO_ROWS, :] * jnp.where(blk > 0, 1.0, 0.0)
    next_row = zn_ref[...].astype(F32)[0:1, :] * jnp.where(blk < nb - 1, 1.0, 0.0)
    row_id = lax.broadcasted_iota(jnp.int32, (rows, 1), 0)
    z_dn = jnp.where(row_id == 0, prev_row, pltpu.roll(z, 1, 0))
    z_up = jnp.where(row_id == rows - 1, next_row, pltpu.roll(z, rows - 1, 0))
    zc = cw_ref[0:1, :] * z_dn + cw_ref[1:2, :] * z + cw_ref[2:3, :] * z_up

    r = zc[:, 0:RW_WIDTH]
    k = zc[:, RW_WIDTH:2 * RW_WIDTH]
    v = zc[:, 2 * RW_WIDTH:3 * RW_WIDTH]
    zw = zc[:, 3 * RW_WIDTH:3 * RW_WIDTH + LANE]
    za = zc[:, 3 * RW_WIDTH + LANE:3 * RW_WIDTH + 2 * LANE]
    w0 = vec_ref[0:1, :]
    a0 = vec_ref[1:2, :]
    k_k = vec_ref[2:3, :]
    k_a = vec_ref[3:4, :]

    w_lin = w0 + _dot(jnp.tanh(zw).astype(BF16), w2_ref[...])
    w_log = -_softplus(-w_lin) - 0.5
    lw = -jnp.exp(w_log)
    rate = jax.nn.sigmoid(a0 + _dot(za.astype(BF16), a2_ref[...]))
    kkv = k * k_k
    ssq = _dot((kkv * kkv).astype(BF16), bd_ref[...])
    kk = kkv / jnp.maximum(jnp.sqrt(ssq), L2_EPS)
    kd = k * (1.0 + (rate - 1.0) * k_a)
    av = -kk
    bv = kk * rate

    lw_hi = lw.astype(BF16)
    lw_lo = (lw - lw_hi.astype(F32)).astype(BF16)
    g_in = _dot(tri_ref[...], lw_hi) + _dot(tri_ref[...], lw_lo)
    g_ex = g_in - lw
    last = 0 if rev else CHUNK - 1
    g_tot_rows = [g_in[c * CHUNK + last:c * CHUNK + last + 1, :] for c in range(n_chunks)]
    g_tot = jnp.concatenate([jnp.broadcast_to(t, (CHUNK, RW_WIDTH)) for t in g_tot_rows], axis=0)
    e_neg = jnp.exp(-g_in)
    e_end = jnp.exp(g_tot - g_in)
    at_s[...] = av * jnp.exp(g_ex)
    rt_s[...] = r * jnp.exp(g_in)
    bh_s[...] = bv * e_neg
    kh_s[...] = kd * e_neg
    be_s[...] = bv * e_end
    ke_s[...] = kd * e_end
    v_s[...] = v
    for c in range(n_chunks):
        eg_s[c] = jnp.broadcast_to(jnp.exp(g_tot_rows[c]), (8, RW_WIDTH))

    ri = lax.broadcasted_iota(jnp.int32, (CHUNK, CHUNK), 0)
    ci = lax.broadcasted_iota(jnp.int32, (CHUNK, CHUNK), 1)
    m_strict = (ci > ri) if rev else (ci < ri)
    m_incl = (ci >= ri) if rev else (ci <= ri)
    eye = jnp.where(ri == ci, 1.0, 0.0).astype(F32)

    par = SCAN_PAR_CHUNKS
    n_groups = n_chunks // par
    heads = range(RW_HEADS)
    items = [(cc, h) for cc in range(par) for h in heads]
    n_it = range(len(items))

    def group_body(j, carry):
        g = n_groups - 1 - j if rev else j
        base = g * par

        def ld(ref, it):
            cc, h = items[it]
            r0 = pl.multiple_of((base + cc) * CHUNK, CHUNK)
            return ref[pl.ds(r0, CHUNK), h * RW_HEAD:(h + 1) * RW_HEAD]

        rt = [ld(rt_s, it) for it in n_it]
        atb = [ld(at_s, it).astype(BF16) for it in n_it]
        lhs = [jnp.concatenate([atb[it], rt[it].astype(BF16)], axis=0) for it in n_it]
        a_b = [_dot_nt(lhs[it], ld(bh_s, it).astype(BF16)) for it in n_it]
        a_k = [_dot_nt(lhs[it], ld(kh_s, it).astype(BF16)) for it in n_it]
        a_ab = [jnp.where(m_strict, a_b[it][0:CHUNK], 0.0) for it in n_it]
        a_ak = [jnp.where(m_strict, a_k[it][0:CHUNK], 0.0).astype(BF16) for it in n_it]
        a_rb = [jnp.where(m_incl, a_b[it][CHUNK:], 0.0).astype(BF16) for it in n_it]
        a_rk = [jnp.where(m_incl, a_k[it][CHUNK:], 0.0).astype(BF16) for it in n_it]
        vv = [ld(v_s, it).astype(BF16) for it in n_it]
        akv = [_dot(a_ak[it], vv[it]).astype(BF16) for it in n_it]

        t_inv = [eye + a_ab[it] for it in n_it]
        lb = [a_ab[it].astype(BF16) for it in n_it]
        l_pow = [_dot(lb[it], lb[it]) for it in n_it]
        for _ in range(4):
            lb = [l_pow[it].astype(BF16) for it in n_it]
            st = [_dot(jnp.concatenate([t_inv[it].astype(BF16), lb[it]], axis=0), lb[it]) for it in n_it]
            t_inv = [t_inv[it] + st[it][0:CHUNK] for it in n_it]
            l_pow = [st[it][CHUNK:] for it in n_it]
        t_inv = [t_inv[it] + _dot(t_inv[it].astype(BF16), l_pow[it].astype(BF16)) for it in n_it]
        tb = [t_inv[it].astype(BF16) for it in n_it]

        pb = [_dot(tb[it], atb[it]).astype(BF16) for it in n_it]
        qb = [_dot(tb[it], akv[it]).astype(BF16) for it in n_it]
        ry = [(rt[it] + _dot(a_rb[it], pb[it])).astype(BF16) for it in n_it]
        y0 = [_dot(a_rb[it], qb[it]) + _dot(a_rk[it], vv[it]) for it in n_it]
        be = [ld(be_s, it).astype(BF16) for it in n_it]
        ke = [ld(ke_s, it).astype(BF16) for it in n_it]
        m_m = [_dot_tn(be[it], pb[it]).astype(BF16) for it in n_it]
        h0 = [_dot_tn(be[it], qb[it]) + _dot_tn(ke[it], vv[it]) for it in n_it]
        ry_m = [jnp.concatenate([ry[it], m_m[it]], axis=0) for it in n_it]

        hst = [h_s[h] for h in heads]
        for cc in (range(par - 1, -1, -1) if rev else range(par)):
            eg_all = eg_s[base + cc]
            r0 = pl.multiple_of((base + cc) * CHUNK, CHUNK)
            for h in heads:
                it = cc * RW_HEADS + h
                sl = slice(h * RW_HEAD, (h + 1) * RW_HEAD)
                yh = _dot(ry_m[it], hst[h].astype(BF16))
                y_s[pl.ds(r0, CHUNK), sl] = yh[0:CHUNK] + y0[it]
                eg_col = jnp.sum(eye * eg_all[0:1, sl], axis=1, keepdims=True)
                hst[h] = hst[h] * eg_col + yh[CHUNK:] + h0[it]
        for h in heads:
            h_s[h] = hst[h]
        return carry

    if n_groups == 1:
        group_body(0, 0)
    else:
        lax.fori_loop(0, n_groups, group_body, 0)

    if rev:
        r_k = vec_ref[4:5, :]
        lnx_g = vec_ref[5:6, :]
        lnx_b = vec_ref[6:7, :]
        y = yf_ref[...] + y_s[...]
        inv_n = 1.0 / RW_HEAD
        mu = _dot(y.astype(BF16), bd_ref[...]) * inv_n
        d = y - mu
        var = _dot((d * d).astype(BF16), bd_ref[...]) * inv_n
        yn = d * lax.rsqrt(var + GN_EPS) * lnx_g + lnx_b
        rk = r * k * r_k
        rk_hi = rk.astype(BF16)
        rk_lo = (rk - rk_hi.astype(F32)).astype(BF16)
        bonus = (_dot(rk_hi, bd_ref[...]) + _dot(rk_lo, bd_ref[...])) * v
        zg = zc[:, 3 * RW_WIDTH + 2 * LANE:RW_COLS]
        gate = _dot(jax.nn.sigmoid(zg).astype(BF16), g2_ref[...])
        o_ref[...] = ((yn + bonus) * gate).astype(o_ref.dtype)


def _scan(z_rw, conv_w, vecs, w2p, a2p, bd, tri, rev, y_fwd=None, g2=None):
    bsz, seq, _ = z_rw.shape
    rows = SCAN_ROWS
    nb = seq // rows
    hpb = rows // HALO_ROWS
    n_halo = seq // HALO_ROWS
    n_chunks = rows // CHUNK

    def blk(i):
        return nb - 1 - i if rev else i

    in_specs = [
        pl.BlockSpec((None, rows, RW_COLS), lambda b, i: (b, blk(i), 0)),
        pl.BlockSpec((None, HALO_ROWS, RW_COLS), lambda b, i: (b, jnp.maximum(blk(i) * hpb - 1, 0), 0)),
        pl.BlockSpec((None, HALO_ROWS, RW_COLS),
                     lambda b, i: (b, jnp.minimum((blk(i) + 1) * hpb, n_halo - 1), 0)),
        _const_spec(conv_w.shape), _const_spec(vecs.shape), _const_spec(w2p.shape),
        _const_spec(a2p.shape), _const_spec(bd.shape), _const_spec(tri.shape),
    ]
    args = [z_rw, z_rw, z_rw, conv_w, vecs, w2p, a2p, bd, tri]
    scratch = [pltpu.VMEM((rows, RW_WIDTH), F32) for _ in range(7)]
    scratch += [pltpu.VMEM((n_chunks, 8, RW_WIDTH), F32),
                pltpu.VMEM((RW_HEADS, RW_HEAD, RW_HEAD), F32)]
    if rev:
        in_specs += [pl.BlockSpec((None, rows, RW_WIDTH), lambda b, i: (b, blk(i), 0)),
                     _const_spec(g2.shape)]
        args += [y_fwd, g2]
        scratch += [pltpu.VMEM((rows, RW_WIDTH), F32)]
        out_dtype = BF16
    else:
        out_dtype = F32
    return pl.pallas_call(
        functools.partial(_scan_kernel, rev=rev),
        grid=(bsz, nb),
        in_specs=in_specs,
        out_specs=pl.BlockSpec((None, rows, RW_WIDTH), lambda b, i: (b, blk(i), 0)),
        out_shape=jax.ShapeDtypeStruct((bsz, seq, RW_WIDTH), out_dtype),
        scratch_shapes=scratch,
        compiler_params=pltpu.CompilerParams(dimension_semantics=("parallel", "arbitrary"),
                                             vmem_limit_bytes=VMEM_LIMIT),
        name="scan_bwd" if rev else "scan_fwd",
    )(*args)


def _rms(xf, g):
    return xf * lax.rsqrt(jnp.mean(xf * xf, axis=-1, keepdims=True) + RMS_EPS) * g


def _attn_kernel(zkv_ref, zq_ref, cos_ref, sin_ref, gq_ref, gkv_ref, wq1_ref, wq2_ref, wkv_ref,
                 o_ref, k_s, v_s):
    seq = zkv_ref.shape[0]
    tq = zq_ref.shape[0]
    kt = 256
    i = pl.program_id(1)
    kw = MLA_HEADS * HEAD_PAD

    @pl.when(i == 0)
    def _():
        def kv_body(t, carry):
            r0 = pl.multiple_of(t * kt, kt)
            zk = zkv_ref[pl.ds(r0, kt), :].astype(F32)
            kvn = _rms(zk[:, 0:KV_LORA], gkv_ref[...])
            kv = _dot(kvn.astype(BF16), wkv_ref[...])
            cs = cos_ref[pl.ds(r0, kt), :]
            sn = sin_ref[pl.ds(r0, kt), :]
            kpe = zk[:, KV_LORA:KV_LORA + LANE] * cs + zk[:, KV_LORA + LANE:KV_LORA + 2 * LANE] * sn
            for h in range(MLA_HEADS):
                sl = slice(h * HEAD_PAD, (h + 1) * HEAD_PAD)
                k_s[pl.ds(r0, kt), sl] = (kv[:, sl] + kpe).astype(BF16)
            v_s[pl.ds(r0, kt), :] = kv[:, kw:2 * kw].astype(BF16)
            return carry
        lax.fori_loop(0, seq // kt, kv_body, 0)

    q0 = pl.multiple_of(i * tq, tq)
    qn = _rms(zq_ref[...].astype(F32), gq_ref[...]).astype(BF16)
    q1 = _dot(qn, wq1_ref[...])
    q2 = _dot(qn, wq2_ref[...])
    cs = cos_ref[pl.ds(q0, tq), :]
    sn = sin_ref[pl.ds(q0, tq), :]
    scale = (QK_NOPE + QK_ROPE) ** -0.5
    for h in range(MLA_HEADS):
        sl = slice(h * HEAD_PAD, (h + 1) * HEAD_PAD)
        qh = ((q1[:, sl] * cs + q2[:, sl] * sn) * scale).astype(BF16)
        s = _dot_nt(qh, k_s[:, sl])
        m = jnp.max(s, axis=-1, keepdims=True)
        p = jnp.exp(s - m)
        l = jnp.sum(p, axis=-1, keepdims=True)
        o = _dot(p.astype(BF16), v_s[:, sl])
        o_ref[:, h * V_HEAD:(h + 1) * V_HEAD] = (o[:, 0:V_HEAD] / l).astype(o_ref.dtype)


def _attn(z_kv, z_q, cos_t, sin_t, gq, gkv, wq1, wq2, wkv):
    bsz, seq, _ = z_kv.shape
    tq = 256
    kw = MLA_HEADS * HEAD_PAD
    return pl.pallas_call(
        _attn_kernel,
        grid=(bsz, seq // tq),
        in_specs=[pl.BlockSpec((None, seq, ZKV_COLS), lambda b, i: (b, 0, 0)),
                  pl.BlockSpec((None, tq, Q_LORA), lambda b, i: (b, i, 0)),
                  pl.BlockSpec((None, seq, LANE), lambda b, i: (b, 0, 0)),
                  pl.BlockSpec((None, seq, LANE), lambda b, i: (b, 0, 0)),
                  _const_spec(gq.shape), _const_spec(gkv.shape), _const_spec(wq1.shape),
                  _const_spec(wq2.shape), _const_spec(wkv.shape)],
        out_specs=pl.BlockSpec((None, tq, MLA_HEADS * V_HEAD), lambda b, i: (b, i, 0)),
        out_shape=jax.ShapeDtypeStruct((bsz, seq, MLA_HEADS * V_HEAD), BF16),
        scratch_shapes=[pltpu.VMEM((seq, kw), BF16), pltpu.VMEM((seq, kw), BF16)],
        compiler_params=pltpu.CompilerParams(dimension_semantics=("parallel", "arbitrary"),
                                             vmem_limit_bytes=VMEM_LIMIT),
        name="attn",
    )(z_kv, z_q, cos_t, sin_t, gq, gkv, wq1, wq2, wkv)


def _mix_kernel(yrw_ref, ymla_ref, ga_ref, gb_ref, x_ref, mod_ref, wr_ref, wm_ref, wo_ref, ln_ref, o_ref):
    gate1 = mod_ref[2:3, :]
    br_rw = _dot(yrw_ref[...], wr_ref[...])
    br_mla = _dot(ymla_ref[...], wm_ref[...])
    mixed = (jax.nn.sigmoid(ga_ref[...].astype(F32)) * br_rw
             + jax.nn.sigmoid(gb_ref[...].astype(F32)) * br_mla)
    out = _dot(mixed.astype(BF16), wo_ref[...])
    t = DN_ALPHA * x_ref[...] + (1.0 + gate1) * out
    o_ref[...] = _layer_norm(t, ln_ref[0:1, :], ln_ref[1:2, :])


def _mix(y_rw, y_mla, z_g, x, mod3, wr, wm, wo, ln):
    bsz, seq, _ = x.shape
    tm = 512
    row = lambda n: pl.BlockSpec((None, tm, n), lambda b, i: (b, i, 0))
    return pl.pallas_call(
        _mix_kernel,
        grid=(bsz, seq // tm),
        in_specs=[row(RW_WIDTH), row(MLA_HEADS * V_HEAD),
                  pl.BlockSpec((None, tm, D_MODEL), lambda b, i: (b, i, 0)),
                  pl.BlockSpec((None, tm, D_MODEL), lambda b, i: (b, i, 1)),
                  row(D_MODEL),
                  pl.BlockSpec((None, 6, D_MODEL), lambda b, i: (b, 0, 0)),
                  _const_spec(wr.shape), _const_spec(wm.shape), _const_spec(wo.shape),
                  _const_spec(ln.shape)],
        out_specs=row(D_MODEL),
        out_shape=jax.ShapeDtypeStruct((bsz, seq, D_MODEL), F32),
        compiler_params=pltpu.CompilerParams(dimension_semantics=("parallel", "parallel"),
                                             vmem_limit_bytes=VMEM_LIMIT),
        name="mix",
    )(y_rw, y_mla, z_g, z_g, x, mod3, wr, wm, wo, ln)


def _ffn_kernel(x_ref, mod_ref, w1_ref, w2_ref, ln_ref, o_ref):
    shift = mod_ref[3:4, :]
    scale = mod_ref[4:5, :]
    gate2 = mod_ref[5:6, :]
    x1 = x_ref[...]
    h = (x1 * (1.0 + scale) + shift).astype(BF16)
    kc = 1024
    acc = jnp.zeros(x1.shape, F32)
    for c in range(D_FF // kc):
        u = jnp.maximum(_dot(h, w1_ref[:, c * kc:(c + 1) * kc]), 0.0)
        acc = acc + _dot((u * u).astype(BF16), w2_ref[c * kc:(c + 1) * kc, :])
    t = DN_ALPHA * x1 + (1.0 + gate2) * acc
    o_ref[...] = _layer_norm(t, ln_ref[0:1, :], ln_ref[1:2, :])


def _ffn(x1, mod3, w1, w2, ln):
    bsz, seq, _ = x1.shape
    tm = 512
    row = pl.BlockSpec((None, tm, D_MODEL), lambda b, i: (b, i, 0))
    return pl.pallas_call(
        _ffn_kernel,
        grid=(bsz, seq // tm),
        in_specs=[row, pl.BlockSpec((None, 6, D_MODEL), lambda b, i: (b, 0, 0)),
                  _const_spec(w1.shape), _const_spec(w2.shape), _const_spec(ln.shape)],
        out_specs=row,
        out_shape=jax.ShapeDtypeStruct((bsz, seq, D_MODEL), F32),
        compiler_params=pltpu.CompilerParams(dimension_semantics=("parallel", "parallel"),
                                             vmem_limit_bytes=VMEM_LIMIT),
        name="ffn",
    )(x1, mod3, w1, w2, ln)


def _pad_cols(w, left, total):
    return jnp.pad(w, ((0, 0), (left, total - left - w.shape[1])))


def _inproj_weight(w_in):
    o = 0
    w_rw = w_in[:, o:o + RW_COLS]; o += RW_COLS
    w_q = w_in[:, o:o + Q_LORA]; o += Q_LORA
    w_kv = w_in[:, o:o + KV_LORA]; o += KV_LORA
    w_kr = w_in[:, o:o + QK_ROPE]; o += QK_ROPE
    w_g = w_in[:, o:o + 2 * D_MODEL]
    half = QK_ROPE // 2
    w_kr_rot = jnp.concatenate([-w_kr[:, half:], w_kr[:, :half]], axis=1)
    w_all = jnp.concatenate([w_rw, w_kv, _pad_cols(w_kr, QK_NOPE, LANE), _pad_cols(w_kr_rot, QK_NOPE, LANE),
                             w_q, w_g], axis=1)
    return w_all.astype(BF16)


def _mla_weights(w_uq, w_ukv):
    half = QK_ROPE // 2
    q = w_uq.reshape(Q_LORA, MLA_HEADS, QK_NOPE + QK_ROPE)
    q_nope, q_1, q_2 = q[..., :QK_NOPE], q[..., QK_NOPE:QK_NOPE + half], q[..., QK_NOPE + half:]
    zpad = jnp.zeros((Q_LORA, MLA_HEADS, HEAD_PAD - QK_NOPE - QK_ROPE), F32)
    wq1 = jnp.concatenate([q_nope, q_1, q_2, zpad], axis=-1).reshape(Q_LORA, MLA_HEADS * HEAD_PAD)
    wq2 = jnp.concatenate([jnp.zeros_like(q_nope), -q_2, q_1, zpad], axis=-1).reshape(Q_LORA, MLA_HEADS * HEAD_PAD)
    kv = w_ukv.reshape(KV_LORA, MLA_HEADS, QK_NOPE + V_HEAD)
    zhalf = jnp.zeros((KV_LORA, MLA_HEADS, HEAD_PAD - QK_NOPE), F32)
    wk = jnp.concatenate([kv[..., :QK_NOPE], zhalf], axis=-1).reshape(KV_LORA, MLA_HEADS * HEAD_PAD)
    wv = jnp.concatenate([kv[..., QK_NOPE:], zhalf], axis=-1).reshape(KV_LORA, MLA_HEADS * HEAD_PAD)
    return wq1.astype(BF16), wq2.astype(BF16), jnp.concatenate([wk, wv], axis=1).astype(BF16)


def _scan_constants(rev):
    rows = np.arange(SCAN_ROWS)
    same = (rows[:, None] // CHUNK) == (rows[None, :] // CHUNK)
    tri = same & ((rows[None, :] >= rows[:, None]) if rev else (rows[None, :] <= rows[:, None]))
    ch = np.arange(RW_WIDTH)
    bd = (ch[:, None] // RW_HEAD) == (ch[None, :] // RW_HEAD)
    return jnp.asarray(tri, BF16), jnp.asarray(bd, BF16)


def _lora_pair(w, d):
    zero = jnp.zeros_like(w[0])
    return jnp.concatenate([zero, w[1]] if d else [w[0], zero], axis=0).astype(BF16)


def kernel(x, c, positions, w_ada, b_ada, w_in, rw_conv, rw_w0, rw_w2, rw_a0, rw_a2, rw_k_k, rw_k_a, rw_r_k, rw_g2, rw_lnx_g, rw_lnx_b, mla_q_norm_g, mla_kv_norm_g, mla_w_uq, mla_w_ukv, w_br_rwkv, w_br_mla, w_out, ln1_g, ln1_b, w_ff1, w_ff2, ln2_g, ln2_b):
    bsz, seq, _ = x.shape
    cos_t, sin_t = _rope_tables(positions)
    for l in range(DEPTH):
        mod3 = _ada(c, w_ada[l], b_ada[l]).reshape(bsz, 6, D_MODEL)
        z_rw, z_kv, z_q, z_g = _inproj(x, mod3, _inproj_weight(w_in[l]))

        zero = jnp.zeros((RW_WIDTH,), F32)
        y_rw = None
        for d in (0, 1):
            vecs = jnp.stack([rw_w0[l, d], rw_a0[l, d], rw_k_k[l], rw_k_a[l], rw_r_k[l],
                              rw_lnx_g[l], rw_lnx_b[l], zero])
            tri, bd = _scan_constants(bool(d))
            y_rw = _scan(z_rw, rw_conv[l], vecs, _lora_pair(rw_w2[l], d), _lora_pair(rw_a2[l], d), bd, tri,
                         rev=bool(d), y_fwd=y_rw, g2=rw_g2[l].astype(BF16))

        wq1, wq2, wkv = _mla_weights(mla_w_uq[l], mla_w_ukv[l])
        y_mla = _attn(z_kv, z_q, cos_t, sin_t, mla_q_norm_g[l].reshape(1, Q_LORA),
                      mla_kv_norm_g[l].reshape(1, KV_LORA), wq1, wq2, wkv)

        x = _mix(y_rw, y_mla, z_g, x, mod3, w_br_rwkv[l].astype(BF16), w_br_mla[l].astype(BF16),
                 w_out[l].astype(BF16), jnp.stack([ln1_g[l], ln1_b[l]]))
        x = _ffn(x, mod3, w_ff1[l].astype(BF16), w_ff2[l].astype(BF16), jnp.stack([ln2_g[l], ln2_b[l]]))
    return x
```

```python
import functools

import numpy as np
import jax
import jax.numpy as jnp
from jax import lax
from jax.experimental import pallas as pl
from jax.experimental.pallas import tpu as pltpu

F32 = jnp.float32
BF16 = jnp.bfloat16

D_MODEL = 1024
RW_HEADS = 8
RW_HEAD = 64
RW_WIDTH = RW_HEADS * RW_HEAD
DECAY_LORA = 64
AAA_LORA = 64
GATE_LORA = 128
RW_COLS = 3 * RW_WIDTH + 2 * DECAY_LORA + 2 * AAA_LORA + GATE_LORA
MLA_HEADS = 8
QK_NOPE = 64
QK_ROPE = 32
V_HEAD = 64
Q_LORA = 384
KV_LORA = 256
ROPE_THETA = 10000.0
D_FF = 4 * D_MODEL
LN_EPS = 1e-5
RMS_EPS = 1e-6
GN_EPS = 64e-5
L2_EPS = 1e-12
DEPTH = 1
DN_ALPHA = (2.0 * DEPTH) ** 0.25
LOG2_E = 1.4426950408889634
DECAY_SCALE = 0.6065306597126334

LANE = 128
CHUNK = 64
SCAN_ROWS = 256
SCAN_PAR_CHUNKS = 4
X_HALO_ROWS = 8
HEAD_PAD = 128
ZKV_COLS = KV_LORA + 2 * LANE
VMEM_LIMIT = 56 * 1024 * 1024


def _dot(a, b):
    return jnp.dot(a, b, preferred_element_type=F32)


def _dot_nt(a, b):
    return lax.dot_general(a, b, (((1,), (1,)), ((), ())), preferred_element_type=F32)


def _dot_tn(a, b):
    return lax.dot_general(a, b, (((0,), (0,)), ((), ())), preferred_element_type=F32)


def _const_spec(shape):
    zeros = (0,) * len(shape)
    return pl.BlockSpec(shape, lambda *_: zeros)


def _layer_norm(t, g, b):
    mu = jnp.mean(t, axis=-1, keepdims=True)
    d = t - mu
    var = jnp.mean(d * d, axis=-1, keepdims=True)
    return d * lax.rsqrt(var + LN_EPS) * g + b


def _ada_kernel(c_ref, w_ref, b_ref, o_ref):
    c = c_ref[...]
    act = c * jax.nn.sigmoid(c)
    o_ref[...] = _dot(act.astype(BF16), w_ref[...].astype(BF16)) + b_ref[...]


def _ada(c, w, b):
    bsz = c.shape[0]
    n = w.shape[1]
    tn = 1536
    return pl.pallas_call(
        _ada_kernel,
        grid=(n // tn,),
        in_specs=[_const_spec((bsz, D_MODEL)),
                  pl.BlockSpec((D_MODEL, tn), lambda j: (0, j)),
                  pl.BlockSpec((1, tn), lambda j: (0, j))],
        out_specs=pl.BlockSpec((bsz, tn), lambda j: (0, j)),
        out_shape=jax.ShapeDtypeStruct((bsz, n), F32),
        name="ada",
    )(c, w, b.reshape(1, n))


def _rope_kernel(pos_ref, inv_ref, cos_ref, sin_ref):
    ang = pos_ref[...].astype(F32) * inv_ref[...]
    cos_ref[...] = jnp.cos(ang)
    sin_ref[...] = jnp.sin(ang)


def _rope_tables(positions):
    bsz, seq = positions.shape
    half = QK_ROPE // 2
    inv = ROPE_THETA ** (-np.arange(half, dtype=np.float32) / half)
    inv128 = np.zeros((1, LANE), np.float32)
    inv128[0, QK_NOPE:QK_NOPE + half] = inv
    inv128[0, QK_NOPE + half:QK_NOPE + QK_ROPE] = inv
    spec = pl.BlockSpec((None, seq, LANE), lambda b: (b, 0, 0))
    return pl.pallas_call(
        _rope_kernel,
        grid=(bsz,),
        in_specs=[pl.BlockSpec((None, seq, 1), lambda b: (b, 0, 0)), _const_spec((1, LANE))],
        out_specs=[spec, spec],
        out_shape=[jax.ShapeDtypeStruct((bsz, seq, LANE), F32)] * 2,
        name="rope",
    )(positions.reshape(bsz, seq, 1), jnp.asarray(inv128))


def _inproj_kernel(x_ref, xp_ref, xn_ref, mod_ref, w_ref, cw_ref, zrw_ref, zkv_ref, zq_ref, zg_ref):
    i = pl.program_id(1)
    nb = pl.num_programs(1)
    rows = x_ref.shape[0]
    shift = mod_ref[0:1, :]
    scale = mod_ref[1:2, :]
    h = (x_ref[...] * (1.0 + scale) + shift).astype(BF16)
    col = RW_COLS
    for o_ref in (zkv_ref, zq_ref, zg_ref):
        n = o_ref.shape[-1]
        o_ref[...] = _dot(h, w_ref[:, col:col + n]).astype(o_ref.dtype)
        col += n

    z = _dot(h, w_ref[:, 0:RW_COLS])
    x_halo = jnp.concatenate([xp_ref[...], xn_ref[...]], axis=0)
    z_halo = _dot((x_halo * (1.0 + scale) + shift).astype(BF16), w_ref[:, 0:RW_COLS])
    prev_row = z_halo[X_HALO_ROWS - 1:X_HALO_ROWS, :] * jnp.where(i > 0, 1.0, 0.0)
    next_row = z_halo[X_HALO_ROWS:X_HALO_ROWS + 1, :] * jnp.where(i < nb - 1, 1.0, 0.0)
    row_id = lax.broadcasted_iota(jnp.int32, (rows, 1), 0)
    z_dn = jnp.where(row_id == 0, prev_row, pltpu.roll(z, 1, 0))
    z_up = jnp.where(row_id == rows - 1, next_row, pltpu.roll(z, rows - 1, 0))
    zrw_ref[...] = (cw_ref[0:1, :] * z_dn + cw_ref[1:2, :] * z + cw_ref[2:3, :] * z_up).astype(zrw_ref.dtype)


def _inproj(x, mod3, w_all, conv_w):
    bsz, seq, _ = x.shape
    tm = 512
    hpb = tm // X_HALO_ROWS
    n_halo = seq // X_HALO_ROWS
    widths = (RW_COLS, ZKV_COLS, Q_LORA, 2 * D_MODEL)
    return pl.pallas_call(
        _inproj_kernel,
        grid=(bsz, seq // tm),
        in_specs=[pl.BlockSpec((None, tm, D_MODEL), lambda b, i: (b, i, 0)),
                  pl.BlockSpec((None, X_HALO_ROWS, D_MODEL), lambda b, i: (b, jnp.maximum(i * hpb - 1, 0), 0)),
                  pl.BlockSpec((None, X_HALO_ROWS, D_MODEL),
                               lambda b, i: (b, jnp.minimum((i + 1) * hpb, n_halo - 1), 0)),
                  pl.BlockSpec((None, 6, D_MODEL), lambda b, i: (b, 0, 0)),
                  _const_spec(w_all.shape), _const_spec(conv_w.shape)],
        out_specs=[pl.BlockSpec((None, tm, n), lambda b, i: (b, i, 0)) for n in widths],
        out_shape=[jax.ShapeDtypeStruct((bsz, seq, n), BF16) for n in widths],
        compiler_params=pltpu.CompilerParams(dimension_semantics=("parallel", "parallel"),
                                             vmem_limit_bytes=VMEM_LIMIT),
        name="inproj",
    )(x, x, x, mod3, w_all, conv_w)


def _sigmoid(x):
    return 0.5 + 0.5 * jnp.tanh(0.5 * x)


def _scan_kernel(*refs, rev):
    if rev:
        (zm_ref, vec_ref, w2_ref, a2_ref, bd_ref, tri_ref,
         yf_ref, g2_ref, o_ref,
         at_s, rt_s, bh_s, kh_s, be_s, ke_s, v_s, vsw_s, eg_s, h_s, ya_s, yb_s) = refs
    else:
        (zm_ref, vec_ref, w2_ref, a2_ref, bd_ref, tri_ref,
         o_ref,
         at_s, rt_s, bh_s, kh_s, be_s, ke_s, v_s, vsw_s, eg_s, h_s, ya_s, yb_s) = refs
    rows = zm_ref.shape[0]
    n_chunks = rows // CHUNK
    i = pl.program_id(1)

    @pl.when(i == 0)
    def _():
        h_s[...] = jnp.zeros_like(h_s)

    zc = zm_ref[...].astype(F32)
    r = zc[:, 0:RW_WIDTH]
    k = zc[:, RW_WIDTH:2 * RW_WIDTH]
    v = zc[:, 2 * RW_WIDTH:3 * RW_WIDTH]
    zw = zc[:, 3 * RW_WIDTH:3 * RW_WIDTH + LANE]
    za = zc[:, 3 * RW_WIDTH + LANE:3 * RW_WIDTH + 2 * LANE]
    w0 = vec_ref[0:1, :]
    a0 = vec_ref[1:2, :]
    k_k = vec_ref[2:3, :]
    k_a = vec_ref[3:4, :]

    w_lin = w0 + _dot(jnp.tanh(zw).astype(BF16), w2_ref[...])
    lw = -DECAY_SCALE * _sigmoid(w_lin)
    rate = _sigmoid(a0 + _dot(za.astype(BF16), a2_ref[...]))
    kkv = k * k_k
    ssq = _dot((kkv * kkv).astype(BF16), bd_ref[...])
    kk = kkv * lax.rsqrt(jnp.maximum(ssq, L2_EPS * L2_EPS))
    kd = k * (1.0 + (rate - 1.0) * k_a)
    av = -kk
    bv = kk * rate

    lw_hi = lw.astype(BF16)
    lw_lo = (lw - lw_hi.astype(F32)).astype(BF16)
    g_in = _dot(tri_ref[...], lw_hi) + _dot(tri_ref[...], lw_lo)
    g_ex = g_in - lw
    last = 0 if rev else CHUNK - 1
    g_tot_rows = [g_in[c * CHUNK + last:c * CHUNK + last + 1, :] for c in range(n_chunks)]
    g_tot = jnp.concatenate([jnp.broadcast_to(t, (CHUNK, RW_WIDTH)) for t in g_tot_rows], axis=0)
    eg_rows = [jnp.exp(t) for t in g_tot_rows]
    e_neg = jnp.exp(-g_in)
    e_end = e_neg * jnp.concatenate([jnp.broadcast_to(t, (CHUNK, RW_WIDTH)) for t in eg_rows], axis=0)
    at_s[...] = av * jnp.exp(g_ex)
    rt_s[...] = r * jnp.exp(g_in)
    bh_s[...] = bv * e_neg
    kh_s[...] = kd * e_neg
    be_s[...] = bv * e_end
    ke_s[...] = kd * e_end
    v_s[...] = v
    vsw_s[...] = pltpu.roll(v, RW_HEAD, 1)
    for c in range(n_chunks):
        eg_s[c] = jnp.broadcast_to(eg_rows[c], (8, RW_WIDTH))

    ri = lax.broadcasted_iota(jnp.int32, (CHUNK, CHUNK), 0)
    ci = lax.broadcasted_iota(jnp.int32, (CHUNK, CHUNK), 1)
    m_strict = (ci > ri) if rev else (ci < ri)
    m_incl = (ci >= ri) if rev else (ci <= ri)
    eye = jnp.where(ri == ci, 1.0, 0.0).astype(F32)
    lower = lax.broadcasted_iota(jnp.int32, (CHUNK, LANE), 1) < RW_HEAD
    eye_up = jnp.where(lax.broadcasted_iota(jnp.int32, (CHUNK, LANE), 1)
                       == lax.broadcasted_iota(jnp.int32, (CHUNK, LANE), 0) + RW_HEAD, 1.0, 0.0).astype(F32)
    zeros_half = jnp.zeros((CHUNK, RW_HEAD), F32)

    def pad(t):
        return jnp.concatenate([t, zeros_half], axis=1)

    par = SCAN_PAR_CHUNKS
    n_groups = n_chunks // par
    heads = range(RW_HEADS)
    items = [(cc, h) for cc in range(par) for h in heads]
    n_it = range(len(items))

    def group_body(j, carry):
        g = n_groups - 1 - j if rev else j
        base = g * par

        def ld(ref, it):
            cc, h = items[it]
            r0 = pl.multiple_of((base + cc) * CHUNK, CHUNK)
            return ref[pl.ds(r0, CHUNK), h * RW_HEAD:(h + 1) * RW_HEAD]

        rt = [ld(rt_s, it) for it in n_it]
        atb = [ld(at_s, it).astype(BF16) for it in n_it]
        lhs = [jnp.concatenate([atb[it], rt[it].astype(BF16)], axis=0) for it in n_it]
        a_b = [_dot_nt(lhs[it], ld(bh_s, it).astype(BF16)) for it in n_it]
        a_k = [_dot_nt(lhs[it], ld(kh_s, it).astype(BF16)) for it in n_it]
        a_ab = [jnp.where(m_strict, a_b[it][0:CHUNK], 0.0) for it in n_it]
        a_ak = [jnp.where(m_strict, a_k[it][0:CHUNK], 0.0).astype(BF16) for it in n_it]
        a_rb = [jnp.where(m_incl, a_b[it][CHUNK:], 0.0).astype(BF16) for it in n_it]
        a_rk = [jnp.where(m_incl, a_k[it][CHUNK:], 0.0).astype(BF16) for it in n_it]

        def ld_v_upper(it):
            cc, h = items[it]
            r0 = pl.multiple_of((base + cc) * CHUNK, CHUNK)
            src = v_s if h % 2 else vsw_s
            blk = src[pl.ds(r0, CHUNK), (h // 2) * LANE:(h // 2 + 1) * LANE]
            return jnp.where(lower, 0.0, blk).astype(BF16)

        def ld_decay(it):
            cc, h = items[it]
            return eg_s[base + cc][0:1, h * RW_HEAD:(h + 1) * RW_HEAD]

        v_up = [ld_v_upper(it) for it in n_it]
        akv = [_dot(a_ak[it], v_up[it]) for it in n_it]

        z = [pad(a_ab[it]) + eye_up for it in n_it]
        for _ in range(6):
            z = [_dot(z[it][:, 0:CHUNK].astype(BF16), z[it].astype(BF16)) + jnp.where(lower, 0.0, z[it])
                 for it in n_it]
        tb = [z[it][:, CHUNK:].astype(BF16) for it in n_it]

        w_m = [(pad(ld(at_s, it)) + akv[it]).astype(BF16) for it in n_it]
        pq = [_dot(tb[it], w_m[it]).astype(BF16) for it in n_it]
        ryq = [_dot(a_rb[it], pq[it]) for it in n_it]
        ry = [(rt[it] + ryq[it][:, 0:RW_HEAD]).astype(BF16) for it in n_it]
        y0 = [jnp.where(lower, 0.0, ryq[it]) + _dot(a_rk[it], v_up[it]) for it in n_it]
        be = [ld(be_s, it).astype(BF16) for it in n_it]
        ke = [ld(ke_s, it).astype(BF16) for it in n_it]
        mq = [_dot_tn(be[it], pq[it]) for it in n_it]
        m_m = [(mq[it][:, 0:RW_HEAD] + eye * ld_decay(it)).astype(BF16) for it in n_it]
        h0 = [jnp.where(lower, 0.0, mq[it]) + _dot_tn(ke[it], v_up[it]) for it in n_it]
        ry_m = [jnp.concatenate([ry[it], m_m[it]], axis=0) for it in n_it]

        hst = [h_s[h] for h in heads]
        for cc in (range(par - 1, -1, -1) if rev else range(par)):
            r0 = pl.multiple_of((base + cc) * CHUNK, CHUNK)
            for h in heads:
                it = cc * RW_HEADS + h
                yh = _dot(ry_m[it], hst[h].astype(BF16))
                y_dst = ya_s if h % 2 else yb_s
                y_dst[pl.ds(r0, CHUNK), (h // 2) * LANE:(h // 2 + 1) * LANE] = yh[0:CHUNK] + y0[it]
                hst[h] = yh[CHUNK:] + h0[it]
        for h in heads:
            h_s[h] = hst[h]
        return carry

    if n_groups == 1:
        group_body(0, 0)
    else:
        lax.fori_loop(0, n_groups, group_body, 0)

    y_dir = ya_s[...] + pltpu.roll(yb_s[...], RW_WIDTH - RW_HEAD, 1)
    if not rev:
        o_ref[...] = y_dir
    else:
        r_k = vec_ref[4:5, :]
        lnx_g = vec_ref[5:6, :]
        lnx_b = vec_ref[6:7, :]
        y = yf_ref[...] + y_dir
        inv_n = 1.0 / RW_HEAD
        mu = _dot(y.astype(BF16), bd_ref[...]) * inv_n
        d = y - mu
        var = _dot((d * d).astype(BF16), bd_ref[...]) * inv_n
        yn = d * lax.rsqrt(var + GN_EPS) * lnx_g + lnx_b
        rk = r * k * r_k
        rk_hi = rk.astype(BF16)
        rk_lo = (rk - rk_hi.astype(F32)).astype(BF16)
        bonus = (_dot(rk_hi, bd_ref[...]) + _dot(rk_lo, bd_ref[...])) * v
        zg = zc[:, 3 * RW_WIDTH + 2 * LANE:RW_COLS]
        gate = _dot(_sigmoid(zg).astype(BF16), g2_ref[...])
        o_ref[...] = ((yn + bonus) * gate).astype(o_ref.dtype)


def _scan(z_rw, vecs, w2p, a2p, bd, tri, rev, y_fwd=None, g2=None):
    bsz, seq, _ = z_rw.shape
    rows = SCAN_ROWS
    nb = seq // rows
    n_chunks = rows // CHUNK

    def blk(i):
        return nb - 1 - i if rev else i

    in_specs = [
        pl.BlockSpec((None, rows, RW_COLS), lambda b, i: (b, blk(i), 0)),
        _const_spec(vecs.shape), _const_spec(w2p.shape),
        _const_spec(a2p.shape), _const_spec(bd.shape), _const_spec(tri.shape),
    ]
    args = [z_rw, vecs, w2p, a2p, bd, tri]
    scratch = [pltpu.VMEM((rows, RW_WIDTH), F32) for _ in range(8)]
    scratch += [pltpu.VMEM((n_chunks, 8, RW_WIDTH), F32),
                pltpu.VMEM((RW_HEADS, RW_HEAD, LANE), F32),
                pltpu.VMEM((rows, RW_WIDTH), F32), pltpu.VMEM((rows, RW_WIDTH), F32)]
    if rev:
        in_specs += [pl.BlockSpec((None, rows, RW_WIDTH), lambda b, i: (b, blk(i), 0)),
                     _const_spec(g2.shape)]
        args += [y_fwd, g2]
        out_dtype = BF16
    else:
        out_dtype = F32
    return pl.pallas_call(
        functools.partial(_scan_kernel, rev=rev),
        grid=(bsz, nb),
        in_specs=in_specs,
        out_specs=pl.BlockSpec((None, rows, RW_WIDTH), lambda b, i: (b, blk(i), 0)),
        out_shape=jax.ShapeDtypeStruct((bsz, seq, RW_WIDTH), out_dtype),
        scratch_shapes=scratch,
        compiler_params=pltpu.CompilerParams(dimension_semantics=("parallel", "arbitrary"),
                                             vmem_limit_bytes=VMEM_LIMIT),
        name="scan_bwd" if rev else "scan_fwd",
    )(*args)


def _rms(xf, g):
    return xf * lax.rsqrt(jnp.mean(xf * xf, axis=-1, keepdims=True) + RMS_EPS) * g


def _attn_kernel(zkv_ref, zq_ref, cos_ref, sin_ref, gq_ref, gkv_ref, wq1_ref, wq2_ref, wk_ref, wvt_ref,
                 o_ref, k_s, vt_s):
    seq = zkv_ref.shape[0]
    tq = zq_ref.shape[0]
    kt = 256
    i = pl.program_id(1)

    @pl.when(i == 0)
    def _():
        def kv_body(t, carry):
            r0 = pl.multiple_of(t * kt, kt)
            zk = zkv_ref[pl.ds(r0, kt), :].astype(F32)
            kvn = _rms(zk[:, 0:KV_LORA], gkv_ref[...]).astype(BF16)
            kn = _dot(kvn, wk_ref[...])
            cs = cos_ref[pl.ds(r0, kt), :]
            sn = sin_ref[pl.ds(r0, kt), :]
            kpe = zk[:, KV_LORA:KV_LORA + LANE] * cs + zk[:, KV_LORA + LANE:KV_LORA + 2 * LANE] * sn
            for h in range(MLA_HEADS):
                sl = slice(h * HEAD_PAD, (h + 1) * HEAD_PAD)
                k_s[pl.ds(r0, kt), sl] = (kn[:, sl] + kpe).astype(BF16)
            vt_s[:, pl.ds(r0, kt)] = _dot_nt(wvt_ref[...], kvn).astype(BF16)
            return carry
        lax.fori_loop(0, seq // kt, kv_body, 0)

    q0 = pl.multiple_of(i * tq, tq)
    qn = _rms(zq_ref[...].astype(F32), gq_ref[...]).astype(BF16)
    q1 = _dot(qn, wq1_ref[...])
    q2 = _dot(qn, wq2_ref[...])
    cs = cos_ref[pl.ds(q0, tq), :]
    sn = sin_ref[pl.ds(q0, tq), :]
    scale = (QK_NOPE + QK_ROPE) ** -0.5 * LOG2_E
    def scores(h):
        sl = slice(h * HEAD_PAD, (h + 1) * HEAD_PAD)
        qh = ((q1[:, sl] * cs + q2[:, sl] * sn) * scale).astype(BF16)
        return _dot_nt(k_s[:, sl], qh)

    outs = []
    st_next = scores(0)
    for h in range(MLA_HEADS):
        st = st_next
        if h + 1 < MLA_HEADS:
            st_next = scores(h + 1)
        m = jnp.max(st, axis=0, keepdims=True)
        p = jnp.exp2(st - m)
        l = jnp.sum(p, axis=0, keepdims=True)
        ot = _dot(vt_s[h * V_HEAD:(h + 1) * V_HEAD, :], p.astype(BF16))
        outs.append(ot / l)
    o_ref[...] = jnp.concatenate(outs, axis=0).T.astype(o_ref.dtype)


def _attn(z_kv, z_q, cos_t, sin_t, gq, gkv, wq1, wq2, wk, wvt):
    bsz, seq, _ = z_kv.shape
    tq = 256
    kw = MLA_HEADS * HEAD_PAD
    return pl.pallas_call(
        _attn_kernel,
        grid=(bsz, seq // tq),
        in_specs=[pl.BlockSpec((None, seq, ZKV_COLS), lambda b, i: (b, 0, 0)),
                  pl.BlockSpec((None, tq, Q_LORA), lambda b, i: (b, i, 0)),
                  pl.BlockSpec((None, seq, LANE), lambda b, i: (b, 0, 0)),
                  pl.BlockSpec((None, seq, LANE), lambda b, i: (b, 0, 0)),
                  _const_spec(gq.shape), _const_spec(gkv.shape), _const_spec(wq1.shape),
                  _const_spec(wq2.shape), _const_spec(wk.shape), _const_spec(wvt.shape)],
        out_specs=pl.BlockSpec((None, tq, MLA_HEADS * V_HEAD), lambda b, i: (b, i, 0)),
        out_shape=jax.ShapeDtypeStruct((bsz, seq, MLA_HEADS * V_HEAD), BF16),
        scratch_shapes=[pltpu.VMEM((seq, kw), BF16), pltpu.VMEM((MLA_HEADS * V_HEAD, seq), BF16)],
        compiler_params=pltpu.CompilerParams(dimension_semantics=("parallel", "arbitrary"),
                                             vmem_limit_bytes=VMEM_LIMIT),
        name="attn",
    )(z_kv, z_q, cos_t, sin_t, gq, gkv, wq1, wq2, wk, wvt)


def _mix_kernel(yrw_ref, ymla_ref, ga_ref, gb_ref, x_ref, mod_ref, wr_ref, wm_ref, wo_ref, ln_ref, o_ref):
    gate1 = mod_ref[2:3, :]
    br_rw = _dot(yrw_ref[...], wr_ref[...])
    br_mla = _dot(ymla_ref[...], wm_ref[...])
    mixed = (_sigmoid(ga_ref[...].astype(F32)) * br_rw
             + _sigmoid(gb_ref[...].astype(F32)) * br_mla)
    out = _dot(mixed.astype(BF16), wo_ref[...])
    t = DN_ALPHA * x_ref[...] + (1.0 + gate1) * out
    o_ref[...] = _layer_norm(t, ln_ref[0:1, :], ln_ref[1:2, :])


def _mix(y_rw, y_mla, z_g, x, mod3, wr, wm, wo, ln):
    bsz, seq, _ = x.shape
    tm = 512
    row = lambda n: pl.BlockSpec((None, tm, n), lambda b, i: (b, i, 0))
    return pl.pallas_call(
        _mix_kernel,
        grid=(bsz, seq // tm),
        in_specs=[row(RW_WIDTH), row(MLA_HEADS * V_HEAD),
                  pl.BlockSpec((None, tm, D_MODEL), lambda b, i: (b, i, 0)),
                  pl.BlockSpec((None, tm, D_MODEL), lambda b, i: (b, i, 1)),
                  row(D_MODEL),
                  pl.BlockSpec((None, 6, D_MODEL), lambda b, i: (b, 0, 0)),
                  _const_spec(wr.shape), _const_spec(wm.shape), _const_spec(wo.shape),
                  _const_spec(ln.shape)],
        out_specs=row(D_MODEL),
        out_shape=jax.ShapeDtypeStruct((bsz, seq, D_MODEL), F32),
        compiler_params=pltpu.CompilerParams(dimension_semantics=("parallel", "parallel"),
                                             vmem_limit_bytes=VMEM_LIMIT),
        name="mix",
    )(y_rw, y_mla, z_g, z_g, x, mod3, wr, wm, wo, ln)


def _ffn_kernel(x_ref, mod_ref, w1_ref, w2_ref, ln_ref, o_ref):
    shift = mod_ref[3:4, :]
    scale = mod_ref[4:5, :]
    gate2 = mod_ref[5:6, :]
    x1 = x_ref[...]
    h = (x1 * (1.0 + scale) + shift).astype(BF16)
    kc = 1024
    acc = jnp.zeros(x1.shape, F32)
    for c in range(D_FF // kc):
        u = jnp.maximum(_dot(h, w1_ref[:, c * kc:(c + 1) * kc]), 0.0)
        acc = acc + _dot((u * u).astype(BF16), w2_ref[c * kc:(c + 1) * kc, :])
    t = DN_ALPHA * x1 + (1.0 + gate2) * acc
    o_ref[...] = _layer_norm(t, ln_ref[0:1, :], ln_ref[1:2, :])


def _ffn(x1, mod3, w1, w2, ln):
    bsz, seq, _ = x1.shape
    tm = 512
    row = pl.BlockSpec((None, tm, D_MODEL), lambda b, i: (b, i, 0))
    return pl.pallas_call(
        _ffn_kernel,
        grid=(bsz, seq // tm),
        in_specs=[row, pl.BlockSpec((None, 6, D_MODEL), lambda b, i: (b, 0, 0)),
                  _const_spec(w1.shape), _const_spec(w2.shape), _const_spec(ln.shape)],
        out_specs=row,
        out_shape=jax.ShapeDtypeStruct((bsz, seq, D_MODEL), F32),
        compiler_params=pltpu.CompilerParams(dimension_semantics=("parallel", "parallel"),
                                             vmem_limit_bytes=VMEM_LIMIT),
        name="ffn",
    )(x1, mod3, w1, w2, ln)


def _pad_cols(w, left, total):
    return jnp.pad(w, ((0, 0), (left, total - left - w.shape[1])))


def _inproj_weight(w_in):
    o = 0
    w_rw = w_in[:, o:o + RW_COLS]; o += RW_COLS
    w_q = w_in[:, o:o + Q_LORA]; o += Q_LORA
    w_kv = w_in[:, o:o + KV_LORA]; o += KV_LORA
    w_kr = w_in[:, o:o + QK_ROPE]; o += QK_ROPE
    w_g = w_in[:, o:o + 2 * D_MODEL]
    half = QK_ROPE // 2
    w_kr_rot = jnp.concatenate([-w_kr[:, half:], w_kr[:, :half]], axis=1)
    w_all = jnp.concatenate([w_rw, w_kv, _pad_cols(w_kr, QK_NOPE, LANE), _pad_cols(w_kr_rot, QK_NOPE, LANE),
                             w_q, w_g], axis=1)
    return w_all.astype(BF16)


def _mla_weights(w_uq, w_ukv):
    half = QK_ROPE // 2
    q = w_uq.reshape(Q_LORA, MLA_HEADS, QK_NOPE + QK_ROPE)
    q_nope, q_1, q_2 = q[..., :QK_NOPE], q[..., QK_NOPE:QK_NOPE + half], q[..., QK_NOPE + half:]
    zpad = jnp.zeros((Q_LORA, MLA_HEADS, HEAD_PAD - QK_NOPE - QK_ROPE), F32)
    wq1 = jnp.concatenate([q_nope, q_1, q_2, zpad], axis=-1).reshape(Q_LORA, MLA_HEADS * HEAD_PAD)
    wq2 = jnp.concatenate([jnp.zeros_like(q_nope), -q_2, q_1, zpad], axis=-1).reshape(Q_LORA, MLA_HEADS * HEAD_PAD)
    kv = w_ukv.reshape(KV_LORA, MLA_HEADS, QK_NOPE + V_HEAD)
    zhalf = jnp.zeros((KV_LORA, MLA_HEADS, HEAD_PAD - QK_NOPE), F32)
    wk = jnp.concatenate([kv[..., :QK_NOPE], zhalf], axis=-1).reshape(KV_LORA, MLA_HEADS * HEAD_PAD)
    wvt = kv[..., QK_NOPE:].reshape(KV_LORA, MLA_HEADS * V_HEAD).T
    return wq1.astype(BF16), wq2.astype(BF16), wk.astype(BF16), wvt.astype(BF16)


def _scan_constants(rev):
    rows = np.arange(SCAN_ROWS)
    same = (rows[:, None] // CHUNK) == (rows[None, :] // CHUNK)
    tri = same & ((rows[None, :] >= rows[:, None]) if rev else (rows[None, :] <= rows[:, None]))
    ch = np.arange(RW_WIDTH)
    bd = (ch[:, None] // RW_HEAD) == (ch[None, :] // RW_HEAD)
    return jnp.asarray(tri, BF16), jnp.asarray(bd, BF16)


def _lora_pair(w, d):
    zero = jnp.zeros_like(w[0])
    return jnp.concatenate([zero, w[1]] if d else [w[0], zero], axis=0).astype(BF16)


def kernel(x, c, positions, w_ada, b_ada, w_in, rw_conv, rw_w0, rw_w2, rw_a0, rw_a2, rw_k_k, rw_k_a, rw_r_k, rw_g2, rw_lnx_g, rw_lnx_b, mla_q_norm_g, mla_kv_norm_g, mla_w_uq, mla_w_ukv, w_br_rwkv, w_br_mla, w_out, ln1_g, ln1_b, w_ff1, w_ff2, ln2_g, ln2_b):
    bsz, seq, _ = x.shape
    cos_t, sin_t = _rope_tables(positions)
    for l in range(DEPTH):
        mod3 = _ada(c, w_ada[l], b_ada[l]).reshape(bsz, 6, D_MODEL)
        z_rw, z_kv, z_q, z_g = _inproj(x, mod3, _inproj_weight(w_in[l]), rw_conv[l])

        zero = jnp.zeros((RW_WIDTH,), F32)
        y_rw = None
        for d in (0, 1):
            vecs = jnp.stack([rw_w0[l, d], rw_a0[l, d], rw_k_k[l], rw_k_a[l], rw_r_k[l],
                              rw_lnx_g[l], rw_lnx_b[l], zero])
            tri, bd = _scan_constants(bool(d))
            y_rw = _scan(z_rw, vecs, _lora_pair(rw_w2[l], d), _lora_pair(rw_a2[l], d), bd, tri,
                         rev=bool(d), y_fwd=y_rw, g2=rw_g2[l].astype(BF16))

        wq1, wq2, wk, wvt = _mla_weights(mla_w_uq[l], mla_w_ukv[l])
        y_mla = _attn(z_kv, z_q, cos_t, sin_t, mla_q_norm_g[l].reshape(1, Q_LORA),
                      mla_kv_norm_g[l].reshape(1, KV_LORA), wq1, wq2, wk, wvt)

        x = _mix(y_rw, y_mla, z_g, x, mod3, w_br_rwkv[l].astype(BF16), w_br_mla[l].astype(BF16),
                 w_out[l].astype(BF16), jnp.stack([ln1_g[l], ln1_b[l]]))
        x = _ffn(x, mod3, w_ff1[l].astype(BF16), w_ff2[l].astype(BF16), jnp.stack([ln2_g[l], ln2_b[l]]))
    return x
```

```python
import functools

import numpy as np
import jax
import jax.numpy as jnp
from jax import lax
from jax.experimental import pallas as pl
from jax.experimental.pallas import tpu as pltpu

F32 = jnp.float32
BF16 = jnp.bfloat16

D_MODEL = 1024
RW_HEADS = 8
RW_HEAD = 64
RW_WIDTH = RW_HEADS * RW_HEAD
DECAY_LORA = 64
AAA_LORA = 64
GATE_LORA = 128
RW_COLS = 3 * RW_WIDTH + 2 * DECAY_LORA + 2 * AAA_LORA + GATE_LORA
MLA_HEADS = 8
QK_NOPE = 64
QK_ROPE = 32
V_HEAD = 64
Q_LORA = 384
KV_LORA = 256
ROPE_THETA = 10000.0
D_FF = 4 * D_MODEL
LN_EPS = 1e-5
RMS_EPS = 1e-6
GN_EPS = 64e-5
L2_EPS = 1e-12
DEPTH = 1
DN_ALPHA = (2.0 * DEPTH) ** 0.25
LOG2_E = 1.4426950408889634
DECAY_SCALE = 0.6065306597126334

LANE = 128
CHUNK = 64
SCAN_ROWS = 512
SCAN_WAVE = 4
X_HALO_ROWS = 8
HEAD_PAD = 128
ZKV_COLS = KV_LORA + 2 * LANE
VMEM_LIMIT = 56 * 1024 * 1024


def _dot(a, b):
    return jnp.dot(a, b, preferred_element_type=F32)


def _dot_nt(a, b):
    return lax.dot_general(a, b, (((1,), (1,)), ((), ())), preferred_element_type=F32)


def _dot_tn(a, b):
    return lax.dot_general(a, b, (((0,), (0,)), ((), ())), preferred_element_type=F32)


def _const_spec(shape):
    zeros = (0,) * len(shape)
    return pl.BlockSpec(shape, lambda *_: zeros)


def _layer_norm(t, g, b):
    mu = jnp.mean(t, axis=-1, keepdims=True)
    d = t - mu
    var = jnp.mean(d * d, axis=-1, keepdims=True)
    return d * lax.rsqrt(var + LN_EPS) * g + b


def _ada_kernel(c_ref, w_ref, b_ref, o_ref):
    c = c_ref[...]
    act = c * jax.nn.sigmoid(c)
    o_ref[...] = _dot(act.astype(BF16), w_ref[...].astype(BF16)) + b_ref[...]


def _ada(c, w, b):
    bsz = c.shape[0]
    n = w.shape[1]
    tn = 1536
    return pl.pallas_call(
        _ada_kernel,
        grid=(n // tn,),
        in_specs=[_const_spec((bsz, D_MODEL)),
                  pl.BlockSpec((D_MODEL, tn), lambda j: (0, j)),
                  pl.BlockSpec((1, tn), lambda j: (0, j))],
        out_specs=pl.BlockSpec((bsz, tn), lambda j: (0, j)),
        out_shape=jax.ShapeDtypeStruct((bsz, n), F32),
        name="ada",
    )(c, w, b.reshape(1, n))


def _rope_kernel(pos_ref, inv_ref, cos_ref, sin_ref):
    ang = pos_ref[...].astype(F32) * inv_ref[...]
    cos_ref[...] = jnp.cos(ang)
    sin_ref[...] = jnp.sin(ang)


def _rope_tables(positions):
    bsz, seq = positions.shape
    half = QK_ROPE // 2
    inv = ROPE_THETA ** (-np.arange(half, dtype=np.float32) / half)
    inv128 = np.zeros((1, LANE), np.float32)
    inv128[0, QK_NOPE:QK_NOPE + half] = inv
    inv128[0, QK_NOPE + half:QK_NOPE + QK_ROPE] = inv
    spec = pl.BlockSpec((None, seq, LANE), lambda b: (b, 0, 0))
    return pl.pallas_call(
        _rope_kernel,
        grid=(bsz,),
        in_specs=[pl.BlockSpec((None, seq, 1), lambda b: (b, 0, 0)), _const_spec((1, LANE))],
        out_specs=[spec, spec],
        out_shape=[jax.ShapeDtypeStruct((bsz, seq, LANE), F32)] * 2,
        name="rope",
    )(positions.reshape(bsz, seq, 1), jnp.asarray(inv128))


def _inproj_kernel(x_ref, xp_ref, xn_ref, mod_ref, w_ref, cw_ref, zrw_ref, zkv_ref, zq_ref, zg_ref):
    i = pl.program_id(1)
    nb = pl.num_programs(1)
    rows = x_ref.shape[0]
    shift = mod_ref[0:1, :]
    scale = mod_ref[1:2, :]
    h = (x_ref[...] * (1.0 + scale) + shift).astype(BF16)
    col = RW_COLS
    for o_ref in (zkv_ref, zq_ref, zg_ref):
        n = o_ref.shape[-1]
        o_ref[...] = _dot(h, w_ref[:, col:col + n]).astype(o_ref.dtype)
        col += n

    z = _dot(h, w_ref[:, 0:RW_COLS])
    x_halo = jnp.concatenate([xp_ref[...], xn_ref[...]], axis=0)
    z_halo = _dot((x_halo * (1.0 + scale) + shift).astype(BF16), w_ref[:, 0:RW_COLS])
    prev_row = z_halo[X_HALO_ROWS - 1:X_HALO_ROWS, :] * jnp.where(i > 0, 1.0, 0.0)
    next_row = z_halo[X_HALO_ROWS:X_HALO_ROWS + 1, :] * jnp.where(i < nb - 1, 1.0, 0.0)
    row_id = lax.broadcasted_iota(jnp.int32, (rows, 1), 0)
    z_dn = jnp.where(row_id == 0, prev_row, pltpu.roll(z, 1, 0))
    z_up = jnp.where(row_id == rows - 1, next_row, pltpu.roll(z, rows - 1, 0))
    zrw_ref[...] = (cw_ref[0:1, :] * z_dn + cw_ref[1:2, :] * z + cw_ref[2:3, :] * z_up).astype(zrw_ref.dtype)


def _inproj(x, mod3, w_all, conv_w):
    bsz, seq, _ = x.shape
    tm = 512
    hpb = tm // X_HALO_ROWS
    n_halo = seq // X_HALO_ROWS
    widths = (RW_COLS, ZKV_COLS, Q_LORA, 2 * D_MODEL)
    return pl.pallas_call(
        _inproj_kernel,
        grid=(bsz, seq // tm),
        in_specs=[pl.BlockSpec((None, tm, D_MODEL), lambda b, i: (b, i, 0)),
                  pl.BlockSpec((None, X_HALO_ROWS, D_MODEL), lambda b, i: (b, jnp.maximum(i * hpb - 1, 0), 0)),
                  pl.BlockSpec((None, X_HALO_ROWS, D_MODEL),
                               lambda b, i: (b, jnp.minimum((i + 1) * hpb, n_halo - 1), 0)),
                  pl.BlockSpec((None, 6, D_MODEL), lambda b, i: (b, 0, 0)),
                  _const_spec(w_all.shape), _const_spec(conv_w.shape)],
        out_specs=[pl.BlockSpec((None, tm, n), lambda b, i: (b, i, 0)) for n in widths],
        out_shape=[jax.ShapeDtypeStruct((bsz, seq, n), BF16) for n in widths],
        compiler_params=pltpu.CompilerParams(dimension_semantics=("parallel", "parallel"),
                                             vmem_limit_bytes=VMEM_LIMIT),
        name="inproj",
    )(x, x, x, mod3, w_all, conv_w)


def _sigmoid(x):
    return 0.5 + 0.5 * jnp.tanh(0.5 * x)


def _scan_kernel(*refs, rev):
    if rev:
        (zm_ref, vec_ref, w2_ref, a2_ref, bd_ref, tri_ref,
         yf_ref, g2_ref, o_ref,
         at_s, rt_s, bh_s, kh_s, be_s, ke_s, v_s, vsw_s, eg_s, h_s, ya_s, yb_s) = refs
    else:
        (zm_ref, vec_ref, w2_ref, a2_ref, bd_ref, tri_ref,
         o_ref,
         at_s, rt_s, bh_s, kh_s, be_s, ke_s, v_s, vsw_s, eg_s, h_s, ya_s, yb_s) = refs
    rows = zm_ref.shape[0]
    n_chunks = rows // CHUNK
    i = pl.program_id(1)

    @pl.when(i == 0)
    def _():
        h_s[...] = jnp.zeros_like(h_s)

    def token_prep(wave):
        cs = sorted(wave)
        n_r = len(cs) * CHUNK
        rs = slice(cs[0] * CHUNK, cs[0] * CHUNK + n_r)
        zc = zm_ref[rs, :].astype(F32)
        r = zc[:, 0:RW_WIDTH]
        k = zc[:, RW_WIDTH:2 * RW_WIDTH]
        v = zc[:, 2 * RW_WIDTH:3 * RW_WIDTH]
        zw = zc[:, 3 * RW_WIDTH:3 * RW_WIDTH + LANE]
        za = zc[:, 3 * RW_WIDTH + LANE:3 * RW_WIDTH + 2 * LANE]
        w0 = vec_ref[0:1, :]
        a0 = vec_ref[1:2, :]
        k_k = vec_ref[2:3, :]
        k_a = vec_ref[3:4, :]
        v_s[rs, :] = v
        vsw_s[rs, :] = pltpu.roll(v, RW_HEAD, 1)
        w_lin = w0 + _dot(jnp.tanh(zw).astype(BF16), w2_ref[...])
        rate_lin = a0 + _dot(za.astype(BF16), a2_ref[...])
        kkv = k * k_k
        ssq = _dot((kkv * kkv).astype(BF16), bd_ref[...])
        yield
        lw = -DECAY_SCALE * _sigmoid(w_lin)
        lw_hi = lw.astype(BF16)
        lw_lo = (lw - lw_hi.astype(F32)).astype(BF16)
        tri = tri_ref[0:n_r, 0:n_r]
        g_in = _dot(tri, lw_hi) + _dot(tri, lw_lo)
        yield
        rate = _sigmoid(rate_lin)
        kk = kkv * lax.rsqrt(jnp.maximum(ssq, L2_EPS * L2_EPS))
        kd = k * (1.0 + (rate - 1.0) * k_a)
        av = -kk
        bv = kk * rate
        yield
        g_ex = g_in - lw
        last = 0 if rev else CHUNK - 1
        g_tot_rows = [g_in[j * CHUNK + last:j * CHUNK + last + 1, :] for j in range(len(cs))]
        eg_rows = [jnp.exp(t) for t in g_tot_rows]
        for j, c in enumerate(cs):
            eg_s[c] = jnp.broadcast_to(eg_rows[j], (8, RW_WIDTH))
        e_neg = jnp.exp(-g_in)
        e_end = e_neg * jnp.concatenate([jnp.broadcast_to(t, (CHUNK, RW_WIDTH)) for t in eg_rows], axis=0)
        at_s[rs, :] = av * jnp.exp(g_ex)
        rt_s[rs, :] = r * jnp.exp(g_in)
        yield
        bh_s[rs, :] = bv * e_neg
        kh_s[rs, :] = kd * e_neg
        be_s[rs, :] = bv * e_end
        ke_s[rs, :] = kd * e_end

    ri = lax.broadcasted_iota(jnp.int32, (CHUNK, CHUNK), 0)
    ci = lax.broadcasted_iota(jnp.int32, (CHUNK, CHUNK), 1)
    m_strict = (ci > ri) if rev else (ci < ri)
    m_incl = (ci >= ri) if rev else (ci <= ri)
    eye = jnp.where(ri == ci, 1.0, 0.0).astype(F32)
    lower = lax.broadcasted_iota(jnp.int32, (CHUNK, LANE), 1) < RW_HEAD
    eye_up = jnp.where(lax.broadcasted_iota(jnp.int32, (CHUNK, LANE), 1)
                       == lax.broadcasted_iota(jnp.int32, (CHUNK, LANE), 0) + RW_HEAD, 1.0, 0.0).astype(F32)
    zeros_half = jnp.zeros((CHUNK, RW_HEAD), F32)

    def pad(t):
        return jnp.concatenate([t, zeros_half], axis=1)

    heads = range(RW_HEADS)
    chunk_order = list(range(n_chunks - 1, -1, -1)) if rev else list(range(n_chunks))
    waves = [chunk_order[w:w + SCAN_WAVE] for w in range(0, n_chunks, SCAN_WAVE)]

    def ld(ref, item):
        c, h = item
        return ref[c * CHUNK:(c + 1) * CHUNK, h * RW_HEAD:(h + 1) * RW_HEAD]

    def ld_v_upper(item):
        c, h = item
        src = v_s if h % 2 else vsw_s
        blk = src[c * CHUNK:(c + 1) * CHUNK, (h // 2) * LANE:(h // 2 + 1) * LANE]
        return jnp.where(lower, 0.0, blk).astype(BF16)

    def ld_decay(item):
        c, h = item
        return eg_s[c][0:1, h * RW_HEAD:(h + 1) * RW_HEAD]

    def chunk_local(items, out):
        n_it = range(len(items))
        rt = [ld(rt_s, it) for it in items]
        atb = [ld(at_s, it).astype(BF16) for it in items]
        lhs = [jnp.concatenate([atb[i], rt[i].astype(BF16)], axis=0) for i in n_it]
        a_b = [_dot_nt(lhs[i], ld(bh_s, items[i]).astype(BF16)) for i in n_it]
        a_k = [_dot_nt(lhs[i], ld(kh_s, items[i]).astype(BF16)) for i in n_it]
        yield
        a_ab =[jnp.where(m_strict, a_b[i][0:CHUNK], 0.0) for i in n_it]
        a_ak = [jnp.where(m_strict, a_k[i][0:CHUNK], 0.0).astype(BF16) for i in n_it]
        a_rb = [jnp.where(m_incl, a_b[i][CHUNK:], 0.0).astype(BF16) for i in n_it]
        a_rk = [jnp.where(m_incl, a_k[i][CHUNK:], 0.0).astype(BF16) for i in n_it]
        v_up = [ld_v_upper(it) for it in items]
        akv = [_dot(a_ak[i], v_up[i]) for i in n_it]
        yield

        z = [pad(a_ab[i]) + eye_up for i in n_it]
        for _ in range(6):
            z = [_dot(z[i][:, 0:CHUNK].astype(BF16), z[i].astype(BF16)) + jnp.where(lower, 0.0, z[i])
                 for i in n_it]
            yield
        tb =[z[i][:, CHUNK:].astype(BF16) for i in n_it]

        w_m = [(pad(ld(at_s, items[i])) + akv[i]).astype(BF16) for i in n_it]
        pq = [_dot(tb[i], w_m[i]).astype(BF16) for i in n_it]
        yield
        ryq =[_dot(a_rb[i], pq[i]) for i in n_it]
        ry = [(rt[i] + ryq[i][:, 0:RW_HEAD]).astype(BF16) for i in n_it]
        y0 = [jnp.where(lower, 0.0, ryq[i]) + _dot(a_rk[i], v_up[i]) for i in n_it]
        yield
        be =[ld(be_s, it).astype(BF16) for it in items]
        ke = [ld(ke_s, it).astype(BF16) for it in items]
        mq = [_dot_tn(be[i], pq[i]) for i in n_it]
        m_m = [(mq[i][:, 0:RW_HEAD] + eye * ld_decay(items[i])).astype(BF16) for i in n_it]
        h0 = [jnp.where(lower, 0.0, mq[i]) + _dot_tn(ke[i], v_up[i]) for i in n_it]
        ry_m = [jnp.concatenate([ry[i], m_m[i]], axis=0) for i in n_it]
        out.update(ry_m=ry_m, y0=y0, h0=h0)

    hst = [h_s[h] for h in heads]

    def carried(wave, res):
        for ci, c in enumerate(wave):
            for h in heads:
                i = ci * RW_HEADS + h
                yh = _dot(res["ry_m"][i], hst[h].astype(BF16))
                y_dst = ya_s if h % 2 else yb_s
                y_dst[c * CHUNK:(c + 1) * CHUNK, (h // 2) * LANE:(h // 2 + 1) * LANE] = yh[0:CHUNK] + res["y0"][i]
                hst[h] = yh[CHUNK:] + res["h0"][i]
                if h % 4 == 3:
                    yield

    def emit(*gens):
        live = list(gens)
        while live:
            for g in list(live):
                if next(g, StopIteration) is StopIteration:
                    live.remove(g)

    results = [dict() for _ in waves]
    emit(token_prep(waves[0]))
    for w, wave in enumerate(waves):
        gens = [chunk_local([(c, h) for c in wave for h in heads], results[w])]
        if w + 1 < len(waves):
            gens.append(token_prep(waves[w + 1]))
        if w > 0:
            gens.append(carried(waves[w - 1], results[w - 1]))
        emit(*gens)
    emit(carried(waves[-1], results[-1]))
    for h in heads:
        h_s[h] = hst[h]

    y_dir = ya_s[...] + pltpu.roll(yb_s[...], RW_WIDTH - RW_HEAD, 1)
    if not rev:
        o_ref[...] = y_dir
    else:
        r_k = vec_ref[4:5, :]
        lnx_g = vec_ref[5:6, :]
        lnx_b = vec_ref[6:7, :]
        zc = zm_ref[...].astype(F32)
        r = zc[:, 0:RW_WIDTH]
        k = zc[:, RW_WIDTH:2 * RW_WIDTH]
        v = zc[:, 2 * RW_WIDTH:3 * RW_WIDTH]
        y = yf_ref[...] + y_dir
        inv_n = 1.0 / RW_HEAD
        mu = _dot(y.astype(BF16), bd_ref[...]) * inv_n
        d = y - mu
        var = _dot((d * d).astype(BF16), bd_ref[...]) * inv_n
        yn = d * lax.rsqrt(var + GN_EPS) * lnx_g + lnx_b
        rk = r * k * r_k
        rk_hi = rk.astype(BF16)
        rk_lo = (rk - rk_hi.astype(F32)).astype(BF16)
        bonus = (_dot(rk_hi, bd_ref[...]) + _dot(rk_lo, bd_ref[...])) * v
        zg = zc[:, 3 * RW_WIDTH + 2 * LANE:RW_COLS]
        gate = _dot(_sigmoid(zg).astype(BF16), g2_ref[...])
        o_ref[...] = ((yn + bonus) * gate).astype(o_ref.dtype)


def _scan(z_rw, vecs, w2p, a2p, bd, tri, rev, y_fwd=None, g2=None):
    bsz, seq, _ = z_rw.shape
    rows = SCAN_ROWS
    nb = seq // rows
    n_chunks = rows // CHUNK

    def blk(i):
        return nb - 1 - i if rev else i

    in_specs = [
        pl.BlockSpec((None, rows, RW_COLS), lambda b, i: (b, blk(i), 0)),
        _const_spec(vecs.shape), _const_spec(w2p.shape),
        _const_spec(a2p.shape), _const_spec(bd.shape), _const_spec(tri.shape),
    ]
    args = [z_rw, vecs, w2p, a2p, bd, tri]
    scratch = [pltpu.VMEM((rows, RW_WIDTH), F32) for _ in range(8)]
    scratch += [pltpu.VMEM((n_chunks, 8, RW_WIDTH), F32),
                pltpu.VMEM((RW_HEADS, RW_HEAD, LANE), F32),
                pltpu.VMEM((rows, RW_WIDTH), F32), pltpu.VMEM((rows, RW_WIDTH), F32)]
    if rev:
        in_specs += [pl.BlockSpec((None, rows, RW_WIDTH), lambda b, i: (b, blk(i), 0)),
                     _const_spec(g2.shape)]
        args += [y_fwd, g2]
        out_dtype = BF16
    else:
        out_dtype = F32
    return pl.pallas_call(
        functools.partial(_scan_kernel, rev=rev),
        grid=(bsz, nb),
        in_specs=in_specs,
        out_specs=pl.BlockSpec((None, rows, RW_WIDTH), lambda b, i: (b, blk(i), 0)),
        out_shape=jax.ShapeDtypeStruct((bsz, seq, RW_WIDTH), out_dtype),
        scratch_shapes=scratch,
        compiler_params=pltpu.CompilerParams(dimension_semantics=("parallel", "arbitrary"),
                                             vmem_limit_bytes=VMEM_LIMIT),
        name="scan_bwd" if rev else "scan_fwd",
    )(*args)


def _rms(xf, g):
    return xf * lax.rsqrt(jnp.mean(xf * xf, axis=-1, keepdims=True) + RMS_EPS) * g


def _attn_kernel(zkv_ref, zq_ref, cos_ref, sin_ref, gq_ref, gkv_ref, wq1_ref, wq2_ref, wk_ref, wvt_ref,
                 o_ref, k_s, vt_s):
    seq = zkv_ref.shape[0]
    tq = zq_ref.shape[0]
    kt = 256
    i = pl.program_id(1)

    @pl.when(i == 0)
    def _():
        def kv_body(t, carry):
            r0 = pl.multiple_of(t * kt, kt)
            zk = zkv_ref[pl.ds(r0, kt), :].astype(F32)
            kvn = _rms(zk[:, 0:KV_LORA], gkv_ref[...]).astype(BF16)
            kn = _dot(kvn, wk_ref[...])
            cs = cos_ref[pl.ds(r0, kt), :]
            sn = sin_ref[pl.ds(r0, kt), :]
            kpe = zk[:, KV_LORA:KV_LORA + LANE] * cs + zk[:, KV_LORA + LANE:KV_LORA + 2 * LANE] * sn
            for h in range(MLA_HEADS):
                sl = slice(h * HEAD_PAD, (h + 1) * HEAD_PAD)
                k_s[pl.ds(r0, kt), sl] = (kn[:, sl] + kpe).astype(BF16)
            vt_s[:, pl.ds(r0, kt)] = _dot_nt(wvt_ref[...], kvn).astype(BF16)
            return carry
        lax.fori_loop(0, seq // kt, kv_body, 0)

    q0 = pl.multiple_of(i * tq, tq)
    qn = _rms(zq_ref[...].astype(F32), gq_ref[...]).astype(BF16)
    q1 = _dot(qn, wq1_ref[...])
    q2 = _dot(qn, wq2_ref[...])
    cs = cos_ref[pl.ds(q0, tq), :]
    sn = sin_ref[pl.ds(q0, tq), :]
    scale = (QK_NOPE + QK_ROPE) ** -0.5 * LOG2_E
    def scores(h):
        sl = slice(h * HEAD_PAD, (h + 1) * HEAD_PAD)
        qh = ((q1[:, sl] * cs + q2[:, sl] * sn) * scale).astype(BF16)
        return _dot_nt(k_s[:, sl], qh)

    outs = []
    st_next = scores(0)
    for h in range(MLA_HEADS):
        st = st_next
        if h + 1 < MLA_HEADS:
            st_next = scores(h + 1)
        m = jnp.max(st, axis=0, keepdims=True)
        p = jnp.exp2(st - m)
        l = jnp.sum(p, axis=0, keepdims=True)
        ot = _dot(vt_s[h * V_HEAD:(h + 1) * V_HEAD, :], p.astype(BF16))
        outs.append(ot / l)
    o_ref[...] = jnp.concatenate(outs, axis=0).T.astype(o_ref.dtype)


def _attn(z_kv, z_q, cos_t, sin_t, gq, gkv, wq1, wq2, wk, wvt):
    bsz, seq, _ = z_kv.shape
    tq = 256
    kw = MLA_HEADS * HEAD_PAD
    return pl.pallas_call(
        _attn_kernel,
        grid=(bsz, seq // tq),
        in_specs=[pl.BlockSpec((None, seq, ZKV_COLS), lambda b, i: (b, 0, 0)),
                  pl.BlockSpec((None, tq, Q_LORA), lambda b, i: (b, i, 0)),
                  pl.BlockSpec((None, seq, LANE), lambda b, i: (b, 0, 0)),
                  pl.BlockSpec((None, seq, LANE), lambda b, i: (b, 0, 0)),
                  _const_spec(gq.shape), _const_spec(gkv.shape), _const_spec(wq1.shape),
                  _const_spec(wq2.shape), _const_spec(wk.shape), _const_spec(wvt.shape)],
        out_specs=pl.BlockSpec((None, tq, MLA_HEADS * V_HEAD), lambda b, i: (b, i, 0)),
        out_shape=jax.ShapeDtypeStruct((bsz, seq, MLA_HEADS * V_HEAD), BF16),
        scratch_shapes=[pltpu.VMEM((seq, kw), BF16), pltpu.VMEM((MLA_HEADS * V_HEAD, seq), BF16)],
        compiler_params=pltpu.CompilerParams(dimension_semantics=("parallel", "arbitrary"),
                                             vmem_limit_bytes=VMEM_LIMIT),
        name="attn",
    )(z_kv, z_q, cos_t, sin_t, gq, gkv, wq1, wq2, wk, wvt)


def _mix_kernel(yrw_ref, ymla_ref, ga_ref, gb_ref, x_ref, mod_ref, wr_ref, wm_ref, wo_ref, ln_ref, o_ref):
    gate1 = mod_ref[2:3, :]
    br_rw = _dot(yrw_ref[...], wr_ref[...])
    br_mla = _dot(ymla_ref[...], wm_ref[...])
    mixed = (_sigmoid(ga_ref[...].astype(F32)) * br_rw
             + _sigmoid(gb_ref[...].astype(F32)) * br_mla)
    out = _dot(mixed.astype(BF16), wo_ref[...])
    t = DN_ALPHA * x_ref[...] + (1.0 + gate1) * out
    o_ref[...] = _layer_norm(t, ln_ref[0:1, :], ln_ref[1:2, :])


def _mix(y_rw, y_mla, z_g, x, mod3, wr, wm, wo, ln):
    bsz, seq, _ = x.shape
    tm = 512
    row = lambda n: pl.BlockSpec((None, tm, n), lambda b, i: (b, i, 0))
    return pl.pallas_call(
        _mix_kernel,
        grid=(bsz, seq // tm),
        in_specs=[row(RW_WIDTH), row(MLA_HEADS * V_HEAD),
                  pl.BlockSpec((None, tm, D_MODEL), lambda b, i: (b, i, 0)),
                  pl.BlockSpec((None, tm, D_MODEL), lambda b, i: (b, i, 1)),
                  row(D_MODEL),
                  pl.BlockSpec((None, 6, D_MODEL), lambda b, i: (b, 0, 0)),
                  _const_spec(wr.shape), _const_spec(wm.shape), _const_spec(wo.shape),
                  _const_spec(ln.shape)],
        out_specs=row(D_MODEL),
        out_shape=jax.ShapeDtypeStruct((bsz, seq, D_MODEL), F32),
        compiler_params=pltpu.CompilerParams(dimension_semantics=("parallel", "parallel"),
                                             vmem_limit_bytes=VMEM_LIMIT),
        name="mix",
    )(y_rw, y_mla, z_g, z_g, x, mod3, wr, wm, wo, ln)


def _ffn_kernel(x_ref, mod_ref, w1_ref, w2_ref, ln_ref, o_ref):
    shift = mod_ref[3:4, :]
    scale = mod_ref[4:5, :]
    gate2 = mod_ref[5:6, :]
    x1 = x_ref[...]
    h = (x1 * (1.0 + scale) + shift).astype(BF16)
    kc = 1024
    acc = jnp.zeros(x1.shape, F32)
    for c in range(D_FF // kc):
        u = jnp.maximum(_dot(h, w1_ref[:, c * kc:(c + 1) * kc]), 0.0)
        acc = acc + _dot((u * u).astype(BF16), w2_ref[c * kc:(c + 1) * kc, :])
    t = DN_ALPHA * x1 + (1.0 + gate2) * acc
    o_ref[...] = _layer_norm(t, ln_ref[0:1, :], ln_ref[1:2, :])


def _ffn(x1, mod3, w1, w2, ln):
    bsz, seq, _ = x1.shape
    tm = 512
    row = pl.BlockSpec((None, tm, D_MODEL), lambda b, i: (b, i, 0))
    return pl.pallas_call(
        _ffn_kernel,
        grid=(bsz, seq // tm),
        in_specs=[row, pl.BlockSpec((None, 6, D_MODEL), lambda b, i: (b, 0, 0)),
                  _const_spec(w1.shape), _const_spec(w2.shape), _const_spec(ln.shape)],
        out_specs=row,
        out_shape=jax.ShapeDtypeStruct((bsz, seq, D_MODEL), F32),
        compiler_params=pltpu.CompilerParams(dimension_semantics=("parallel", "parallel"),
                                             vmem_limit_bytes=VMEM_LIMIT),
        name="ffn",
    )(x1, mod3, w1, w2, ln)


def _pad_cols(w, left, total):
    return jnp.pad(w, ((0, 0), (left, total - left - w.shape[1])))


def _inproj_weight(w_in):
    o = 0
    w_rw = w_in[:, o:o + RW_COLS]; o += RW_COLS
    w_q = w_in[:, o:o + Q_LORA]; o += Q_LORA
    w_kv = w_in[:, o:o + KV_LORA]; o += KV_LORA
    w_kr = w_in[:, o:o + QK_ROPE]; o += QK_ROPE
    w_g = w_in[:, o:o + 2 * D_MODEL]
    half = QK_ROPE // 2
    w_kr_rot = jnp.concatenate([-w_kr[:, half:], w_kr[:, :half]], axis=1)
    w_all = jnp.concatenate([w_rw, w_kv, _pad_cols(w_kr, QK_NOPE, LANE), _pad_cols(w_kr_rot, QK_NOPE, LANE),
                             w_q, w_g], axis=1)
    return w_all.astype(BF16)


def _mla_weights(w_uq, w_ukv):
    half = QK_ROPE // 2
    q = w_uq.reshape(Q_LORA, MLA_HEADS, QK_NOPE + QK_ROPE)
    q_nope, q_1, q_2 = q[..., :QK_NOPE], q[..., QK_NOPE:QK_NOPE + half], q[..., QK_NOPE + half:]
    zpad = jnp.zeros((Q_LORA, MLA_HEADS, HEAD_PAD - QK_NOPE - QK_ROPE), F32)
    wq1 = jnp.concatenate([q_nope, q_1, q_2, zpad], axis=-1).reshape(Q_LORA, MLA_HEADS * HEAD_PAD)
    wq2 = jnp.concatenate([jnp.zeros_like(q_nope), -q_2, q_1, zpad], axis=-1).reshape(Q_LORA, MLA_HEADS * HEAD_PAD)
    kv = w_ukv.reshape(KV_LORA, MLA_HEADS, QK_NOPE + V_HEAD)
    zhalf = jnp.zeros((KV_LORA, MLA_HEADS, HEAD_PAD - QK_NOPE), F32)
    wk = jnp.concatenate([kv[..., :QK_NOPE], zhalf], axis=-1).reshape(KV_LORA, MLA_HEADS * HEAD_PAD)
    wvt = kv[..., QK_NOPE:].reshape(KV_LORA, MLA_HEADS * V_HEAD).T
    return wq1.astype(BF16), wq2.astype(BF16), wk.astype(BF16), wvt.astype(BF16)


def _scan_constants(rev):
    rows = np.arange(SCAN_ROWS)
    same = (rows[:, None] // CHUNK) == (rows[None, :] // CHUNK)
    tri = same & ((rows[None, :] >= rows[:, None]) if rev else (rows[None, :] <= rows[:, None]))
    ch = np.arange(RW_WIDTH)
    bd = (ch[:, None] // RW_HEAD) == (ch[None, :] // RW_HEAD)
    return jnp.asarray(tri, BF16), jnp.asarray(bd, BF16)


def _lora_pair(w, d):
    zero = jnp.zeros_like(w[0])
    return jnp.concatenate([zero, w[1]] if d else [w[0], zero], axis=0).astype(BF16)


def kernel(x, c, positions, w_ada, b_ada, w_in, rw_conv, rw_w0, rw_w2, rw_a0, rw_a2, rw_k_k, rw_k_a, rw_r_k, rw_g2, rw_lnx_g, rw_lnx_b, mla_q_norm_g, mla_kv_norm_g, mla_w_uq, mla_w_ukv, w_br_rwkv, w_br_mla, w_out, ln1_g, ln1_b, w_ff1, w_ff2, ln2_g, ln2_b):
    bsz, seq, _ = x.shape
    cos_t, sin_t = _rope_tables(positions)
    for l in range(DEPTH):
        mod3 = _ada(c, w_ada[l], b_ada[l]).reshape(bsz, 6, D_MODEL)
        z_rw, z_kv, z_q, z_g = _inproj(x, mod3, _inproj_weight(w_in[l]), rw_conv[l])

        zero = jnp.zeros((RW_WIDTH,), F32)
        y_rw = None
        for d in (0, 1):
            vecs = jnp.stack([rw_w0[l, d], rw_a0[l, d], rw_k_k[l], rw_k_a[l], rw_r_k[l],
                              rw_lnx_g[l], rw_lnx_b[l], zero])
            tri, bd = _scan_constants(bool(d))
            y_rw = _scan(z_rw, vecs, _lora_pair(rw_w2[l], d), _lora_pair(rw_a2[l], d), bd, tri,
                         rev=bool(d), y_fwd=y_rw, g2=rw_g2[l].astype(BF16))

        wq1, wq2, wk, wvt = _mla_weights(mla_w_uq[l], mla_w_ukv[l])
        y_mla = _attn(z_kv, z_q, cos_t, sin_t, mla_q_norm_g[l].reshape(1, Q_LORA),
                      mla_kv_norm_g[l].reshape(1, KV_LORA), wq1, wq2, wk, wvt)

        x = _mix(y_rw, y_mla, z_g, x, mod3, w_br_rwkv[l].astype(BF16), w_br_mla[l].astype(BF16),
                 w_out[l].astype(BF16), jnp.stack([ln1_g[l], ln1_b[l]]))
        x = _ffn(x, mod3, w_ff1[l].astype(BF16), w_ff2[l].astype(BF16), jnp.stack([ln2_g[l], ln2_b[l]]))
    return x
```

```python
import functools

import numpy as np
import jax
import jax.numpy as jnp
from jax import lax
from jax.experimental import pallas as pl
from jax.experimental.pallas import tpu as pltpu

F32 = jnp.float32
BF16 = jnp.bfloat16

D_MODEL = 1024
RW_HEADS = 8
RW_HEAD = 64
RW_WIDTH = RW_HEADS * RW_HEAD
DECAY_LORA = 64
AAA_LORA = 64
GATE_LORA = 128
RW_COLS = 3 * RW_WIDTH + 2 * DECAY_LORA + 2 * AAA_LORA + GATE_LORA
MLA_HEADS = 8
QK_NOPE = 64
QK_ROPE = 32
V_HEAD = 64
Q_LORA = 384
KV_LORA = 256
ROPE_THETA = 10000.0
D_FF = 4 * D_MODEL
LN_EPS = 1e-5
RMS_EPS = 1e-6
GN_EPS = 64e-5
L2_EPS = 1e-12
DEPTH = 1
DN_ALPHA = (2.0 * DEPTH) ** 0.25
LOG2_E = 1.4426950408889634
DECAY_SCALE = 0.6065306597126334

LANE = 128
CHUNK = 64
SCAN_ROWS = 512
SCAN_WAVE = 4
X_HALO_ROWS = 8
HEAD_PAD = 128
ATTN_KEY_PARTS = 4
VT_ROWS = 80
ZKV_COLS = KV_LORA + 2 * LANE
VMEM_LIMIT = 56 * 1024 * 1024


def _dot(a, b):
    return jnp.dot(a, b, preferred_element_type=F32)


def _dot_nt(a, b):
    return lax.dot_general(a, b, (((1,), (1,)), ((), ())), preferred_element_type=F32)


def _dot_tn(a, b):
    return lax.dot_general(a, b, (((0,), (0,)), ((), ())), preferred_element_type=F32)


def _const_spec(shape):
    zeros = (0,) * len(shape)
    return pl.BlockSpec(shape, lambda *_: zeros)


def _layer_norm(t, g, b):
    mu = jnp.mean(t, axis=-1, keepdims=True)
    d = t - mu
    var = jnp.mean(d * d, axis=-1, keepdims=True)
    return d * lax.rsqrt(var + LN_EPS) * g + b


def _ada_kernel(c_ref, w_ref, b_ref, o_ref):
    c = c_ref[...]
    act = c * jax.nn.sigmoid(c)
    o_ref[...] = _dot(act.astype(BF16), w_ref[...].astype(BF16)) + b_ref[...]


def _ada(c, w, b):
    bsz = c.shape[0]
    n = w.shape[1]
    tn = 1536
    return pl.pallas_call(
        _ada_kernel,
        grid=(n // tn,),
        in_specs=[_const_spec((bsz, D_MODEL)),
                  pl.BlockSpec((D_MODEL, tn), lambda j: (0, j)),
                  pl.BlockSpec((1, tn), lambda j: (0, j))],
        out_specs=pl.BlockSpec((bsz, tn), lambda j: (0, j)),
        out_shape=jax.ShapeDtypeStruct((bsz, n), F32),
        name="ada",
    )(c, w, b.reshape(1, n))


def _rope_kernel(pos_ref, inv_ref, place_ref, one_ref, cos_ref, sin_ref):
    ang = inv_ref[...] * pos_ref[...].astype(F32)

    def table(t):
        hi = t.astype(BF16)
        lo = (t - hi.astype(F32)).astype(BF16)
        return _dot_tn(hi, place_ref[...]) + _dot_tn(lo, place_ref[...])

    cos_ref[...] = table(jnp.cos(ang)) + one_ref[...]
    sin_ref[...] = table(jnp.sin(ang))


def _rope_tables(positions):
    bsz, seq = positions.shape
    half = QK_ROPE // 2
    inv = (ROPE_THETA ** (-np.arange(half, dtype=np.float32) / half)).reshape(half, 1)
    place = np.zeros((half, LANE), np.float32)
    place[np.arange(half), QK_NOPE + np.arange(half)] = 1.0
    place[np.arange(half), QK_NOPE + half + np.arange(half)] = 1.0
    one = np.zeros((1, LANE), np.float32)
    one[0, :QK_NOPE] = 1.0
    spec = pl.BlockSpec((None, seq, LANE), lambda b: (b, 0, 0))
    return pl.pallas_call(
        _rope_kernel,
        grid=(bsz,),
        in_specs=[pl.BlockSpec((None, 1, seq), lambda b: (b, 0, 0)), _const_spec((half, 1)),
                  _const_spec((half, LANE)), _const_spec((1, LANE))],
        out_specs=[spec, spec],
        out_shape=[jax.ShapeDtypeStruct((bsz, seq, LANE), F32)] * 2,
        name="rope",
    )(positions.reshape(bsz, 1, seq), jnp.asarray(inv), jnp.asarray(place, BF16), jnp.asarray(one))


def _inproj_kernel(x_ref, xp_ref, xn_ref, mod_ref, w_ref, cw_ref, zrw_ref, zkv_ref, zq_ref, zg_ref):
    i = pl.program_id(1)
    nb = pl.num_programs(1)
    rows = x_ref.shape[0]
    shift = mod_ref[0:1, :]
    scale = mod_ref[1:2, :]
    h = (x_ref[...] * (1.0 + scale) + shift).astype(BF16)
    col = RW_COLS
    for o_ref in (zkv_ref, zq_ref, zg_ref):
        n = o_ref.shape[-1]
        o_ref[...] = _dot(h, w_ref[:, col:col + n]).astype(o_ref.dtype)
        col += n

    z = _dot(h, w_ref[:, 0:RW_COLS])
    x_halo = jnp.concatenate([xp_ref[...], xn_ref[...]], axis=0)
    z_halo = _dot((x_halo * (1.0 + scale) + shift).astype(BF16), w_ref[:, 0:RW_COLS])
    prev_row = z_halo[X_HALO_ROWS - 1:X_HALO_ROWS, :] * jnp.where(i > 0, 1.0, 0.0)
    next_row = z_halo[X_HALO_ROWS:X_HALO_ROWS + 1, :] * jnp.where(i < nb - 1, 1.0, 0.0)
    row_id = lax.broadcasted_iota(jnp.int32, (rows, 1), 0)
    z_dn = jnp.where(row_id == 0, prev_row, pltpu.roll(z, 1, 0))
    z_up = jnp.where(row_id == rows - 1, next_row, pltpu.roll(z, rows - 1, 0))
    zrw_ref[...] = (cw_ref[0:1, :] * z_dn + cw_ref[1:2, :] * z + cw_ref[2:3, :] * z_up).astype(zrw_ref.dtype)


def _inproj(x, mod3, w_all, conv_w):
    bsz, seq, _ = x.shape
    tm = 512
    hpb = tm // X_HALO_ROWS
    n_halo = seq // X_HALO_ROWS
    widths = (RW_COLS, ZKV_COLS, Q_LORA, 2 * D_MODEL)
    return pl.pallas_call(
        _inproj_kernel,
        grid=(bsz, seq // tm),
        in_specs=[pl.BlockSpec((None, tm, D_MODEL), lambda b, i: (b, i, 0)),
                  pl.BlockSpec((None, X_HALO_ROWS, D_MODEL), lambda b, i: (b, jnp.maximum(i * hpb - 1, 0), 0)),
                  pl.BlockSpec((None, X_HALO_ROWS, D_MODEL),
                               lambda b, i: (b, jnp.minimum((i + 1) * hpb, n_halo - 1), 0)),
                  pl.BlockSpec((None, 6, D_MODEL), lambda b, i: (b, 0, 0)),
                  _const_spec(w_all.shape), _const_spec(conv_w.shape)],
        out_specs=[pl.BlockSpec((None, tm, n), lambda b, i: (b, i, 0)) for n in widths],
        out_shape=[jax.ShapeDtypeStruct((bsz, seq, n), BF16) for n in widths],
        compiler_params=pltpu.CompilerParams(dimension_semantics=("parallel", "parallel"),
                                             vmem_limit_bytes=VMEM_LIMIT),
        name="inproj",
    )(x, x, x, mod3, w_all, conv_w)


def _sigmoid(x):
    return 0.5 + 0.5 * jnp.tanh(0.5 * x)


def _scan_kernel(*refs, rev):
    if rev:
        (zm_ref, vec_ref, w2_ref, a2_ref, bd_ref, tri_ref,
         yf_ref, g2_ref, o_ref,
         at_s, rt_s, bh_s, kh_s, be_s, ke_s, v_s, vsw_s, eg_s, h_s, ya_s, yb_s) = refs
    else:
        (zm_ref, vec_ref, w2_ref, a2_ref, bd_ref, tri_ref,
         o_ref,
         at_s, rt_s, bh_s, kh_s, be_s, ke_s, v_s, vsw_s, eg_s, h_s, ya_s, yb_s) = refs
    rows = zm_ref.shape[0]
    n_chunks = rows // CHUNK
    i = pl.program_id(1)

    @pl.when(i == 0)
    def _():
        h_s[...] = jnp.zeros_like(h_s)

    def token_prep(wave):
        cs = sorted(wave)
        n_r = len(cs) * CHUNK
        rs = slice(cs[0] * CHUNK, cs[0] * CHUNK + n_r)
        zc = zm_ref[rs, :].astype(F32)
        r = zc[:, 0:RW_WIDTH]
        k = zc[:, RW_WIDTH:2 * RW_WIDTH]
        v = zc[:, 2 * RW_WIDTH:3 * RW_WIDTH]
        zw = zc[:, 3 * RW_WIDTH:3 * RW_WIDTH + LANE]
        za = zc[:, 3 * RW_WIDTH + LANE:3 * RW_WIDTH + 2 * LANE]
        w0 = vec_ref[0:1, :]
        a0 = vec_ref[1:2, :]
        k_k = vec_ref[2:3, :]
        k_a = vec_ref[3:4, :]
        v_s[rs, :] = v
        vsw_s[rs, :] = pltpu.roll(v, RW_HEAD, 1)
        w_lin = w0 + _dot(jnp.tanh(zw).astype(BF16), w2_ref[...])
        rate_lin = a0 + _dot(za.astype(BF16), a2_ref[...])
        kkv = k * k_k
        ssq = _dot((kkv * kkv).astype(BF16), bd_ref[...])
        yield
        lw = -DECAY_SCALE * _sigmoid(w_lin)
        lw_hi = lw.astype(BF16)
        lw_lo = (lw - lw_hi.astype(F32)).astype(BF16)
        tri = tri_ref[0:n_r, 0:n_r]
        g_in = _dot(tri, lw_hi) + _dot(tri, lw_lo)
        yield
        rate = _sigmoid(rate_lin)
        kk = kkv * lax.rsqrt(jnp.maximum(ssq, L2_EPS * L2_EPS))
        kd = k * (1.0 + (rate - 1.0) * k_a)
        av = -kk
        bv = kk * rate
        yield
        g_ex = g_in - lw
        last = 0 if rev else CHUNK - 1
        g_tot_rows = [g_in[j * CHUNK + last:j * CHUNK + last + 1, :] for j in range(len(cs))]
        eg_rows = [jnp.exp(t) for t in g_tot_rows]
        for j, c in enumerate(cs):
            eg_s[c] = jnp.broadcast_to(eg_rows[j], (8, RW_WIDTH))
        e_neg = jnp.exp(-g_in)
        e_end = e_neg * jnp.concatenate([jnp.broadcast_to(t, (CHUNK, RW_WIDTH)) for t in eg_rows], axis=0)
        at_s[rs, :] = av * jnp.exp(g_ex)
        rt_s[rs, :] = r * jnp.exp(g_in)
        yield
        bh_s[rs, :] = bv * e_neg
        kh_s[rs, :] = kd * e_neg
        be_s[rs, :] = bv * e_end
        ke_s[rs, :] = kd * e_end

    ri = lax.broadcasted_iota(jnp.int32, (CHUNK, CHUNK), 0)
    ci = lax.broadcasted_iota(jnp.int32, (CHUNK, CHUNK), 1)
    m_strict = (ci > ri) if rev else (ci < ri)
    m_incl = (ci >= ri) if rev else (ci <= ri)
    eye = jnp.where(ri == ci, 1.0, 0.0).astype(F32)
    lower = lax.broadcasted_iota(jnp.int32, (CHUNK, LANE), 1) < RW_HEAD
    eye_up = jnp.where(lax.broadcasted_iota(jnp.int32, (CHUNK, LANE), 1)
                       == lax.broadcasted_iota(jnp.int32, (CHUNK, LANE), 0) + RW_HEAD, 1.0, 0.0).astype(F32)
    zeros_half = jnp.zeros((CHUNK, RW_HEAD), F32)

    def pad(t):
        return jnp.concatenate([t, zeros_half], axis=1)

    heads = range(RW_HEADS)
    chunk_order = list(range(n_chunks - 1, -1, -1)) if rev else list(range(n_chunks))
    waves = [chunk_order[w:w + SCAN_WAVE] for w in range(0, n_chunks, SCAN_WAVE)]

    def ld(ref, item):
        c, h = item
        return ref[c * CHUNK:(c + 1) * CHUNK, h * RW_HEAD:(h + 1) * RW_HEAD]

    def ld_v_upper(item):
        c, h = item
        src = v_s if h % 2 else vsw_s
        blk = src[c * CHUNK:(c + 1) * CHUNK, (h // 2) * LANE:(h // 2 + 1) * LANE]
        return jnp.where(lower, 0.0, blk).astype(BF16)

    def ld_decay(item):
        c, h = item
        return eg_s[c][0:1, h * RW_HEAD:(h + 1) * RW_HEAD]

    def chunk_local(items, out):
        n_it = range(len(items))
        rt = [ld(rt_s, it) for it in items]
        atb = [ld(at_s, it).astype(BF16) for it in items]
        lhs = [jnp.concatenate([atb[i], rt[i].astype(BF16)], axis=0) for i in n_it]
        a_b = [_dot_nt(lhs[i], ld(bh_s, items[i]).astype(BF16)) for i in n_it]
        a_k = [_dot_nt(lhs[i], ld(kh_s, items[i]).astype(BF16)) for i in n_it]
        yield
        a_ab =[jnp.where(m_strict, a_b[i][0:CHUNK], 0.0) for i in n_it]
        a_ak = [jnp.where(m_strict, a_k[i][0:CHUNK], 0.0).astype(BF16) for i in n_it]
        a_rb = [jnp.where(m_incl, a_b[i][CHUNK:], 0.0).astype(BF16) for i in n_it]
        a_rk = [jnp.where(m_incl, a_k[i][CHUNK:], 0.0).astype(BF16) for i in n_it]
        v_up = [ld_v_upper(it) for it in items]
        akv = [_dot(a_ak[i], v_up[i]) for i in n_it]
        yield

        z = [pad(a_ab[i]) + eye_up for i in n_it]
        for _ in range(6):
            z = [_dot(z[i][:, 0:CHUNK].astype(BF16), z[i].astype(BF16)) + jnp.where(lower, 0.0, z[i])
                 for i in n_it]
            yield
        tb =[z[i][:, CHUNK:].astype(BF16) for i in n_it]

        w_m = [(pad(ld(at_s, items[i])) + akv[i]).astype(BF16) for i in n_it]
        pq = [_dot(tb[i], w_m[i]).astype(BF16) for i in n_it]
        yield
        ryq =[_dot(a_rb[i], pq[i]) for i in n_it]
        ry = [(rt[i] + ryq[i][:, 0:RW_HEAD]).astype(BF16) for i in n_it]
        y0 = [jnp.where(lower, 0.0, ryq[i]) + _dot(a_rk[i], v_up[i]) for i in n_it]
        yield
        be =[ld(be_s, it).astype(BF16) for it in items]
        ke = [ld(ke_s, it).astype(BF16) for it in items]
        mq = [_dot_tn(be[i], pq[i]) for i in n_it]
        m_m = [(mq[i][:, 0:RW_HEAD] + eye * ld_decay(items[i])).astype(BF16) for i in n_it]
        h0 = [jnp.where(lower, 0.0, mq[i]) + _dot_tn(ke[i], v_up[i]) for i in n_it]
        ry_m = [jnp.concatenate([ry[i], m_m[i]], axis=0) for i in n_it]
        out.update(ry_m=ry_m, y0=y0, h0=h0)

    hst = [h_s[h] for h in heads]

    def carried(wave, res):
        for ci, c in enumerate(wave):
            for h in heads:
                i = ci * RW_HEADS + h
                yh = _dot(res["ry_m"][i], hst[h].astype(BF16))
                y_dst = ya_s if h % 2 else yb_s
                y_dst[c * CHUNK:(c + 1) * CHUNK, (h // 2) * LANE:(h // 2 + 1) * LANE] = yh[0:CHUNK] + res["y0"][i]
                hst[h] = yh[CHUNK:] + res["h0"][i]
                if h % 4 == 3:
                    yield

    def emit(*gens):
        live = list(gens)
        while live:
            for g in list(live):
                if next(g, StopIteration) is StopIteration:
                    live.remove(g)

    results = [dict() for _ in waves]
    emit(token_prep(waves[0]))
    for w, wave in enumerate(waves):
        gens = [chunk_local([(c, h) for c in wave for h in heads], results[w])]
        if w + 1 < len(waves):
            gens.append(token_prep(waves[w + 1]))
        if w > 0:
            gens.append(carried(waves[w - 1], results[w - 1]))
        emit(*gens)
    emit(carried(waves[-1], results[-1]))
    for h in heads:
        h_s[h] = hst[h]

    y_dir = ya_s[...] + pltpu.roll(yb_s[...], RW_WIDTH - RW_HEAD, 1)
    if not rev:
        o_ref[...] = y_dir
    else:
        r_k = vec_ref[4:5, :]
        lnx_g = vec_ref[5:6, :]
        lnx_b = vec_ref[6:7, :]
        zc = zm_ref[...].astype(F32)
        r = zc[:, 0:RW_WIDTH]
        k = zc[:, RW_WIDTH:2 * RW_WIDTH]
        v = zc[:, 2 * RW_WIDTH:3 * RW_WIDTH]
        y = yf_ref[...] + y_dir
        inv_n = 1.0 / RW_HEAD
        mu = _dot(y.astype(BF16), bd_ref[...]) * inv_n
        d = y - mu
        var = _dot((d * d).astype(BF16), bd_ref[...]) * inv_n
        yn = d * lax.rsqrt(var + GN_EPS) * lnx_g + lnx_b
        rk = r * k * r_k
        rk_hi = rk.astype(BF16)
        rk_lo = (rk - rk_hi.astype(F32)).astype(BF16)
        bonus = (_dot(rk_hi, bd_ref[...]) + _dot(rk_lo, bd_ref[...])) * v
        zg = zc[:, 3 * RW_WIDTH + 2 * LANE:RW_COLS]
        gate = _dot(_sigmoid(zg).astype(BF16), g2_ref[...])
        o_ref[...] = ((yn + bonus) * gate).astype(o_ref.dtype)


def _scan(z_rw, vecs, w2p, a2p, bd, tri, rev, y_fwd=None, g2=None):
    bsz, seq, _ = z_rw.shape
    rows = SCAN_ROWS
    nb = seq // rows
    n_chunks = rows // CHUNK

    def blk(i):
        return nb - 1 - i if rev else i

    in_specs = [
        pl.BlockSpec((None, rows, RW_COLS), lambda b, i: (b, blk(i), 0)),
        _const_spec(vecs.shape), _const_spec(w2p.shape),
        _const_spec(a2p.shape), _const_spec(bd.shape), _const_spec(tri.shape),
    ]
    args = [z_rw, vecs, w2p, a2p, bd, tri]
    scratch = [pltpu.VMEM((rows, RW_WIDTH), F32) for _ in range(8)]
    scratch += [pltpu.VMEM((n_chunks, 8, RW_WIDTH), F32),
                pltpu.VMEM((RW_HEADS, RW_HEAD, LANE), F32),
                pltpu.VMEM((rows, RW_WIDTH), F32), pltpu.VMEM((rows, RW_WIDTH), F32)]
    if rev:
        in_specs += [pl.BlockSpec((None, rows, RW_WIDTH), lambda b, i: (b, blk(i), 0)),
                     _const_spec(g2.shape)]
        args += [y_fwd, g2]
        out_dtype = BF16
    else:
        out_dtype = F32
    return pl.pallas_call(
        functools.partial(_scan_kernel, rev=rev),
        grid=(bsz, nb),
        in_specs=in_specs,
        out_specs=pl.BlockSpec((None, rows, RW_WIDTH), lambda b, i: (b, blk(i), 0)),
        out_shape=jax.ShapeDtypeStruct((bsz, seq, RW_WIDTH), out_dtype),
        scratch_shapes=scratch,
        compiler_params=pltpu.CompilerParams(dimension_semantics=("parallel", "arbitrary"),
                                             vmem_limit_bytes=VMEM_LIMIT),
        name="scan_bwd" if rev else "scan_fwd",
    )(*args)


def _rms(xf, g):
    return xf * lax.rsqrt(jnp.mean(xf * xf, axis=-1, keepdims=True) + RMS_EPS) * g


def _attn_kernel(zkv_ref, zq_ref, cos_ref, sin_ref, gq_ref, gkv_ref, wq1_ref, wk_ref, wvt_ref,
                 o_ref, k_s, vt_s):
    seq = zkv_ref.shape[0]
    tq = zq_ref.shape[0]
    kt = 256
    i = pl.program_id(1)

    @pl.when(i == 0)
    def _():
        def kv_body(t, carry):
            r0 = pl.multiple_of(t * kt, kt)
            zk = zkv_ref[pl.ds(r0, kt), :].astype(F32)
            kvn = _rms(zk[:, 0:KV_LORA], gkv_ref[...]).astype(BF16)
            kn = _dot(kvn, wk_ref[...])
            cs = cos_ref[pl.ds(r0, kt), :]
            sn = sin_ref[pl.ds(r0, kt), :]
            kpe = zk[:, KV_LORA:KV_LORA + LANE] * cs + zk[:, KV_LORA + LANE:KV_LORA + 2 * LANE] * sn
            for h in range(MLA_HEADS):
                sl = slice(h * HEAD_PAD, (h + 1) * HEAD_PAD)
                k_s[pl.ds(r0, kt), sl] = (kn[:, sl] + kpe).astype(BF16)
            vt = _dot_nt(wvt_ref[...], kvn).astype(BF16)
            ones = jnp.ones((VT_ROWS - V_HEAD, kt), BF16)
            for h in range(MLA_HEADS):
                vt_s[h * VT_ROWS:(h + 1) * VT_ROWS, pl.ds(r0, kt)] = jnp.concatenate(
                    [vt[h * V_HEAD:(h + 1) * V_HEAD], ones], axis=0)
            return carry
        lax.fori_loop(0, seq // kt, kv_body, 0)

    q0 = pl.multiple_of(i * tq, tq)
    qn = _rms(zq_ref[...].astype(F32), gq_ref[...]).astype(BF16)
    q1 = _dot(qn, wq1_ref[...])
    half = QK_ROPE // 2
    lane = lax.broadcasted_iota(jnp.int32, q1.shape, 1) % HEAD_PAD
    q2 = jnp.where((lane >= QK_NOPE) & (lane < QK_NOPE + half), -pltpu.roll(q1, q1.shape[1] - half, 1),
                   jnp.where((lane >= QK_NOPE + half) & (lane < QK_NOPE + QK_ROPE), pltpu.roll(q1, half, 1), 0.0))
    cs = cos_ref[pl.ds(q0, tq), :]
    sn = sin_ref[pl.ds(q0, tq), :]
    scale = (QK_NOPE + QK_ROPE) ** -0.5 * LOG2_E

    kparts = [slice(j * (seq // ATTN_KEY_PARTS), (j + 1) * (seq // ATTN_KEY_PARTS)) for j in range(ATTN_KEY_PARTS)]

    def scores(h):
        sl = slice(h * HEAD_PAD, (h + 1) * HEAD_PAD)
        qh = ((q1[:, sl] * cs + q2[:, sl] * sn) * scale).astype(BF16)
        return [_dot_nt(k_s[kp, sl], qh) for kp in kparts]

    outs = []
    st_next = scores(0)
    for h in range(MLA_HEADS):
        st = st_next
        if h + 1 < MLA_HEADS:
            st_next = scores(h + 1)
        m = functools.reduce(jnp.maximum, [jnp.max(t, axis=0, keepdims=True) for t in st])
        ps = [jnp.exp2((t - m).astype(BF16)) for t in st]
        ol = sum(_dot(vt_s[h * VT_ROWS:(h + 1) * VT_ROWS, kp], p) for kp, p in zip(kparts, ps))
        outs.append(ol[0:V_HEAD] / ol[V_HEAD:V_HEAD + 1])
    o_ref[...] = jnp.concatenate(outs, axis=0).T.astype(o_ref.dtype)


def _attn(z_kv, z_q, cos_t, sin_t, gq, gkv, wq1, wk, wvt):
    bsz, seq, _ = z_kv.shape
    tq = 256
    kw = MLA_HEADS * HEAD_PAD
    return pl.pallas_call(
        _attn_kernel,
        grid=(bsz, seq // tq),
        in_specs=[pl.BlockSpec((None, seq, ZKV_COLS), lambda b, i: (b, 0, 0)),
                  pl.BlockSpec((None, tq, Q_LORA), lambda b, i: (b, i, 0)),
                  pl.BlockSpec((None, seq, LANE), lambda b, i: (b, 0, 0)),
                  pl.BlockSpec((None, seq, LANE), lambda b, i: (b, 0, 0)),
                  _const_spec(gq.shape), _const_spec(gkv.shape), _const_spec(wq1.shape),
                  _const_spec(wk.shape), _const_spec(wvt.shape)],
        out_specs=pl.BlockSpec((None, tq, MLA_HEADS * V_HEAD), lambda b, i: (b, i, 0)),
        out_shape=jax.ShapeDtypeStruct((bsz, seq, MLA_HEADS * V_HEAD), BF16),
        scratch_shapes=[pltpu.VMEM((seq, kw), BF16), pltpu.VMEM((MLA_HEADS * VT_ROWS, seq), BF16)],
        compiler_params=pltpu.CompilerParams(dimension_semantics=("parallel", "arbitrary"),
                                             vmem_limit_bytes=VMEM_LIMIT),
        name="attn",
    )(z_kv, z_q, cos_t, sin_t, gq, gkv, wq1, wk, wvt)


def _mix_kernel(yrw_ref, ymla_ref, ga_ref, gb_ref, x_ref, mod_ref, wr_ref, wm_ref, wo_ref, ln_ref, o_ref):
    gate1 = mod_ref[2:3, :]
    br_rw = _dot(yrw_ref[...], wr_ref[...])
    br_mla = _dot(ymla_ref[...], wm_ref[...])
    mixed = (_sigmoid(ga_ref[...].astype(F32)) * br_rw
             + _sigmoid(gb_ref[...].astype(F32)) * br_mla)
    out = _dot(mixed.astype(BF16), wo_ref[...])
    t = DN_ALPHA * x_ref[...] + (1.0 + gate1) * out
    o_ref[...] = _layer_norm(t, ln_ref[0:1, :], ln_ref[1:2, :])


def _mix(y_rw, y_mla, z_g, x, mod3, wr, wm, wo, ln):
    bsz, seq, _ = x.shape
    tm = 512
    row = lambda n: pl.BlockSpec((None, tm, n), lambda b, i: (b, i, 0))
    return pl.pallas_call(
        _mix_kernel,
        grid=(bsz, seq // tm),
        in_specs=[row(RW_WIDTH), row(MLA_HEADS * V_HEAD),
                  pl.BlockSpec((None, tm, D_MODEL), lambda b, i: (b, i, 0)),
                  pl.BlockSpec((None, tm, D_MODEL), lambda b, i: (b, i, 1)),
                  row(D_MODEL),
                  pl.BlockSpec((None, 6, D_MODEL), lambda b, i: (b, 0, 0)),
                  _const_spec(wr.shape), _const_spec(wm.shape), _const_spec(wo.shape),
                  _const_spec(ln.shape)],
        out_specs=row(D_MODEL),
        out_shape=jax.ShapeDtypeStruct((bsz, seq, D_MODEL), F32),
        compiler_params=pltpu.CompilerParams(dimension_semantics=("parallel", "parallel"),
                                             vmem_limit_bytes=VMEM_LIMIT),
        name="mix",
    )(y_rw, y_mla, z_g, z_g, x, mod3, wr, wm, wo, ln)


def _ffn_kernel(x_ref, mod_ref, w1_ref, w2_ref, ln_ref, o_ref):
    shift = mod_ref[3:4, :]
    scale = mod_ref[4:5, :]
    gate2 = mod_ref[5:6, :]
    x1 = x_ref[...]
    h = (x1 * (1.0 + scale) + shift).astype(BF16)
    kc = 1024
    acc = jnp.zeros(x1.shape, F32)
    for c in range(D_FF // kc):
        u = jnp.maximum(_dot(h, w1_ref[:, c * kc:(c + 1) * kc]), 0.0)
        acc = acc + _dot((u * u).astype(BF16), w2_ref[c * kc:(c + 1) * kc, :])
    t = DN_ALPHA * x1 + (1.0 + gate2) * acc
    o_ref[...] = _layer_norm(t, ln_ref[0:1, :], ln_ref[1:2, :])


def _ffn(x1, mod3, w1, w2, ln):
    bsz, seq, _ = x1.shape
    tm = 512
    row = pl.BlockSpec((None, tm, D_MODEL), lambda b, i: (b, i, 0))
    return pl.pallas_call(
        _ffn_kernel,
        grid=(bsz, seq // tm),
        in_specs=[row, pl.BlockSpec((None, 6, D_MODEL), lambda b, i: (b, 0, 0)),
                  _const_spec(w1.shape), _const_spec(w2.shape), _const_spec(ln.shape)],
        out_specs=row,
        out_shape=jax.ShapeDtypeStruct((bsz, seq, D_MODEL), F32),
        compiler_params=pltpu.CompilerParams(dimension_semantics=("parallel", "parallel"),
                                             vmem_limit_bytes=VMEM_LIMIT),
        name="ffn",
    )(x1, mod3, w1, w2, ln)


def _pad_cols(w, left, total):
    return jnp.pad(w, ((0, 0), (left, total - left - w.shape[1])))


def _inproj_weight(w_in):
    o = 0
    w_rw = w_in[:, o:o + RW_COLS]; o += RW_COLS
    w_q = w_in[:, o:o + Q_LORA]; o += Q_LORA
    w_kv = w_in[:, o:o + KV_LORA]; o += KV_LORA
    w_kr = w_in[:, o:o + QK_ROPE]; o += QK_ROPE
    w_g = w_in[:, o:o + 2 * D_MODEL]
    half = QK_ROPE // 2
    w_kr_rot = jnp.concatenate([-w_kr[:, half:], w_kr[:, :half]], axis=1)
    w_all = jnp.concatenate([w_rw, w_kv, _pad_cols(w_kr, QK_NOPE, LANE), _pad_cols(w_kr_rot, QK_NOPE, LANE),
                             w_q, w_g], axis=1)
    return w_all.astype(BF16)


def _mla_weights(w_uq, w_ukv):
    half = QK_ROPE // 2
    q = w_uq.reshape(Q_LORA, MLA_HEADS, QK_NOPE + QK_ROPE)
    q_nope, q_1, q_2 = q[..., :QK_NOPE], q[..., QK_NOPE:QK_NOPE + half], q[..., QK_NOPE + half:]
    zpad = jnp.zeros((Q_LORA, MLA_HEADS, HEAD_PAD - QK_NOPE - QK_ROPE), F32)
    wq1 = jnp.concatenate([q_nope, q_1, q_2, zpad], axis=-1).reshape(Q_LORA, MLA_HEADS * HEAD_PAD)
    kv = w_ukv.reshape(KV_LORA, MLA_HEADS, QK_NOPE + V_HEAD)
    zhalf = jnp.zeros((KV_LORA, MLA_HEADS, HEAD_PAD - QK_NOPE), F32)
    wk = jnp.concatenate([kv[..., :QK_NOPE], zhalf], axis=-1).reshape(KV_LORA, MLA_HEADS * HEAD_PAD)
    wvt = kv[..., QK_NOPE:].reshape(KV_LORA, MLA_HEADS * V_HEAD).T
    return wq1.astype(BF16), wk.astype(BF16), wvt.astype(BF16)


def _scan_constants(rev):
    rows = np.arange(SCAN_ROWS)
    same = (rows[:, None] // CHUNK) == (rows[None, :] // CHUNK)
    tri = same & ((rows[None, :] >= rows[:, None]) if rev else (rows[None, :] <= rows[:, None]))
    ch = np.arange(RW_WIDTH)
    bd = (ch[:, None] // RW_HEAD) == (ch[None, :] // RW_HEAD)
    return jnp.asarray(tri, BF16), jnp.asarray(bd, BF16)


def _lora_pair(w, d):
    zero = jnp.zeros_like(w[0])
    return jnp.concatenate([zero, w[1]] if d else [w[0], zero], axis=0).astype(BF16)


def kernel(x, c, positions, w_ada, b_ada, w_in, rw_conv, rw_w0, rw_w2, rw_a0, rw_a2, rw_k_k, rw_k_a, rw_r_k, rw_g2, rw_lnx_g, rw_lnx_b, mla_q_norm_g, mla_kv_norm_g, mla_w_uq, mla_w_ukv, w_br_rwkv, w_br_mla, w_out, ln1_g, ln1_b, w_ff1, w_ff2, ln2_g, ln2_b):
    bsz, seq, _ = x.shape
    cos_t, sin_t = _rope_tables(positions)
    for l in range(DEPTH):
        mod3 = _ada(c, w_ada[l], b_ada[l]).reshape(bsz, 6, D_MODEL)
        z_rw, z_kv, z_q, z_g = _inproj(x, mod3, _inproj_weight(w_in[l]), rw_conv[l])

        zero = jnp.zeros((RW_WIDTH,), F32)
        y_rw = None
        for d in (0, 1):
            vecs = jnp.stack([rw_w0[l, d], rw_a0[l, d], rw_k_k[l], rw_k_a[l], rw_r_k[l],
                              rw_lnx_g[l], rw_lnx_b[l], zero])
            tri, bd = _scan_constants(bool(d))
            y_rw = _scan(z_rw, vecs, _lora_pair(rw_w2[l], d), _lora_pair(rw_a2[l], d), bd, tri,
                         rev=bool(d), y_fwd=y_rw, g2=rw_g2[l].astype(BF16))

        wq1, wk, wvt = _mla_weights(mla_w_uq[l], mla_w_ukv[l])
        y_mla = _attn(z_kv, z_q, cos_t, sin_t, mla_q_norm_g[l].reshape(1, Q_LORA),
                      mla_kv_norm_g[l].reshape(1, KV_LORA), wq1, wk, wvt)

        x = _mix(y_rw, y_mla, z_g, x, mod3, w_br_rwkv[l].astype(BF16), w_br_mla[l].astype(BF16),
                 w_out[l].astype(BF16), jnp.stack([ln1_g[l], ln1_b[l]]))
        x = _ffn(x, mod3, w_ff1[l].astype(BF16), w_ff2[l].astype(BF16), jnp.stack([ln2_g[l], ln2_b[l]]))
    return x
```

```python
import functools

import numpy as np
import jax
import jax.numpy as jnp
from jax import lax
from jax.experimental import pallas as pl
from jax.experimental.pallas import tpu as pltpu

F32 = jnp.float32
BF16 = jnp.bfloat16

D_MODEL = 1024
RW_HEADS = 8
RW_HEAD = 64
RW_WIDTH = RW_HEADS * RW_HEAD
DECAY_LORA = 64
AAA_LORA = 64
GATE_LORA = 128
RW_COLS = 3 * RW_WIDTH + 2 * DECAY_LORA + 2 * AAA_LORA + GATE_LORA
MLA_HEADS = 8
QK_NOPE = 64
QK_ROPE = 32
V_HEAD = 64
Q_LORA = 384
KV_LORA = 256
ROPE_THETA = 10000.0
D_FF = 4 * D_MODEL
LN_EPS = 1e-5
RMS_EPS = 1e-6
GN_EPS = 64e-5
L2_EPS = 1e-12
DEPTH = 1
DN_ALPHA = (2.0 * DEPTH) ** 0.25
LOG2_E = 1.4426950408889634
DECAY_SCALE = 0.6065306597126334

LANE = 128
CHUNK = 64
SCAN_ROWS = 512
SCAN_WAVE = 4
X_HALO_ROWS = 8
HEAD_PAD = 128
ATTN_KEY_PARTS = 4
VT_ROWS = 80
ZKV_COLS = KV_LORA + 2 * LANE
VMEM_LIMIT = 56 * 1024 * 1024


def _dot(a, b):
    return jnp.dot(a, b, preferred_element_type=F32)


def _dot_nt(a, b):
    return lax.dot_general(a, b, (((1,), (1,)), ((), ())), preferred_element_type=F32)


def _dot_tn(a, b):
    return lax.dot_general(a, b, (((0,), (0,)), ((), ())), preferred_element_type=F32)


def _const_spec(shape):
    zeros = (0,) * len(shape)
    return pl.BlockSpec(shape, lambda *_: zeros)


def _layer_norm(t, g, b):
    mu = jnp.mean(t, axis=-1, keepdims=True)
    d = t - mu
    var = jnp.mean(d * d, axis=-1, keepdims=True)
    return d * lax.rsqrt(var + LN_EPS) * g + b


def _ada_kernel(c_ref, w_ref, b_ref, o_ref):
    c = c_ref[...]
    act = c * jax.nn.sigmoid(c)
    o_ref[...] = _dot(act.astype(BF16), w_ref[...].astype(BF16)) + b_ref[...]


def _ada(c, w, b):
    bsz = c.shape[0]
    n = w.shape[1]
    tn = 1536
    return pl.pallas_call(
        _ada_kernel,
        grid=(n // tn,),
        in_specs=[_const_spec((bsz, D_MODEL)),
                  pl.BlockSpec((D_MODEL, tn), lambda j: (0, j)),
                  pl.BlockSpec((1, tn), lambda j: (0, j))],
        out_specs=pl.BlockSpec((bsz, tn), lambda j: (0, j)),
        out_shape=jax.ShapeDtypeStruct((bsz, n), F32),
        name="ada",
    )(c, w, b.reshape(1, n))


def _rope_kernel(pos_ref, inv_ref, place_ref, one_ref, cos_ref, sin_ref):
    ang = inv_ref[...] * pos_ref[...].astype(F32)

    def table(t):
        hi = t.astype(BF16)
        lo = (t - hi.astype(F32)).astype(BF16)
        return _dot_tn(hi, place_ref[...]) + _dot_tn(lo, place_ref[...])

    cos_ref[...] = table(jnp.cos(ang)) + one_ref[...]
    sin_ref[...] = table(jnp.sin(ang))


def _rope_tables(positions):
    bsz, seq = positions.shape
    half = QK_ROPE // 2
    inv = (ROPE_THETA ** (-np.arange(half, dtype=np.float32) / half)).reshape(half, 1)
    place = np.zeros((half, LANE), np.float32)
    place[np.arange(half), QK_NOPE + np.arange(half)] = 1.0
    place[np.arange(half), QK_NOPE + half + np.arange(half)] = 1.0
    one = np.zeros((1, LANE), np.float32)
    one[0, :QK_NOPE] = 1.0
    spec = pl.BlockSpec((None, seq, LANE), lambda b: (b, 0, 0))
    return pl.pallas_call(
        _rope_kernel,
        grid=(bsz,),
        in_specs=[pl.BlockSpec((None, 1, seq), lambda b: (b, 0, 0)), _const_spec((half, 1)),
                  _const_spec((half, LANE)), _const_spec((1, LANE))],
        out_specs=[spec, spec],
        out_shape=[jax.ShapeDtypeStruct((bsz, seq, LANE), F32)] * 2,
        name="rope",
    )(positions.reshape(bsz, 1, seq), jnp.asarray(inv), jnp.asarray(place, BF16), jnp.asarray(one))


def _inproj_kernel(x_ref, xp_ref, xn_ref, mod_ref, w_ref, cw_ref, zrw_ref, zkv_ref, zq_ref, zg_ref):
    i = pl.program_id(1)
    nb = pl.num_programs(1)
    rows = x_ref.shape[0]
    shift = mod_ref[0:1, :]
    scale = mod_ref[1:2, :]
    h = (x_ref[...] * (1.0 + scale) + shift).astype(BF16)
    col = RW_COLS
    for o_ref in (zkv_ref, zq_ref, zg_ref):
        n = o_ref.shape[-1]
        o_ref[...] = _dot(h, w_ref[:, col:col + n]).astype(o_ref.dtype)
        col += n

    z = _dot(h, w_ref[:, 0:RW_COLS])
    x_halo = jnp.concatenate([xp_ref[...], xn_ref[...]], axis=0)
    z_halo = _dot((x_halo * (1.0 + scale) + shift).astype(BF16), w_ref[:, 0:RW_COLS])
    prev_row = z_halo[X_HALO_ROWS - 1:X_HALO_ROWS, :] * jnp.where(i > 0, 1.0, 0.0)
    next_row = z_halo[X_HALO_ROWS:X_HALO_ROWS + 1, :] * jnp.where(i < nb - 1, 1.0, 0.0)
    row_id = lax.broadcasted_iota(jnp.int32, (rows, 1), 0)
    z_dn = jnp.where(row_id == 0, prev_row, pltpu.roll(z, 1, 0))
    z_up = jnp.where(row_id == rows - 1, next_row, pltpu.roll(z, rows - 1, 0))
    zrw_ref[...] = (cw_ref[0:1, :] * z_dn + cw_ref[1:2, :] * z + cw_ref[2:3, :] * z_up).astype(zrw_ref.dtype)


def _inproj(x, mod3, w_all, conv_w):
    bsz, seq, _ = x.shape
    tm = 512
    hpb = tm // X_HALO_ROWS
    n_halo = seq // X_HALO_ROWS
    widths = (RW_COLS, ZKV_COLS, Q_LORA, 2 * D_MODEL)
    return pl.pallas_call(
        _inproj_kernel,
        grid=(bsz, seq // tm),
        in_specs=[pl.BlockSpec((None, tm, D_MODEL), lambda b, i: (b, i, 0)),
                  pl.BlockSpec((None, X_HALO_ROWS, D_MODEL), lambda b, i: (b, jnp.maximum(i * hpb - 1, 0), 0)),
                  pl.BlockSpec((None, X_HALO_ROWS, D_MODEL),
                               lambda b, i: (b, jnp.minimum((i + 1) * hpb, n_halo - 1), 0)),
                  pl.BlockSpec((None, 6, D_MODEL), lambda b, i: (b, 0, 0)),
                  _const_spec(w_all.shape), _const_spec(conv_w.shape)],
        out_specs=[pl.BlockSpec((None, tm, n), lambda b, i: (b, i, 0)) for n in widths],
        out_shape=[jax.ShapeDtypeStruct((bsz, seq, n), BF16) for n in widths],
        compiler_params=pltpu.CompilerParams(dimension_semantics=("parallel", "parallel"),
                                             vmem_limit_bytes=VMEM_LIMIT),
        name="inproj",
    )(x, x, x, mod3, w_all, conv_w)


def _sigmoid(x):
    return 0.5 + 0.5 * jnp.tanh(0.5 * x)


def _scan_kernel(*refs, rev):
    if rev:
        (zm_ref, vec_ref, w2_ref, a2_ref, bd_ref, tri_ref,
         yf_ref, g2_ref, o_ref,
         at_s, rt_s, bh_s, kh_s, be_s, ke_s, v_s, vsw_s, eg_s, h_s, ya_s, yb_s) = refs
    else:
        (zm_ref, vec_ref, w2_ref, a2_ref, bd_ref, tri_ref,
         o_ref,
         at_s, rt_s, bh_s, kh_s, be_s, ke_s, v_s, vsw_s, eg_s, h_s, ya_s, yb_s) = refs
    rows = zm_ref.shape[0]
    n_chunks = rows // CHUNK
    i = pl.program_id(1)

    @pl.when(i == 0)
    def _():
        h_s[...] = jnp.zeros_like(h_s)

    def token_prep(wave):
        cs = sorted(wave)
        n_r = len(cs) * CHUNK
        rs = slice(cs[0] * CHUNK, cs[0] * CHUNK + n_r)
        zc = zm_ref[rs, :].astype(F32)
        r = zc[:, 0:RW_WIDTH]
        k = zc[:, RW_WIDTH:2 * RW_WIDTH]
        v = zc[:, 2 * RW_WIDTH:3 * RW_WIDTH]
        zw = zc[:, 3 * RW_WIDTH:3 * RW_WIDTH + LANE]
        za = zc[:, 3 * RW_WIDTH + LANE:3 * RW_WIDTH + 2 * LANE]
        w0 = vec_ref[0:1, :]
        a0 = vec_ref[1:2, :]
        k_k = vec_ref[2:3, :]
        k_a = vec_ref[3:4, :]
        v_s[rs, :] = v
        vsw_s[rs, :] = pltpu.roll(v, RW_HEAD, 1)
        w_lin = w0 + _dot(jnp.tanh(zw).astype(BF16), w2_ref[...])
        rate_lin = a0 + _dot(za.astype(BF16), a2_ref[...])
        kkv = k * k_k
        ssq = _dot((kkv * kkv).astype(BF16), bd_ref[...])
        yield
        lw = -DECAY_SCALE * _sigmoid(w_lin)
        lw_hi = lw.astype(BF16)
        lw_lo = (lw - lw_hi.astype(F32)).astype(BF16)
        tri = tri_ref[0:n_r, 0:n_r]
        g_in = _dot(tri, lw_hi) + _dot(tri, lw_lo)
        yield
        rate = _sigmoid(rate_lin)
        kk = kkv * lax.rsqrt(jnp.maximum(ssq, L2_EPS * L2_EPS))
        kd = k * (1.0 + (rate - 1.0) * k_a)
        av = -kk
        bv = kk * rate
        yield
        g_ex = g_in - lw
        last = 0 if rev else CHUNK - 1
        g_tot_rows = [g_in[j * CHUNK + last:j * CHUNK + last + 1, :] for j in range(len(cs))]
        eg_rows = [jnp.exp(t) for t in g_tot_rows]
        for j, c in enumerate(cs):
            eg_s[c] = jnp.broadcast_to(eg_rows[j], (8, RW_WIDTH))
        e_neg = jnp.exp(-g_in)
        e_end = e_neg * jnp.concatenate([jnp.broadcast_to(t, (CHUNK, RW_WIDTH)) for t in eg_rows], axis=0)
        at_s[rs, :] = av * jnp.exp(g_ex)
        rt_s[rs, :] = r * jnp.exp(g_in)
        yield
        bh_s[rs, :] = bv * e_neg
        kh_s[rs, :] = kd * e_neg
        be_s[rs, :] = bv * e_end
        ke_s[rs, :] = kd * e_end

    ri = lax.broadcasted_iota(jnp.int32, (CHUNK, CHUNK), 0)
    ci = lax.broadcasted_iota(jnp.int32, (CHUNK, CHUNK), 1)
    m_strict = (ci > ri) if rev else (ci < ri)
    m_incl = (ci >= ri) if rev else (ci <= ri)
    eye = jnp.where(ri == ci, 1.0, 0.0).astype(F32)
    lower = lax.broadcasted_iota(jnp.int32, (CHUNK, LANE), 1) < RW_HEAD
    eye_up = jnp.where(lax.broadcasted_iota(jnp.int32, (CHUNK, LANE), 1)
                       == lax.broadcasted_iota(jnp.int32, (CHUNK, LANE), 0) + RW_HEAD, 1.0, 0.0).astype(F32)
    zeros_half = jnp.zeros((CHUNK, RW_HEAD), F32)

    def pad(t):
        return jnp.concatenate([t, zeros_half], axis=1)

    heads = range(RW_HEADS)
    chunk_order = list(range(n_chunks - 1, -1, -1)) if rev else list(range(n_chunks))
    waves = [chunk_order[w:w + SCAN_WAVE] for w in range(0, n_chunks, SCAN_WAVE)]

    def ld(ref, item):
        c, h = item
        return ref[c * CHUNK:(c + 1) * CHUNK, h * RW_HEAD:(h + 1) * RW_HEAD]

    def ld_v_upper(item):
        c, h = item
        src = v_s if h % 2 else vsw_s
        blk = src[c * CHUNK:(c + 1) * CHUNK, (h // 2) * LANE:(h // 2 + 1) * LANE]
        return jnp.where(lower, 0.0, blk).astype(BF16)

    def ld_decay(item):
        c, h = item
        return eg_s[c][0:1, h * RW_HEAD:(h + 1) * RW_HEAD]

    def chunk_local(items, out):
        n_it = range(len(items))
        rt = [ld(rt_s, it) for it in items]
        atb = [ld(at_s, it).astype(BF16) for it in items]
        lhs = [jnp.concatenate([atb[i], rt[i].astype(BF16)], axis=0) for i in n_it]
        a_b = [_dot_nt(lhs[i], ld(bh_s, items[i]).astype(BF16)) for i in n_it]
        a_k = [_dot_nt(lhs[i], ld(kh_s, items[i]).astype(BF16)) for i in n_it]
        yield
        a_ab =[jnp.where(m_strict, a_b[i][0:CHUNK], 0.0) for i in n_it]
        a_ak = [jnp.where(m_strict, a_k[i][0:CHUNK], 0.0).astype(BF16) for i in n_it]
        a_rb = [jnp.where(m_incl, a_b[i][CHUNK:], 0.0).astype(BF16) for i in n_it]
        a_rk = [jnp.where(m_incl, a_k[i][CHUNK:], 0.0).astype(BF16) for i in n_it]
        v_up = [ld_v_upper(it) for it in items]
        akv = [_dot(a_ak[i], v_up[i]) for i in n_it]
        yield

        z = [pad(a_ab[i]) + eye_up for i in n_it]
        for _ in range(6):
            z = [_dot(z[i][:, 0:CHUNK].astype(BF16), z[i].astype(BF16)) + jnp.where(lower, 0.0, z[i])
                 for i in n_it]
            yield
        tb =[z[i][:, CHUNK:].astype(BF16) for i in n_it]

        w_m = [(pad(ld(at_s, items[i])) + akv[i]).astype(BF16) for i in n_it]
        pq = [_dot(tb[i], w_m[i]).astype(BF16) for i in n_it]
        yield
        ryq =[_dot(a_rb[i], pq[i]) for i in n_it]
        ry = [(rt[i] + ryq[i][:, 0:RW_HEAD]).astype(BF16) for i in n_it]
        y0 = [jnp.where(lower, 0.0, ryq[i]) + _dot(a_rk[i], v_up[i]) for i in n_it]
        yield
        be =[ld(be_s, it).astype(BF16) for it in items]
        ke = [ld(ke_s, it).astype(BF16) for it in items]
        mq = [_dot_tn(be[i], pq[i]) for i in n_it]
        m_m = [(mq[i][:, 0:RW_HEAD] + eye * ld_decay(items[i])).astype(BF16) for i in n_it]
        h0 = [jnp.where(lower, 0.0, mq[i]) + _dot_tn(ke[i], v_up[i]) for i in n_it]
        ry_m = [jnp.concatenate([ry[i], m_m[i]], axis=0) for i in n_it]
        out.update(ry_m=ry_m, y0=y0, h0=h0)

    hst = [h_s[h] for h in heads]

    def carried(wave, res):
        for ci, c in enumerate(wave):
            for h in heads:
                i = ci * RW_HEADS + h
                yh = _dot(res["ry_m"][i], hst[h].astype(BF16))
                y_dst = ya_s if h % 2 else yb_s
                y_dst[c * CHUNK:(c + 1) * CHUNK, (h // 2) * LANE:(h // 2 + 1) * LANE] = yh[0:CHUNK] + res["y0"][i]
                hst[h] = yh[CHUNK:] + res["h0"][i]
                if h % 4 == 3:
                    yield

    def emit(*gens):
        live = list(gens)
        while live:
            for g in list(live):
                if next(g, StopIteration) is StopIteration:
                    live.remove(g)

    results = [dict() for _ in waves]
    emit(token_prep(waves[0]))
    for w, wave in enumerate(waves):
        gens = [chunk_local([(c, h) for c in wave for h in heads], results[w])]
        if w + 1 < len(waves):
            gens.append(token_prep(waves[w + 1]))
        if w > 0:
            gens.append(carried(waves[w - 1], results[w - 1]))
        emit(*gens)
    emit(carried(waves[-1], results[-1]))
    for h in heads:
        h_s[h] = hst[h]

    y_dir = ya_s[...] + pltpu.roll(yb_s[...], RW_WIDTH - RW_HEAD, 1)
    if not rev:
        o_ref[...] = y_dir
    else:
        r_k = vec_ref[4:5, :]
        lnx_g = vec_ref[5:6, :]
        lnx_b = vec_ref[6:7, :]
        zc = zm_ref[...].astype(F32)
        r = zc[:, 0:RW_WIDTH]
        k = zc[:, RW_WIDTH:2 * RW_WIDTH]
        v = zc[:, 2 * RW_WIDTH:3 * RW_WIDTH]
        y = yf_ref[...] + y_dir
        inv_n = 1.0 / RW_HEAD
        mu = _dot(y.astype(BF16), bd_ref[...]) * inv_n
        d = y - mu
        var = _dot((d * d).astype(BF16), bd_ref[...]) * inv_n
        yn = d * lax.rsqrt(var + GN_EPS) * lnx_g + lnx_b
        rk = r * k * r_k
        rk_hi = rk.astype(BF16)
        rk_lo = (rk - rk_hi.astype(F32)).astype(BF16)
        bonus = (_dot(rk_hi, bd_ref[...]) + _dot(rk_lo, bd_ref[...])) * v
        zg = zc[:, 3 * RW_WIDTH + 2 * LANE:RW_COLS]
        gate = _dot(_sigmoid(zg).astype(BF16), g2_ref[...])
        o_ref[...] = ((yn + bonus) * gate).astype(o_ref.dtype)


def _scan(z_rw, vecs, w2p, a2p, bd, tri, rev, y_fwd=None, g2=None):
    bsz, seq, _ = z_rw.shape
    rows = SCAN_ROWS
    nb = seq // rows
    n_chunks = rows // CHUNK

    def blk(i):
        return nb - 1 - i if rev else i

    in_specs = [
        pl.BlockSpec((None, rows, RW_COLS), lambda b, i: (b, blk(i), 0)),
        _const_spec(vecs.shape), _const_spec(w2p.shape),
        _const_spec(a2p.shape), _const_spec(bd.shape), _const_spec(tri.shape),
    ]
    args = [z_rw, vecs, w2p, a2p, bd, tri]
    scratch = [pltpu.VMEM((rows, RW_WIDTH), F32) for _ in range(8)]
    scratch += [pltpu.VMEM((n_chunks, 8, RW_WIDTH), F32),
                pltpu.VMEM((RW_HEADS, RW_HEAD, LANE), F32),
                pltpu.VMEM((rows, RW_WIDTH), F32), pltpu.VMEM((rows, RW_WIDTH), F32)]
    if rev:
        in_specs += [pl.BlockSpec((None, rows, RW_WIDTH), lambda b, i: (b, blk(i), 0)),
                     _const_spec(g2.shape)]
        args += [y_fwd, g2]
        out_dtype = BF16
    else:
        out_dtype = F32
    return pl.pallas_call(
        functools.partial(_scan_kernel, rev=rev),
        grid=(bsz, nb),
        in_specs=in_specs,
        out_specs=pl.BlockSpec((None, rows, RW_WIDTH), lambda b, i: (b, blk(i), 0)),
        out_shape=jax.ShapeDtypeStruct((bsz, seq, RW_WIDTH), out_dtype),
        scratch_shapes=scratch,
        compiler_params=pltpu.CompilerParams(dimension_semantics=("parallel", "arbitrary"),
                                             vmem_limit_bytes=VMEM_LIMIT),
        name="scan_bwd" if rev else "scan_fwd",
    )(*args)


def _rms(xf, g):
    return xf * lax.rsqrt(jnp.mean(xf * xf, axis=-1, keepdims=True) + RMS_EPS) * g


def _attn_kernel(zkv_ref, zq_ref, cos_ref, sin_ref, gq_ref, gkv_ref, wq1_ref, wk_ref, wvt_ref,
                 o_ref, k_s, vt_s):
    seq = zkv_ref.shape[0]
    tq = zq_ref.shape[0]
    kt = 256
    i = pl.program_id(1)

    @pl.when(i == 0)
    def _():
        def kv_body(t, carry):
            r0 = pl.multiple_of(t * kt, kt)
            zk = zkv_ref[pl.ds(r0, kt), :].astype(F32)
            kvn = _rms(zk[:, 0:KV_LORA], gkv_ref[...]).astype(BF16)
            kn = _dot(kvn, wk_ref[...])
            cs = cos_ref[pl.ds(r0, kt), :]
            sn = sin_ref[pl.ds(r0, kt), :]
            kpe = zk[:, KV_LORA:KV_LORA + LANE] * cs + zk[:, KV_LORA + LANE:KV_LORA + 2 * LANE] * sn
            for h in range(MLA_HEADS):
                sl = slice(h * HEAD_PAD, (h + 1) * HEAD_PAD)
                k_s[pl.ds(r0, kt), sl] = (kn[:, sl] + kpe).astype(BF16)
            vt = _dot_nt(wvt_ref[...], kvn).astype(BF16)
            ones = jnp.ones((VT_ROWS - V_HEAD, kt), BF16)
            for h in range(MLA_HEADS):
                vt_s[h * VT_ROWS:(h + 1) * VT_ROWS, pl.ds(r0, kt)] = jnp.concatenate(
                    [vt[h * V_HEAD:(h + 1) * V_HEAD], ones], axis=0)
            return carry
        lax.fori_loop(0, seq // kt, kv_body, 0)

    q0 = pl.multiple_of(i * tq, tq)
    qn = _rms(zq_ref[...].astype(F32), gq_ref[...]).astype(BF16)
    q1 = _dot(qn, wq1_ref[...])
    half = QK_ROPE // 2
    lane = lax.broadcasted_iota(jnp.int32, q1.shape, 1) % HEAD_PAD
    q2 = jnp.where((lane >= QK_NOPE) & (lane < QK_NOPE + half), -pltpu.roll(q1, q1.shape[1] - half, 1),
                   jnp.where((lane >= QK_NOPE + half) & (lane < QK_NOPE + QK_ROPE), pltpu.roll(q1, half, 1), 0.0))
    cs = cos_ref[pl.ds(q0, tq), :]
    sn = sin_ref[pl.ds(q0, tq), :]
    scale = (QK_NOPE + QK_ROPE) ** -0.5 * LOG2_E

    kparts = [slice(j * (seq // ATTN_KEY_PARTS), (j + 1) * (seq // ATTN_KEY_PARTS)) for j in range(ATTN_KEY_PARTS)]

    def scores(h):
        sl = slice(h * HEAD_PAD, (h + 1) * HEAD_PAD)
        qh = ((q1[:, sl] * cs + q2[:, sl] * sn) * scale).astype(BF16)
        return [_dot_nt(k_s[kp, sl], qh) for kp in kparts]

    outs = []
    st_next = scores(0)
    for h in range(MLA_HEADS):
        st = st_next
        if h + 1 < MLA_HEADS:
            st_next = scores(h + 1)
        m = functools.reduce(jnp.maximum, [jnp.max(t, axis=0, keepdims=True) for t in st])
        ps = [jnp.exp2(t - m).astype(BF16) for t in st]
        ol = sum(_dot(vt_s[h * VT_ROWS:(h + 1) * VT_ROWS, kp], p) for kp, p in zip(kparts, ps))
        outs.append(ol[0:V_HEAD] / ol[V_HEAD:V_HEAD + 1])
    o_ref[...] = jnp.concatenate(outs, axis=0).T.astype(o_ref.dtype)


def _attn(z_kv, z_q, cos_t, sin_t, gq, gkv, wq1, wk, wvt):
    bsz, seq, _ = z_kv.shape
    tq = 256
    kw = MLA_HEADS * HEAD_PAD
    return pl.pallas_call(
        _attn_kernel,
        grid=(bsz, seq // tq),
        in_specs=[pl.BlockSpec((None, seq, ZKV_COLS), lambda b, i: (b, 0, 0)),
                  pl.BlockSpec((None, tq, Q_LORA), lambda b, i: (b, i, 0)),
                  pl.BlockSpec((None, seq, LANE), lambda b, i: (b, 0, 0)),
                  pl.BlockSpec((None, seq, LANE), lambda b, i: (b, 0, 0)),
                  _const_spec(gq.shape), _const_spec(gkv.shape), _const_spec(wq1.shape),
                  _const_spec(wk.shape), _const_spec(wvt.shape)],
        out_specs=pl.BlockSpec((None, tq, MLA_HEADS * V_HEAD), lambda b, i: (b, i, 0)),
        out_shape=jax.ShapeDtypeStruct((bsz, seq, MLA_HEADS * V_HEAD), BF16),
        scratch_shapes=[pltpu.VMEM((seq, kw), BF16), pltpu.VMEM((MLA_HEADS * VT_ROWS, seq), BF16)],
        compiler_params=pltpu.CompilerParams(dimension_semantics=("parallel", "arbitrary"),
                                             vmem_limit_bytes=VMEM_LIMIT),
        name="attn",
    )(z_kv, z_q, cos_t, sin_t, gq, gkv, wq1, wk, wvt)


def _mix_kernel(yrw_ref, ymla_ref, ga_ref, gb_ref, x_ref, mod_ref, wr_ref, wm_ref, wo_ref, ln_ref, o_ref):
    gate1 = mod_ref[2:3, :]
    br_rw = _dot(yrw_ref[...], wr_ref[...])
    br_mla = _dot(ymla_ref[...], wm_ref[...])
    mixed = (_sigmoid(ga_ref[...].astype(F32)) * br_rw
             + _sigmoid(gb_ref[...].astype(F32)) * br_mla)
    out = _dot(mixed.astype(BF16), wo_ref[...])
    t = DN_ALPHA * x_ref[...] + (1.0 + gate1) * out
    o_ref[...] = _layer_norm(t, ln_ref[0:1, :], ln_ref[1:2, :])


def _mix(y_rw, y_mla, z_g, x, mod3, wr, wm, wo, ln):
    bsz, seq, _ = x.shape
    tm = 512
    row = lambda n: pl.BlockSpec((None, tm, n), lambda b, i: (b, i, 0))
    return pl.pallas_call(
        _mix_kernel,
        grid=(bsz, seq // tm),
        in_specs=[row(RW_WIDTH), row(MLA_HEADS * V_HEAD),
                  pl.BlockSpec((None, tm, D_MODEL), lambda b, i: (b, i, 0)),
                  pl.BlockSpec((None, tm, D_MODEL), lambda b, i: (b, i, 1)),
                  row(D_MODEL),
                  pl.BlockSpec((None, 6, D_MODEL), lambda b, i: (b, 0, 0)),
                  _const_spec(wr.shape), _const_spec(wm.shape), _const_spec(wo.shape),
                  _const_spec(ln.shape)],
        out_specs=row(D_MODEL),
        out_shape=jax.ShapeDtypeStruct((bsz, seq, D_MODEL), F32),
        compiler_params=pltpu.CompilerParams(dimension_semantics=("parallel", "parallel"),
                                             vmem_limit_bytes=VMEM_LIMIT),
        name="mix",
    )(y_rw, y_mla, z_g, z_g, x, mod3, wr, wm, wo, ln)


def _ffn_kernel(x_ref, mod_ref, w1_ref, w2_ref, ln_ref, o_ref):
    shift = mod_ref[3:4, :]
    scale = mod_ref[4:5, :]
    gate2 = mod_ref[5:6, :]
    x1 = x_ref[...]
    h = (x1 * (1.0 + scale) + shift).astype(BF16)
    kc = 1024
    acc = jnp.zeros(x1.shape, F32)
    for c in range(D_FF // kc):
        u = jnp.maximum(_dot(h, w1_ref[:, c * kc:(c + 1) * kc]), 0.0)
        acc = acc + _dot((u * u).astype(BF16), w2_ref[c * kc:(c + 1) * kc, :])
    t = DN_ALPHA * x1 + (1.0 + gate2) * acc
    o_ref[...] = _layer_norm(t, ln_ref[0:1, :], ln_ref[1:2, :])


def _ffn(x1, mod3, w1, w2, ln):
    bsz, seq, _ = x1.shape
    tm = 512
    row = pl.BlockSpec((None, tm, D_MODEL), lambda b, i: (b, i, 0))
    return pl.pallas_call(
        _ffn_kernel,
        grid=(bsz, seq // tm),
        in_specs=[row, pl.BlockSpec((None, 6, D_MODEL), lambda b, i: (b, 0, 0)),
                  _const_spec(w1.shape), _const_spec(w2.shape), _const_spec(ln.shape)],
        out_specs=row,
        out_shape=jax.ShapeDtypeStruct((bsz, seq, D_MODEL), F32),
        compiler_params=pltpu.CompilerParams(dimension_semantics=("parallel", "parallel"),
                                             vmem_limit_bytes=VMEM_LIMIT),
        name="ffn",
    )(x1, mod3, w1, w2, ln)


def _pad_cols(w, left, total):
    return jnp.pad(w, ((0, 0), (left, total - left - w.shape[1])))


def _inproj_weight(w_in):
    o = 0
    w_rw = w_in[:, o:o + RW_COLS]; o += RW_COLS
    w_q = w_in[:, o:o + Q_LORA]; o += Q_LORA
    w_kv = w_in[:, o:o + KV_LORA]; o += KV_LORA
    w_kr = w_in[:, o:o + QK_ROPE]; o += QK_ROPE
    w_g = w_in[:, o:o + 2 * D_MODEL]
    half = QK_ROPE // 2
    w_kr_rot = jnp.concatenate([-w_kr[:, half:], w_kr[:, :half]], axis=1)
    w_all = jnp.concatenate([w_rw, w_kv, _pad_cols(w_kr, QK_NOPE, LANE), _pad_cols(w_kr_rot, QK_NOPE, LANE),
                             w_q, w_g], axis=1)
    return w_all.astype(BF16)


def _mla_weights(w_uq, w_ukv):
    half = QK_ROPE // 2
    q = w_uq.reshape(Q_LORA, MLA_HEADS, QK_NOPE + QK_ROPE)
    q_nope, q_1, q_2 = q[..., :QK_NOPE], q[..., QK_NOPE:QK_NOPE + half], q[..., QK_NOPE + half:]
    zpad = jnp.zeros((Q_LORA, MLA_HEADS, HEAD_PAD - QK_NOPE - QK_ROPE), F32)
    wq1 = jnp.concatenate([q_nope, q_1, q_2, zpad], axis=-1).reshape(Q_LORA, MLA_HEADS * HEAD_PAD)
    kv = w_ukv.reshape(KV_LORA, MLA_HEADS, QK_NOPE + V_HEAD)
    zhalf = jnp.zeros((KV_LORA, MLA_HEADS, HEAD_PAD - QK_NOPE), F32)
    wk = jnp.concatenate([kv[..., :QK_NOPE], zhalf], axis=-1).reshape(KV_LORA, MLA_HEADS * HEAD_PAD)
    wvt = kv[..., QK_NOPE:].reshape(KV_LORA, MLA_HEADS * V_HEAD).T
    return wq1.astype(BF16), wk.astype(BF16), wvt.astype(BF16)


def _scan_constants(rev):
    rows = np.arange(SCAN_ROWS)
    same = (rows[:, None] // CHUNK) == (rows[None, :] // CHUNK)
    tri = same & ((rows[None, :] >= rows[:, None]) if rev else (rows[None, :] <= rows[:, None]))
    ch = np.arange(RW_WIDTH)
    bd = (ch[:, None] // RW_HEAD) == (ch[None, :] // RW_HEAD)
    return jnp.asarray(tri, BF16), jnp.asarray(bd, BF16)


def _lora_pair(w, d):
    zero = jnp.zeros_like(w[0])
    return jnp.concatenate([zero, w[1]] if d else [w[0], zero], axis=0).astype(BF16)


def kernel(x, c, positions, w_ada, b_ada, w_in, rw_conv, rw_w0, rw_w2, rw_a0, rw_a2, rw_k_k, rw_k_a, rw_r_k, rw_g2, rw_lnx_g, rw_lnx_b, mla_q_norm_g, mla_kv_norm_g, mla_w_uq, mla_w_ukv, w_br_rwkv, w_br_mla, w_out, ln1_g, ln1_b, w_ff1, w_ff2, ln2_g, ln2_b):
    bsz, seq, _ = x.shape
    cos_t, sin_t = _rope_tables(positions)
    for l in range(DEPTH):
        mod3 = _ada(c, w_ada[l], b_ada[l]).reshape(bsz, 6, D_MODEL)
        z_rw, z_kv, z_q, z_g = _inproj(x, mod3, _inproj_weight(w_in[l]), rw_conv[l])

        zero = jnp.zeros((RW_WIDTH,), F32)
        y_rw = None
        for d in (0, 1):
            vecs = jnp.stack([rw_w0[l, d], rw_a0[l, d], rw_k_k[l], rw_k_a[l], rw_r_k[l],
                              rw_lnx_g[l], rw_lnx_b[l], zero])
            tri, bd = _scan_constants(bool(d))
            y_rw = _scan(z_rw, vecs, _lora_pair(rw_w2[l], d), _lora_pair(rw_a2[l], d), bd, tri,
                         rev=bool(d), y_fwd=y_rw, g2=rw_g2[l].astype(BF16))

        wq1, wk, wvt = _mla_weights(mla_w_uq[l], mla_w_ukv[l])
        y_mla = _attn(z_kv, z_q, cos_t, sin_t, mla_q_norm_g[l].reshape(1, Q_LORA),
                      mla_kv_norm_g[l].reshape(1, KV_LORA), wq1, wk, wvt)

        x = _mix(y_rw, y_mla, z_g, x, mod3, w_br_rwkv[l].astype(BF16), w_br_mla[l].astype(BF16),
                 w_out[l].astype(BF16), jnp.stack([ln1_g[l], ln1_b[l]]))
        x = _ffn(x, mod3, w_ff1[l].astype(BF16), w_ff2[l].astype(BF16), jnp.stack([ln2_g[l], ln2_b[l]]))
    return x
```

```python
import functools

import numpy as np
import jax
import jax.numpy as jnp
from jax import lax
from jax.experimental import pallas as pl
from jax.experimental.pallas import tpu as pltpu

F32 = jnp.float32
BF16 = jnp.bfloat16

D_MODEL = 1024
RW_HEADS = 8
RW_HEAD = 64
RW_WIDTH = RW_HEADS * RW_HEAD
DECAY_LORA = 64
AAA_LORA = 64
GATE_LORA = 128
RW_COLS = 3 * RW_WIDTH + 2 * DECAY_LORA + 2 * AAA_LORA + GATE_LORA
MLA_HEADS = 8
QK_NOPE = 64
QK_ROPE = 32
V_HEAD = 64
Q_LORA = 384
KV_LORA = 256
ROPE_THETA = 10000.0
D_FF = 4 * D_MODEL
LN_EPS = 1e-5
RMS_EPS = 1e-6
GN_EPS = 64e-5
L2_EPS = 1e-12
DEPTH = 1
DN_ALPHA = (2.0 * DEPTH) ** 0.25
LOG2_E = 1.4426950408889634
DECAY_SCALE = 0.6065306597126334

LANE = 128
CHUNK = 64
SCAN_ROWS = 512
SCAN_WAVE = 4
X_HALO_ROWS = 8
HEAD_PAD = 128
ATTN_KEY_PARTS = 4
VT_ROWS = 80
ZKV_COLS = KV_LORA + 2 * LANE
VMEM_LIMIT = 56 * 1024 * 1024


def _dot(a, b):
    return jnp.dot(a, b, preferred_element_type=F32)


def _dot_nt(a, b):
    return lax.dot_general(a, b, (((1,), (1,)), ((), ())), preferred_element_type=F32)


def _dot_tn(a, b):
    return lax.dot_general(a, b, (((0,), (0,)), ((), ())), preferred_element_type=F32)


def _const_spec(shape):
    zeros = (0,) * len(shape)
    return pl.BlockSpec(shape, lambda *_: zeros)


def _layer_norm(t, g, b):
    mu = jnp.mean(t, axis=-1, keepdims=True)
    d = t - mu
    var = jnp.mean(d * d, axis=-1, keepdims=True)
    return d * lax.rsqrt(var + LN_EPS) * g + b


def _ada_kernel(c_ref, w_ref, b_ref, o_ref):
    c = c_ref[...]
    act = c * jax.nn.sigmoid(c)
    o_ref[...] = _dot(act.astype(BF16), w_ref[...].astype(BF16)) + b_ref[...]


def _ada(c, w, b):
    bsz = c.shape[0]
    n = w.shape[1]
    tn = 1536
    return pl.pallas_call(
        _ada_kernel,
        grid=(n // tn,),
        in_specs=[_const_spec((bsz, D_MODEL)),
                  pl.BlockSpec((D_MODEL, tn), lambda j: (0, j)),
                  pl.BlockSpec((1, tn), lambda j: (0, j))],
        out_specs=pl.BlockSpec((bsz, tn), lambda j: (0, j)),
        out_shape=jax.ShapeDtypeStruct((bsz, n), F32),
        name="ada",
    )(c, w, b.reshape(1, n))


def _rope_kernel(pos_ref, inv_ref, place_ref, one_ref, cos_ref, sin_ref):
    ang = inv_ref[...] * pos_ref[...].astype(F32)

    def table(t):
        hi = t.astype(BF16)
        lo = (t - hi.astype(F32)).astype(BF16)
        return _dot_tn(hi, place_ref[...]) + _dot_tn(lo, place_ref[...])

    cos_ref[...] = table(jnp.cos(ang)) + one_ref[...]
    sin_ref[...] = table(jnp.sin(ang))


def _rope_tables(positions):
    bsz, seq = positions.shape
    half = QK_ROPE // 2
    inv = (ROPE_THETA ** (-np.arange(half, dtype=np.float32) / half)).reshape(half, 1)
    place = np.zeros((half, LANE), np.float32)
    place[np.arange(half), QK_NOPE + np.arange(half)] = 1.0
    place[np.arange(half), QK_NOPE + half + np.arange(half)] = 1.0
    one = np.zeros((1, LANE), np.float32)
    one[0, :QK_NOPE] = 1.0
    spec = pl.BlockSpec((None, seq, LANE), lambda b: (b, 0, 0))
    return pl.pallas_call(
        _rope_kernel,
        grid=(bsz,),
        in_specs=[pl.BlockSpec((None, 1, seq), lambda b: (b, 0, 0)), _const_spec((half, 1)),
                  _const_spec((half, LANE)), _const_spec((1, LANE))],
        out_specs=[spec, spec],
        out_shape=[jax.ShapeDtypeStruct((bsz, seq, LANE), F32)] * 2,
        name="rope",
    )(positions.reshape(bsz, 1, seq), jnp.asarray(inv), jnp.asarray(place, BF16), jnp.asarray(one))


def _inproj_kernel(x_ref, xp_ref, xn_ref, mod_ref, w_ref, cw_ref, zrw_ref, zkv_ref, zq_ref, zg_ref):
    i = pl.program_id(1)
    nb = pl.num_programs(1)
    rows = x_ref.shape[0]
    shift = mod_ref[0:1, :]
    scale = mod_ref[1:2, :]
    h = (x_ref[...] * (1.0 + scale) + shift).astype(BF16)
    col = RW_COLS
    for o_ref in (zkv_ref, zq_ref, zg_ref):
        n = o_ref.shape[-1]
        o_ref[...] = _dot(h, w_ref[:, col:col + n]).astype(o_ref.dtype)
        col += n

    z = _dot(h, w_ref[:, 0:RW_COLS])
    x_halo = jnp.concatenate([xp_ref[...], xn_ref[...]], axis=0)
    z_halo = _dot((x_halo * (1.0 + scale) + shift).astype(BF16), w_ref[:, 0:RW_COLS])
    prev_row = z_halo[X_HALO_ROWS - 1:X_HALO_ROWS, :] * jnp.where(i > 0, 1.0, 0.0)
    next_row = z_halo[X_HALO_ROWS:X_HALO_ROWS + 1, :] * jnp.where(i < nb - 1, 1.0, 0.0)
    row_id = lax.broadcasted_iota(jnp.int32, (rows, 1), 0)
    z_dn = jnp.where(row_id == 0, prev_row, pltpu.roll(z, 1, 0))
    z_up = jnp.where(row_id == rows - 1, next_row, pltpu.roll(z, rows - 1, 0))
    zrw_ref[...] = (cw_ref[0:1, :] * z_dn + cw_ref[1:2, :] * z + cw_ref[2:3, :] * z_up).astype(zrw_ref.dtype)


def _inproj(x, mod3, w_all, conv_w):
    bsz, seq, _ = x.shape
    tm = 512
    hpb = tm // X_HALO_ROWS
    n_halo = seq // X_HALO_ROWS
    widths = (RW_COLS, ZKV_COLS, Q_LORA, 2 * D_MODEL)
    return pl.pallas_call(
        _inproj_kernel,
        grid=(bsz, seq // tm),
        in_specs=[pl.BlockSpec((None, tm, D_MODEL), lambda b, i: (b, i, 0)),
                  pl.BlockSpec((None, X_HALO_ROWS, D_MODEL), lambda b, i: (b, jnp.maximum(i * hpb - 1, 0), 0)),
                  pl.BlockSpec((None, X_HALO_ROWS, D_MODEL),
                               lambda b, i: (b, jnp.minimum((i + 1) * hpb, n_halo - 1), 0)),
                  pl.BlockSpec((None, 6, D_MODEL), lambda b, i: (b, 0, 0)),
                  _const_spec(w_all.shape), _const_spec(conv_w.shape)],
        out_specs=[pl.BlockSpec((None, tm, n), lambda b, i: (b, i, 0)) for n in widths],
        out_shape=[jax.ShapeDtypeStruct((bsz, seq, n), BF16) for n in widths],
        compiler_params=pltpu.CompilerParams(dimension_semantics=("parallel", "parallel"),
                                             vmem_limit_bytes=VMEM_LIMIT),
        name="inproj",
    )(x, x, x, mod3, w_all, conv_w)


def _sigmoid(x):
    return 0.5 + 0.5 * jnp.tanh(0.5 * x)


def _scan_kernel(*refs, rev, blocks_per_seq):
    if rev:
        (zm_ref, zf_ref, vec_ref, w2_ref, a2_ref, bd_ref, tri_ref,
         yf_ref, g2_ref, o_ref,
         at_s, rt_s, bh_s, kh_s, be_s, ke_s, v_s, vsw_s, eg_s, h_s, ya_s, yb_s, zr_s, zy_s, zh_s) = refs
    else:
        (zm_ref, vec_ref, w2_ref, a2_ref, bd_ref, tri_ref,
         o_ref,
         at_s, rt_s, bh_s, kh_s, be_s, ke_s, v_s, vsw_s, eg_s, h_s, ya_s, yb_s, zr_s, zy_s, zh_s) = refs
    rows = zm_ref.shape[0]
    n_chunks = rows // CHUNK
    t = pl.program_id(0)

    @pl.when(t == 0)
    def _():
        h_s[...] = jnp.zeros_like(h_s)
        zr_s[...] = jnp.zeros_like(zr_s)
        zy_s[...] = jnp.zeros_like(zy_s)
        zh_s[...] = jnp.zeros_like(zh_s)

    def token_prep(wave):
        cs = sorted(wave)
        n_r = len(cs) * CHUNK
        rs = slice(cs[0] * CHUNK, cs[0] * CHUNK + n_r)
        zc = zm_ref[rs, :].astype(F32)
        r = zc[:, 0:RW_WIDTH]
        k = zc[:, RW_WIDTH:2 * RW_WIDTH]
        v = zc[:, 2 * RW_WIDTH:3 * RW_WIDTH]
        zw = zc[:, 3 * RW_WIDTH:3 * RW_WIDTH + LANE]
        za = zc[:, 3 * RW_WIDTH + LANE:3 * RW_WIDTH + 2 * LANE]
        w0 = vec_ref[0:1, :]
        a0 = vec_ref[1:2, :]
        k_k = vec_ref[2:3, :]
        k_a = vec_ref[3:4, :]
        v_s[rs, :] = v
        vsw_s[rs, :] = pltpu.roll(v, RW_HEAD, 1)
        w_lin = w0 + _dot(jnp.tanh(zw).astype(BF16), w2_ref[...])
        rate_lin = a0 + _dot(za.astype(BF16), a2_ref[...])
        kkv = k * k_k
        ssq = _dot((kkv * kkv).astype(BF16), bd_ref[...])
        yield
        lw = -DECAY_SCALE * _sigmoid(w_lin)
        lw_hi = lw.astype(BF16)
        lw_lo = (lw - lw_hi.astype(F32)).astype(BF16)
        tri = tri_ref[0:n_r, 0:n_r]
        g_in = _dot(tri, lw_hi) + _dot(tri, lw_lo)
        yield
        rate = _sigmoid(rate_lin)
        kk = kkv * lax.rsqrt(jnp.maximum(ssq, L2_EPS * L2_EPS))
        kd = k * (1.0 + (rate - 1.0) * k_a)
        av = -kk
        bv = kk * rate
        yield
        g_ex = g_in - lw
        last = 0 if rev else CHUNK - 1
        g_tot_rows = [g_in[j * CHUNK + last:j * CHUNK + last + 1, :] for j in range(len(cs))]
        eg_rows = [jnp.exp(t) for t in g_tot_rows]
        for j, c in enumerate(cs):
            eg_s[c] = jnp.broadcast_to(eg_rows[j], (8, RW_WIDTH))
        e_neg = jnp.exp(-g_in)
        e_end = e_neg * jnp.concatenate([jnp.broadcast_to(t, (CHUNK, RW_WIDTH)) for t in eg_rows], axis=0)
        at_s[rs, :] = av * jnp.exp(g_ex)
        rt_s[rs, :] = r * jnp.exp(g_in)
        yield
        bh_s[rs, :] = bv * e_neg
        kh_s[rs, :] = kd * e_neg
        be_s[rs, :] = bv * e_end
        ke_s[rs, :] = kd * e_end

    ri = lax.broadcasted_iota(jnp.int32, (CHUNK, CHUNK), 0)
    ci = lax.broadcasted_iota(jnp.int32, (CHUNK, CHUNK), 1)
    m_strict = (ci > ri) if rev else (ci < ri)
    m_incl = (ci >= ri) if rev else (ci <= ri)
    eye = jnp.where(ri == ci, 1.0, 0.0).astype(F32)
    lower = lax.broadcasted_iota(jnp.int32, (CHUNK, LANE), 1) < RW_HEAD
    eye_up = jnp.where(lax.broadcasted_iota(jnp.int32, (CHUNK, LANE), 1)
                       == lax.broadcasted_iota(jnp.int32, (CHUNK, LANE), 0) + RW_HEAD, 1.0, 0.0).astype(F32)
    zeros_half = jnp.zeros((CHUNK, RW_HEAD), F32)

    def pad(t):
        return jnp.concatenate([t, zeros_half], axis=1)

    heads = range(RW_HEADS)
    chunk_order = list(range(n_chunks - 1, -1, -1)) if rev else list(range(n_chunks))
    waves = [chunk_order[w:w + SCAN_WAVE] for w in range(0, n_chunks, SCAN_WAVE)]

    def ld(ref, item):
        c, h = item
        return ref[c * CHUNK:(c + 1) * CHUNK, h * RW_HEAD:(h + 1) * RW_HEAD]

    def ld_v_upper(item):
        c, h = item
        src = v_s if h % 2 else vsw_s
        blk = src[c * CHUNK:(c + 1) * CHUNK, (h // 2) * LANE:(h // 2 + 1) * LANE]
        return jnp.where(lower, 0.0, blk).astype(BF16)

    def ld_decay(item):
        c, h = item
        return eg_s[c][0:1, h * RW_HEAD:(h + 1) * RW_HEAD]

    def chunk_local(items, slot):
        n_it = range(len(items))
        lhs = [jnp.concatenate([ld(at_s, it).astype(BF16), ld(rt_s, it).astype(BF16)], axis=0)
               for it in items]
        a_b = [_dot_nt(lhs[i], ld(bh_s, items[i]).astype(BF16)) for i in n_it]
        a_k = [_dot_nt(lhs[i], ld(kh_s, items[i]).astype(BF16)) for i in n_it]
        yield
        a_ab =[jnp.where(m_strict, a_b[i][0:CHUNK], 0.0) for i in n_it]
        a_ak = [jnp.where(m_strict, a_k[i][0:CHUNK], 0.0).astype(BF16) for i in n_it]
        a_rb = [jnp.where(m_incl, a_b[i][CHUNK:], 0.0).astype(BF16) for i in n_it]
        a_rk = [jnp.where(m_incl, a_k[i][CHUNK:], 0.0).astype(BF16) for i in n_it]
        w_m = [(pad(ld(at_s, items[i])) + _dot(a_ak[i], ld_v_upper(items[i]))).astype(BF16) for i in n_it]
        yield

        z = [pad(a_ab[i]) + eye_up for i in n_it]
        for _ in range(6):
            z = [_dot(z[i][:, 0:CHUNK].astype(BF16), z[i].astype(BF16)) + jnp.where(lower, 0.0, z[i])
                 for i in n_it]
            yield
        tb =[z[i][:, CHUNK:].astype(BF16) for i in n_it]

        pq = [_dot(tb[i], w_m[i]).astype(BF16) for i in n_it]
        yield
        for i in n_it:
            ryq = _dot(a_rb[i], pq[i])
            zr_s[slot, i, 0:CHUNK, :] = (ld(rt_s, items[i]) + ryq[:, 0:RW_HEAD]).astype(BF16)
            zy_s[slot, i] = jnp.where(lower, 0.0, ryq) + _dot(a_rk[i], ld_v_upper(items[i]))
        yield
        for i in n_it:
            mq = _dot_tn(ld(be_s, items[i]).astype(BF16), pq[i])
            zr_s[slot, i, CHUNK:2 * CHUNK, :] = (mq[:, 0:RW_HEAD] + eye * ld_decay(items[i])).astype(BF16)
            zh_s[slot, i] = (jnp.where(lower, 0.0, mq)
                             + _dot_tn(ld(ke_s, items[i]).astype(BF16), ld_v_upper(items[i])))

    keep_state = jnp.where((t - 1) % blocks_per_seq == 0, 0.0, 1.0)
    hst = [h_s[h] * keep_state for h in heads]

    def carried(wave, slot):
        for ci, c in enumerate(wave):
            for h in heads:
                i = ci * RW_HEADS + h
                yh = _dot(zr_s[slot, i], hst[h].astype(BF16))
                y_dst = ya_s if h % 2 else yb_s
                y_dst[c * CHUNK:(c + 1) * CHUNK, (h // 2) * LANE:(h // 2 + 1) * LANE] = yh[0:CHUNK] + zy_s[slot, i]
                hst[h] = yh[CHUNK:] + zh_s[slot, i]
                if h % 4 == 3:
                    yield

    def emit(*gens):
        live = list(gens)
        while live:
            for g in list(live):
                if next(g, StopIteration) is StopIteration:
                    live.remove(g)

    emit(token_prep(waves[0]), carried(waves[0], 0))
    for w, wave in enumerate(waves):
        gens = [chunk_local([(c, h) for c in wave for h in heads], w)]
        if w + 1 < len(waves):
            gens += [token_prep(waves[w + 1]), carried(waves[w + 1], w + 1)]
        emit(*gens)
    for h in heads:
        h_s[h] = hst[h]

    y_dir = ya_s[...] + pltpu.roll(yb_s[...], RW_WIDTH - RW_HEAD, 1)
    if not rev:
        o_ref[...] = y_dir
    else:
        r_k = vec_ref[4:5, :]
        lnx_g = vec_ref[5:6, :]
        lnx_b = vec_ref[6:7, :]
        zc = zf_ref[...].astype(F32)
        r = zc[:, 0:RW_WIDTH]
        k = zc[:, RW_WIDTH:2 * RW_WIDTH]
        v = zc[:, 2 * RW_WIDTH:3 * RW_WIDTH]
        y = yf_ref[...] + y_dir
        inv_n = 1.0 / RW_HEAD
        mu = _dot(y.astype(BF16), bd_ref[...]) * inv_n
        d = y - mu
        var = _dot((d * d).astype(BF16), bd_ref[...]) * inv_n
        yn = d * lax.rsqrt(var + GN_EPS) * lnx_g + lnx_b
        rk = r * k * r_k
        rk_hi = rk.astype(BF16)
        rk_lo = (rk - rk_hi.astype(F32)).astype(BF16)
        bonus = (_dot(rk_hi, bd_ref[...]) + _dot(rk_lo, bd_ref[...])) * v
        zg = zc[:, 3 * RW_WIDTH + 2 * LANE:RW_COLS]
        gate = _dot(_sigmoid(zg).astype(BF16), g2_ref[...])
        o_ref[...] = ((yn + bonus) * gate).astype(o_ref.dtype)


def _scan(z_rw, vecs, w2p, a2p, bd, tri, rev, y_fwd=None, g2=None):
    bsz, seq, _ = z_rw.shape
    rows = SCAN_ROWS
    nb = seq // rows
    n_chunks = rows // CHUNK

    n_waves = n_chunks // SCAN_WAVE
    n_items = SCAN_WAVE * RW_HEADS
    total = bsz * nb

    def at(t):
        j = t % nb
        return t // nb, (nb - 1 - j if rev else j), 0

    def computed(t):
        return at(jnp.minimum(t, total - 1))

    def finished(t):
        return at(jnp.maximum(t - 1, 0))

    in_specs = [pl.BlockSpec((None, rows, RW_COLS), computed)]
    args = [z_rw]
    if rev:
        in_specs += [pl.BlockSpec((None, rows, RW_COLS), finished)]
        args += [z_rw]
    in_specs += [_const_spec(vecs.shape), _const_spec(w2p.shape),
                 _const_spec(a2p.shape), _const_spec(bd.shape), _const_spec(tri.shape)]
    args += [vecs, w2p, a2p, bd, tri]
    scratch = [pltpu.VMEM((rows, RW_WIDTH), F32) for _ in range(8)]
    scratch += [pltpu.VMEM((n_chunks, 8, RW_WIDTH), F32),
                pltpu.VMEM((RW_HEADS, RW_HEAD, LANE), F32),
                pltpu.VMEM((rows, RW_WIDTH), F32), pltpu.VMEM((rows, RW_WIDTH), F32),
                pltpu.VMEM((n_waves, n_items, 2 * CHUNK, RW_HEAD), BF16),
                pltpu.VMEM((n_waves, n_items, CHUNK, LANE), F32),
                pltpu.VMEM((n_waves, n_items, RW_HEAD, LANE), F32)]
    if rev:
        in_specs += [pl.BlockSpec((None, rows, RW_WIDTH), finished), _const_spec(g2.shape)]
        args += [y_fwd, g2]
        out_dtype = BF16
    else:
        out_dtype = F32
    return pl.pallas_call(
        functools.partial(_scan_kernel, rev=rev, blocks_per_seq=nb),
        grid=(total + 1,),
        in_specs=in_specs,
        out_specs=pl.BlockSpec((None, rows, RW_WIDTH), finished),
        out_shape=jax.ShapeDtypeStruct((bsz, seq, RW_WIDTH), out_dtype),
        scratch_shapes=scratch,
        compiler_params=pltpu.CompilerParams(dimension_semantics=("arbitrary",),
                                             vmem_limit_bytes=VMEM_LIMIT),
        name="scan_bwd" if rev else "scan_fwd",
    )(*args)


def _rms(xf, g):
    return xf * lax.rsqrt(jnp.mean(xf * xf, axis=-1, keepdims=True) + RMS_EPS) * g


def _attn_kernel(zkv_ref, zq_ref, cos_ref, sin_ref, gq_ref, gkv_ref, wq1_ref, wk_ref, wvt_ref,
                 o_ref, k_s, vt_s):
    seq = zkv_ref.shape[0]
    tq = zq_ref.shape[0]
    kt = 256
    i = pl.program_id(1)

    @pl.when(i == 0)
    def _():
        def kv_body(t, carry):
            r0 = pl.multiple_of(t * kt, kt)
            zk = zkv_ref[pl.ds(r0, kt), :].astype(F32)
            kvn = _rms(zk[:, 0:KV_LORA], gkv_ref[...]).astype(BF16)
            kn = _dot(kvn, wk_ref[...])
            cs = cos_ref[pl.ds(r0, kt), :]
            sn = sin_ref[pl.ds(r0, kt), :]
            kpe = zk[:, KV_LORA:KV_LORA + LANE] * cs + zk[:, KV_LORA + LANE:KV_LORA + 2 * LANE] * sn
            for h in range(MLA_HEADS):
                sl = slice(h * HEAD_PAD, (h + 1) * HEAD_PAD)
                k_s[pl.ds(r0, kt), sl] = (kn[:, sl] + kpe).astype(BF16)
            vt = _dot_nt(wvt_ref[...], kvn).astype(BF16)
            ones = jnp.ones((VT_ROWS - V_HEAD, kt), BF16)
            for h in range(MLA_HEADS):
                vt_s[h * VT_ROWS:(h + 1) * VT_ROWS, pl.ds(r0, kt)] = jnp.concatenate(
                    [vt[h * V_HEAD:(h + 1) * V_HEAD], ones], axis=0)
            return carry
        lax.fori_loop(0, seq // kt, kv_body, 0)

    q0 = pl.multiple_of(i * tq, tq)
    qn = _rms(zq_ref[...].astype(F32), gq_ref[...]).astype(BF16)
    q1 = _dot(qn, wq1_ref[...])
    half = QK_ROPE // 2
    lane = lax.broadcasted_iota(jnp.int32, q1.shape, 1) % HEAD_PAD
    q2 = jnp.where((lane >= QK_NOPE) & (lane < QK_NOPE + half), -pltpu.roll(q1, q1.shape[1] - half, 1),
                   jnp.where((lane >= QK_NOPE + half) & (lane < QK_NOPE + QK_ROPE), pltpu.roll(q1, half, 1), 0.0))
    cs = cos_ref[pl.ds(q0, tq), :]
    sn = sin_ref[pl.ds(q0, tq), :]
    scale = (QK_NOPE + QK_ROPE) ** -0.5 * LOG2_E

    kparts = [slice(j * (seq // ATTN_KEY_PARTS), (j + 1) * (seq // ATTN_KEY_PARTS)) for j in range(ATTN_KEY_PARTS)]

    def scores(h):
        sl = slice(h * HEAD_PAD, (h + 1) * HEAD_PAD)
        qh = ((q1[:, sl] * cs + q2[:, sl] * sn) * scale).astype(BF16)
        return [_dot_nt(k_s[kp, sl], qh) for kp in kparts]

    outs = []
    st_next = scores(0)
    for h in range(MLA_HEADS):
        st = st_next
        if h + 1 < MLA_HEADS:
            st_next = scores(h + 1)
        m = functools.reduce(jnp.maximum, [jnp.max(t, axis=0, keepdims=True) for t in st])
        ps = [jnp.exp2(t - m).astype(BF16) for t in st]
        ol = sum(_dot(vt_s[h * VT_ROWS:(h + 1) * VT_ROWS, kp], p) for kp, p in zip(kparts, ps))
        outs.append(ol[0:V_HEAD] / ol[V_HEAD:V_HEAD + 1])
    o_ref[...] = jnp.concatenate(outs, axis=0).T.astype(o_ref.dtype)


def _attn(z_kv, z_q, cos_t, sin_t, gq, gkv, wq1, wk, wvt):
    bsz, seq, _ = z_kv.shape
    tq = 256
    kw = MLA_HEADS * HEAD_PAD
    return pl.pallas_call(
        _attn_kernel,
        grid=(bsz, seq // tq),
        in_specs=[pl.BlockSpec((None, seq, ZKV_COLS), lambda b, i: (b, 0, 0)),
                  pl.BlockSpec((None, tq, Q_LORA), lambda b, i: (b, i, 0)),
                  pl.BlockSpec((None, seq, LANE), lambda b, i: (b, 0, 0)),
                  pl.BlockSpec((None, seq, LANE), lambda b, i: (b, 0, 0)),
                  _const_spec(gq.shape), _const_spec(gkv.shape), _const_spec(wq1.shape),
                  _const_spec(wk.shape), _const_spec(wvt.shape)],
        out_specs=pl.BlockSpec((None, tq, MLA_HEADS * V_HEAD), lambda b, i: (b, i, 0)),
        out_shape=jax.ShapeDtypeStruct((bsz, seq, MLA_HEADS * V_HEAD), BF16),
        scratch_shapes=[pltpu.VMEM((seq, kw), BF16), pltpu.VMEM((MLA_HEADS * VT_ROWS, seq), BF16)],
        compiler_params=pltpu.CompilerParams(dimension_semantics=("parallel", "arbitrary"),
                                             vmem_limit_bytes=VMEM_LIMIT),
        name="attn",
    )(z_kv, z_q, cos_t, sin_t, gq, gkv, wq1, wk, wvt)


def _mix_kernel(yrw_ref, ymla_ref, ga_ref, gb_ref, x_ref, mod_ref, wr_ref, wm_ref, wo_ref, ln_ref, o_ref):
    gate1 = mod_ref[2:3, :]
    br_rw = _dot(yrw_ref[...], wr_ref[...])
    br_mla = _dot(ymla_ref[...], wm_ref[...])
    mixed = (_sigmoid(ga_ref[...].astype(F32)) * br_rw
             + _sigmoid(gb_ref[...].astype(F32)) * br_mla)
    out = _dot(mixed.astype(BF16), wo_ref[...])
    t = DN_ALPHA * x_ref[...] + (1.0 + gate1) * out
    o_ref[...] = _layer_norm(t, ln_ref[0:1, :], ln_ref[1:2, :])


def _mix(y_rw, y_mla, z_g, x, mod3, wr, wm, wo, ln):
    bsz, seq, _ = x.shape
    tm = 512
    row = lambda n: pl.BlockSpec((None, tm, n), lambda b, i: (b, i, 0))
    return pl.pallas_call(
        _mix_kernel,
        grid=(bsz, seq // tm),
        in_specs=[row(RW_WIDTH), row(MLA_HEADS * V_HEAD),
                  pl.BlockSpec((None, tm, D_MODEL), lambda b, i: (b, i, 0)),
                  pl.BlockSpec((None, tm, D_MODEL), lambda b, i: (b, i, 1)),
                  row(D_MODEL),
                  pl.BlockSpec((None, 6, D_MODEL), lambda b, i: (b, 0, 0)),
                  _const_spec(wr.shape), _const_spec(wm.shape), _const_spec(wo.shape),
                  _const_spec(ln.shape)],
        out_specs=row(D_MODEL),
        out_shape=jax.ShapeDtypeStruct((bsz, seq, D_MODEL), F32),
        compiler_params=pltpu.CompilerParams(dimension_semantics=("parallel", "parallel"),
                                             vmem_limit_bytes=VMEM_LIMIT),
        name="mix",
    )(y_rw, y_mla, z_g, z_g, x, mod3, wr, wm, wo, ln)


def _ffn_kernel(x_ref, mod_ref, w1_ref, w2_ref, ln_ref, o_ref):
    shift = mod_ref[3:4, :]
    scale = mod_ref[4:5, :]
    gate2 = mod_ref[5:6, :]
    x1 = x_ref[...]
    h = (x1 * (1.0 + scale) + shift).astype(BF16)
    kc = 1024
    acc = jnp.zeros(x1.shape, F32)
    for c in range(D_FF // kc):
        u = jnp.maximum(_dot(h, w1_ref[:, c * kc:(c + 1) * kc]), 0.0)
        acc = acc + _dot((u * u).astype(BF16), w2_ref[c * kc:(c + 1) * kc, :])
    t = DN_ALPHA * x1 + (1.0 + gate2) * acc
    o_ref[...] = _layer_norm(t, ln_ref[0:1, :], ln_ref[1:2, :])


def _ffn(x1, mod3, w1, w2, ln):
    bsz, seq, _ = x1.shape
    tm = 512
    row = pl.BlockSpec((None, tm, D_MODEL), lambda b, i: (b, i, 0))
    return pl.pallas_call(
        _ffn_kernel,
        grid=(bsz, seq // tm),
        in_specs=[row, pl.BlockSpec((None, 6, D_MODEL), lambda b, i: (b, 0, 0)),
                  _const_spec(w1.shape), _const_spec(w2.shape), _const_spec(ln.shape)],
        out_specs=row,
        out_shape=jax.ShapeDtypeStruct((bsz, seq, D_MODEL), F32),
        compiler_params=pltpu.CompilerParams(dimension_semantics=("parallel", "parallel"),
                                             vmem_limit_bytes=VMEM_LIMIT),
        name="ffn",
    )(x1, mod3, w1, w2, ln)


def _pad_cols(w, left, total):
    return jnp.pad(w, ((0, 0), (left, total - left - w.shape[1])))


def _inproj_weight(w_in):
    o = 0
    w_rw = w_in[:, o:o + RW_COLS]; o += RW_COLS
    w_q = w_in[:, o:o + Q_LORA]; o += Q_LORA
    w_kv = w_in[:, o:o + KV_LORA]; o += KV_LORA
    w_kr = w_in[:, o:o + QK_ROPE]; o += QK_ROPE
    w_g = w_in[:, o:o + 2 * D_MODEL]
    half = QK_ROPE // 2
    w_kr_rot = jnp.concatenate([-w_kr[:, half:], w_kr[:, :half]], axis=1)
    w_all = jnp.concatenate([w_rw, w_kv, _pad_cols(w_kr, QK_NOPE, LANE), _pad_cols(w_kr_rot, QK_NOPE, LANE),
                             w_q, w_g], axis=1)
    return w_all.astype(BF16)


def _mla_weights(w_uq, w_ukv):
    half = QK_ROPE // 2
    q = w_uq.reshape(Q_LORA, MLA_HEADS, QK_NOPE + QK_ROPE)
    q_nope, q_1, q_2 = q[..., :QK_NOPE], q[..., QK_NOPE:QK_NOPE + half], q[..., QK_NOPE + half:]
    zpad = jnp.zeros((Q_LORA, MLA_HEADS, HEAD_PAD - QK_NOPE - QK_ROPE), F32)
    wq1 = jnp.concatenate([q_nope, q_1, q_2, zpad], axis=-1).reshape(Q_LORA, MLA_HEADS * HEAD_PAD)
    kv = w_ukv.reshape(KV_LORA, MLA_HEADS, QK_NOPE + V_HEAD)
    zhalf = jnp.zeros((KV_LORA, MLA_HEADS, HEAD_PAD - QK_NOPE), F32)
    wk = jnp.concatenate([kv[..., :QK_NOPE], zhalf], axis=-1).reshape(KV_LORA, MLA_HEADS * HEAD_PAD)
    wvt = kv[..., QK_NOPE:].reshape(KV_LORA, MLA_HEADS * V_HEAD).T
    return wq1.astype(BF16), wk.astype(BF16), wvt.astype(BF16)


def _scan_constants(rev):
    rows = np.arange(SCAN_ROWS)
    same = (rows[:, None] // CHUNK) == (rows[None, :] // CHUNK)
    tri = same & ((rows[None, :] >= rows[:, None]) if rev else (rows[None, :] <= rows[:, None]))
    ch = np.arange(RW_WIDTH)
    bd = (ch[:, None] // RW_HEAD) == (ch[None, :] // RW_HEAD)
    return jnp.asarray(tri, BF16), jnp.asarray(bd, BF16)


def _lora_pair(w, d):
    zero = jnp.zeros_like(w[0])
    return jnp.concatenate([zero, w[1]] if d else [w[0], zero], axis=0).astype(BF16)


def kernel(x, c, positions, w_ada, b_ada, w_in, rw_conv, rw_w0, rw_w2, rw_a0, rw_a2, rw_k_k, rw_k_a, rw_r_k, rw_g2, rw_lnx_g, rw_lnx_b, mla_q_norm_g, mla_kv_norm_g, mla_w_uq, mla_w_ukv, w_br_rwkv, w_br_mla, w_out, ln1_g, ln1_b, w_ff1, w_ff2, ln2_g, ln2_b):
    bsz, seq, _ = x.shape
    cos_t, sin_t = _rope_tables(positions)
    for l in range(DEPTH):
        mod3 = _ada(c, w_ada[l], b_ada[l]).reshape(bsz, 6, D_MODEL)
        z_rw, z_kv, z_q, z_g = _inproj(x, mod3, _inproj_weight(w_in[l]), rw_conv[l])

        zero = jnp.zeros((RW_WIDTH,), F32)
        y_rw = None
        for d in (0, 1):
            vecs = jnp.stack([rw_w0[l, d], rw_a0[l, d], rw_k_k[l], rw_k_a[l], rw_r_k[l],
                              rw_lnx_g[l], rw_lnx_b[l], zero])
            tri, bd = _scan_constants(bool(d))
            y_rw = _scan(z_rw, vecs, _lora_pair(rw_w2[l], d), _lora_pair(rw_a2[l], d), bd, tri,
                         rev=bool(d), y_fwd=y_rw, g2=rw_g2[l].astype(BF16))

        wq1, wk, wvt = _mla_weights(mla_w_uq[l], mla_w_ukv[l])
        y_mla = _attn(z_kv, z_q, cos_t, sin_t, mla_q_norm_g[l].reshape(1, Q_LORA),
                      mla_kv_norm_g[l].reshape(1, KV_LORA), wq1, wk, wvt)

        x = _mix(y_rw, y_mla, z_g, x, mod3, w_br_rwkv[l].astype(BF16), w_br_mla[l].astype(BF16),
                 w_out[l].astype(BF16), jnp.stack([ln1_g[l], ln1_b[l]]))
        x = _ffn(x, mod3, w_ff1[l].astype(BF16), w_ff2[l].astype(BF16), jnp.stack([ln2_g[l], ln2_b[l]]))
    return x
```

```python
import functools

import numpy as np
import jax
import jax.numpy as jnp
from jax import lax
from jax.experimental import pallas as pl
from jax.experimental.pallas import tpu as pltpu

F32 = jnp.float32
BF16 = jnp.bfloat16

D_MODEL = 1024
RW_HEADS = 8
RW_HEAD = 64
RW_WIDTH = RW_HEADS * RW_HEAD
DECAY_LORA = 64
AAA_LORA = 64
GATE_LORA = 128
RW_COLS = 3 * RW_WIDTH + 2 * DECAY_LORA + 2 * AAA_LORA + GATE_LORA
MLA_HEADS = 8
QK_NOPE = 64
QK_ROPE = 32
V_HEAD = 64
Q_LORA = 384
KV_LORA = 256
ROPE_THETA = 10000.0
D_FF = 4 * D_MODEL
LN_EPS = 1e-5
RMS_EPS = 1e-6
GN_EPS = 64e-5
L2_EPS = 1e-12
DEPTH = 1
DN_ALPHA = (2.0 * DEPTH) ** 0.25
LOG2_E = 1.4426950408889634
DECAY_SCALE = 0.6065306597126334

LANE = 128
CHUNK = 64
SCAN_ROWS = 1024
SCAN_WAVE = 4
X_HALO_ROWS = 8
HEAD_PAD = 128
ATTN_KEY_PARTS = 4
VT_ROWS = 80
ZKV_COLS = KV_LORA + 2 * LANE
VMEM_LIMIT = 56 * 1024 * 1024


def _dot(a, b):
    return jnp.dot(a, b, preferred_element_type=F32)


def _dot_nt(a, b):
    return lax.dot_general(a, b, (((1,), (1,)), ((), ())), preferred_element_type=F32)


def _dot_tn(a, b):
    return lax.dot_general(a, b, (((0,), (0,)), ((), ())), preferred_element_type=F32)


def _const_spec(shape):
    zeros = (0,) * len(shape)
    return pl.BlockSpec(shape, lambda *_: zeros)


def _layer_norm(t, g, b):
    mu = jnp.mean(t, axis=-1, keepdims=True)
    d = t - mu
    var = jnp.mean(d * d, axis=-1, keepdims=True)
    return d * lax.rsqrt(var + LN_EPS) * g + b


def _ada_kernel(c_ref, w_ref, b_ref, o_ref):
    c = c_ref[...]
    act = c * jax.nn.sigmoid(c)
    o_ref[...] = _dot(act.astype(BF16), w_ref[...].astype(BF16)) + b_ref[...]


def _ada(c, w, b):
    bsz = c.shape[0]
    n = w.shape[1]
    tn = 1536
    return pl.pallas_call(
        _ada_kernel,
        grid=(n // tn,),
        in_specs=[_const_spec((bsz, D_MODEL)),
                  pl.BlockSpec((D_MODEL, tn), lambda j: (0, j)),
                  pl.BlockSpec((1, tn), lambda j: (0, j))],
        out_specs=pl.BlockSpec((bsz, tn), lambda j: (0, j)),
        out_shape=jax.ShapeDtypeStruct((bsz, n), F32),
        name="ada",
    )(c, w, b.reshape(1, n))


def _rope_kernel(pos_ref, inv_ref, place_ref, one_ref, cos_ref, sin_ref):
    ang = inv_ref[...] * pos_ref[...].astype(F32)

    def table(t):
        hi = t.astype(BF16)
        lo = (t - hi.astype(F32)).astype(BF16)
        return _dot_tn(hi, place_ref[...]) + _dot_tn(lo, place_ref[...])

    cos_ref[...] = table(jnp.cos(ang)) + one_ref[...]
    sin_ref[...] = table(jnp.sin(ang))


def _rope_tables(positions):
    bsz, seq = positions.shape
    half = QK_ROPE // 2
    inv = (ROPE_THETA ** (-np.arange(half, dtype=np.float32) / half)).reshape(half, 1)
    place = np.zeros((half, LANE), np.float32)
    place[np.arange(half), QK_NOPE + np.arange(half)] = 1.0
    place[np.arange(half), QK_NOPE + half + np.arange(half)] = 1.0
    one = np.zeros((1, LANE), np.float32)
    one[0, :QK_NOPE] = 1.0
    spec = pl.BlockSpec((None, seq, LANE), lambda b: (b, 0, 0))
    return pl.pallas_call(
        _rope_kernel,
        grid=(bsz,),
        in_specs=[pl.BlockSpec((None, 1, seq), lambda b: (b, 0, 0)), _const_spec((half, 1)),
                  _const_spec((half, LANE)), _const_spec((1, LANE))],
        out_specs=[spec, spec],
        out_shape=[jax.ShapeDtypeStruct((bsz, seq, LANE), F32)] * 2,
        name="rope",
    )(positions.reshape(bsz, 1, seq), jnp.asarray(inv), jnp.asarray(place, BF16), jnp.asarray(one))


def _inproj_kernel(x_ref, xp_ref, xn_ref, mod_ref, w_ref, cw_ref, zrw_ref, zkv_ref, zq_ref, zg_ref):
    i = pl.program_id(1)
    nb = pl.num_programs(1)
    rows = x_ref.shape[0]
    shift = mod_ref[0:1, :]
    scale = mod_ref[1:2, :]
    h = (x_ref[...] * (1.0 + scale) + shift).astype(BF16)
    col = RW_COLS
    for o_ref in (zkv_ref, zq_ref, zg_ref):
        n = o_ref.shape[-1]
        o_ref[...] = _dot(h, w_ref[:, col:col + n]).astype(o_ref.dtype)
        col += n

    z = _dot(h, w_ref[:, 0:RW_COLS])
    x_halo = jnp.concatenate([xp_ref[...], xn_ref[...]], axis=0)
    z_halo = _dot((x_halo * (1.0 + scale) + shift).astype(BF16), w_ref[:, 0:RW_COLS])
    prev_row = z_halo[X_HALO_ROWS - 1:X_HALO_ROWS, :] * jnp.where(i > 0, 1.0, 0.0)
    next_row = z_halo[X_HALO_ROWS:X_HALO_ROWS + 1, :] * jnp.where(i < nb - 1, 1.0, 0.0)
    row_id = lax.broadcasted_iota(jnp.int32, (rows, 1), 0)
    z_dn = jnp.where(row_id == 0, prev_row, pltpu.roll(z, 1, 0))
    z_up = jnp.where(row_id == rows - 1, next_row, pltpu.roll(z, rows - 1, 0))
    zrw_ref[...] = (cw_ref[0:1, :] * z_dn + cw_ref[1:2, :] * z + cw_ref[2:3, :] * z_up).astype(zrw_ref.dtype)


def _inproj(x, mod3, w_all, conv_w):
    bsz, seq, _ = x.shape
    tm = 512
    hpb = tm // X_HALO_ROWS
    n_halo = seq // X_HALO_ROWS
    widths = (RW_COLS, ZKV_COLS, Q_LORA, 2 * D_MODEL)
    return pl.pallas_call(
        _inproj_kernel,
        grid=(bsz, seq // tm),
        in_specs=[pl.BlockSpec((None, tm, D_MODEL), lambda b, i: (b, i, 0)),
                  pl.BlockSpec((None, X_HALO_ROWS, D_MODEL), lambda b, i: (b, jnp.maximum(i * hpb - 1, 0), 0)),
                  pl.BlockSpec((None, X_HALO_ROWS, D_MODEL),
                               lambda b, i: (b, jnp.minimum((i + 1) * hpb, n_halo - 1), 0)),
                  pl.BlockSpec((None, 6, D_MODEL), lambda b, i: (b, 0, 0)),
                  _const_spec(w_all.shape), _const_spec(conv_w.shape)],
        out_specs=[pl.BlockSpec((None, tm, n), lambda b, i: (b, i, 0)) for n in widths],
        out_shape=[jax.ShapeDtypeStruct((bsz, seq, n), BF16) for n in widths],
        compiler_params=pltpu.CompilerParams(dimension_semantics=("parallel", "parallel"),
                                             vmem_limit_bytes=VMEM_LIMIT),
        name="inproj",
    )(x, x, x, mod3, w_all, conv_w)


def _sigmoid(x):
    return 0.5 + 0.5 * jnp.tanh(0.5 * x)


def _scan_kernel(*refs, rev):
    if rev:
        (zm_ref, vec_ref, w2_ref, a2_ref, bd_ref, tri_ref,
         yf_ref, g2_ref, o_ref,
         at_s, rt_s, bh_s, kh_s, be_s, ke_s, v_s, vsw_s, eg_s, h_s, ya_s, yb_s) = refs
    else:
        (zm_ref, vec_ref, w2_ref, a2_ref, bd_ref, tri_ref,
         o_ref,
         at_s, rt_s, bh_s, kh_s, be_s, ke_s, v_s, vsw_s, eg_s, h_s, ya_s, yb_s) = refs
    rows = zm_ref.shape[0]
    n_chunks = rows // CHUNK
    i = pl.program_id(1)

    @pl.when(i == 0)
    def _():
        h_s[...] = jnp.zeros_like(h_s)

    def token_prep(wave):
        cs = sorted(wave)
        n_r = len(cs) * CHUNK
        rs = slice(cs[0] * CHUNK, cs[0] * CHUNK + n_r)
        zc = zm_ref[rs, :].astype(F32)
        r = zc[:, 0:RW_WIDTH]
        k = zc[:, RW_WIDTH:2 * RW_WIDTH]
        v = zc[:, 2 * RW_WIDTH:3 * RW_WIDTH]
        zw = zc[:, 3 * RW_WIDTH:3 * RW_WIDTH + LANE]
        za = zc[:, 3 * RW_WIDTH + LANE:3 * RW_WIDTH + 2 * LANE]
        w0 = vec_ref[0:1, :]
        a0 = vec_ref[1:2, :]
        k_k = vec_ref[2:3, :]
        k_a = vec_ref[3:4, :]
        v_s[rs, :] = v
        vsw_s[rs, :] = pltpu.roll(v, RW_HEAD, 1)
        w_lin = w0 + _dot(jnp.tanh(zw).astype(BF16), w2_ref[...])
        rate_lin = a0 + _dot(za.astype(BF16), a2_ref[...])
        kkv = k * k_k
        ssq = _dot((kkv * kkv).astype(BF16), bd_ref[...])
        yield
        lw = -DECAY_SCALE * _sigmoid(w_lin)
        lw_hi = lw.astype(BF16)
        lw_lo = (lw - lw_hi.astype(F32)).astype(BF16)
        tri = tri_ref[0:n_r, 0:n_r]
        g_in = _dot(tri, lw_hi) + _dot(tri, lw_lo)
        yield
        rate = _sigmoid(rate_lin)
        kk = kkv * lax.rsqrt(jnp.maximum(ssq, L2_EPS * L2_EPS))
        kd = k * (1.0 + (rate - 1.0) * k_a)
        av = -kk
        bv = kk * rate
        yield
        g_ex = g_in - lw
        last = 0 if rev else CHUNK - 1
        g_tot_rows = [g_in[j * CHUNK + last:j * CHUNK + last + 1, :] for j in range(len(cs))]
        eg_rows = [jnp.exp(t) for t in g_tot_rows]
        for j, c in enumerate(cs):
            eg_s[c] = jnp.broadcast_to(eg_rows[j], (8, RW_WIDTH))
        e_neg = jnp.exp(-g_in)
        e_end = e_neg * jnp.concatenate([jnp.broadcast_to(t, (CHUNK, RW_WIDTH)) for t in eg_rows], axis=0)
        at_s[rs, :] = av * jnp.exp(g_ex)
        rt_s[rs, :] = r * jnp.exp(g_in)
        yield
        bh_s[rs, :] = bv * e_neg
        kh_s[rs, :] = kd * e_neg
        be_s[rs, :] = bv * e_end
        ke_s[rs, :] = kd * e_end

    ri = lax.broadcasted_iota(jnp.int32, (CHUNK, CHUNK), 0)
    ci = lax.broadcasted_iota(jnp.int32, (CHUNK, CHUNK), 1)
    m_strict = (ci > ri) if rev else (ci < ri)
    m_incl = (ci >= ri) if rev else (ci <= ri)
    eye = jnp.where(ri == ci, 1.0, 0.0).astype(F32)
    lower = lax.broadcasted_iota(jnp.int32, (CHUNK, LANE), 1) < RW_HEAD
    eye_up = jnp.where(lax.broadcasted_iota(jnp.int32, (CHUNK, LANE), 1)
                       == lax.broadcasted_iota(jnp.int32, (CHUNK, LANE), 0) + RW_HEAD, 1.0, 0.0).astype(F32)
    zeros_half = jnp.zeros((CHUNK, RW_HEAD), F32)

    def pad(t):
        return jnp.concatenate([t, zeros_half], axis=1)

    heads = range(RW_HEADS)
    chunk_order = list(range(n_chunks - 1, -1, -1)) if rev else list(range(n_chunks))
    waves = [chunk_order[w:w + SCAN_WAVE] for w in range(0, n_chunks, SCAN_WAVE)]

    def ld(ref, item):
        c, h = item
        return ref[c * CHUNK:(c + 1) * CHUNK, h * RW_HEAD:(h + 1) * RW_HEAD]

    def ld_v_upper(item):
        c, h = item
        src = v_s if h % 2 else vsw_s
        blk = src[c * CHUNK:(c + 1) * CHUNK, (h // 2) * LANE:(h // 2 + 1) * LANE]
        return jnp.where(lower, 0.0, blk).astype(BF16)

    def ld_decay(item):
        c, h = item
        return eg_s[c][0:1, h * RW_HEAD:(h + 1) * RW_HEAD]

    def chunk_local(items, out):
        n_it = range(len(items))
        rt = [ld(rt_s, it) for it in items]
        atb = [ld(at_s, it).astype(BF16) for it in items]
        lhs = [jnp.concatenate([atb[i], rt[i].astype(BF16)], axis=0) for i in n_it]
        a_b = [_dot_nt(lhs[i], ld(bh_s, items[i]).astype(BF16)) for i in n_it]
        a_k = [_dot_nt(lhs[i], ld(kh_s, items[i]).astype(BF16)) for i in n_it]
        yield
        a_ab =[jnp.where(m_strict, a_b[i][0:CHUNK], 0.0) for i in n_it]
        a_ak = [jnp.where(m_strict, a_k[i][0:CHUNK], 0.0).astype(BF16) for i in n_it]
        a_rb = [jnp.where(m_incl, a_b[i][CHUNK:], 0.0).astype(BF16) for i in n_it]
        a_rk = [jnp.where(m_incl, a_k[i][CHUNK:], 0.0).astype(BF16) for i in n_it]
        v_up = [ld_v_upper(it) for it in items]
        akv = [_dot(a_ak[i], v_up[i]) for i in n_it]
        yield

        z = [pad(a_ab[i]) + eye_up for i in n_it]
        for _ in range(6):
            z = [_dot(z[i][:, 0:CHUNK].astype(BF16), z[i].astype(BF16)) + jnp.where(lower, 0.0, z[i])
                 for i in n_it]
            yield
        tb =[z[i][:, CHUNK:].astype(BF16) for i in n_it]

        w_m = [(pad(ld(at_s, items[i])) + akv[i]).astype(BF16) for i in n_it]
        pq = [_dot(tb[i], w_m[i]).astype(BF16) for i in n_it]
        yield
        ryq =[_dot(a_rb[i], pq[i]) for i in n_it]
        ry = [(rt[i] + ryq[i][:, 0:RW_HEAD]).astype(BF16) for i in n_it]
        y0 = [jnp.where(lower, 0.0, ryq[i]) + _dot(a_rk[i], v_up[i]) for i in n_it]
        yield
        be =[ld(be_s, it).astype(BF16) for it in items]
        ke = [ld(ke_s, it).astype(BF16) for it in items]
        mq = [_dot_tn(be[i], pq[i]) for i in n_it]
        m_m = [(mq[i][:, 0:RW_HEAD] + eye * ld_decay(items[i])).astype(BF16) for i in n_it]
        h0 = [jnp.where(lower, 0.0, mq[i]) + _dot_tn(ke[i], v_up[i]) for i in n_it]
        ry_m = [jnp.concatenate([ry[i], m_m[i]], axis=0) for i in n_it]
        out.update(ry_m=ry_m, y0=y0, h0=h0)

    hst = [h_s[h] for h in heads]

    def carried(wave, res):
        for ci, c in enumerate(wave):
            for h in heads:
                i = ci * RW_HEADS + h
                yh = _dot(res["ry_m"][i], hst[h].astype(BF16))
                y_dst = ya_s if h % 2 else yb_s
                y_dst[c * CHUNK:(c + 1) * CHUNK, (h // 2) * LANE:(h // 2 + 1) * LANE] = yh[0:CHUNK] + res["y0"][i]
                hst[h] = yh[CHUNK:] + res["h0"][i]
                if h % 4 == 3:
                    yield

    def emit(*gens):
        live = list(gens)
        while live:
            for g in list(live):
                if next(g, StopIteration) is StopIteration:
                    live.remove(g)

    results = [dict() for _ in waves]
    emit(token_prep(waves[0]))
    for w, wave in enumerate(waves):
        gens = [chunk_local([(c, h) for c in wave for h in heads], results[w])]
        if w + 1 < len(waves):
            gens.append(token_prep(waves[w + 1]))
        if w > 0:
            gens.append(carried(waves[w - 1], results[w - 1]))
        emit(*gens)
    emit(carried(waves[-1], results[-1]))
    for h in heads:
        h_s[h] = hst[h]

    y_dir = ya_s[...] + pltpu.roll(yb_s[...], RW_WIDTH - RW_HEAD, 1)
    if not rev:
        o_ref[...] = y_dir
    else:
        r_k = vec_ref[4:5, :]
        lnx_g = vec_ref[5:6, :]
        lnx_b = vec_ref[6:7, :]
        zc = zm_ref[...].astype(F32)
        r = zc[:, 0:RW_WIDTH]
        k = zc[:, RW_WIDTH:2 * RW_WIDTH]
        v = zc[:, 2 * RW_WIDTH:3 * RW_WIDTH]
        y = yf_ref[...] + y_dir
        inv_n = 1.0 / RW_HEAD
        mu = _dot(y.astype(BF16), bd_ref[...]) * inv_n
        d = y - mu
        var = _dot((d * d).astype(BF16), bd_ref[...]) * inv_n
        yn = d * lax.rsqrt(var + GN_EPS) * lnx_g + lnx_b
        rk = r * k * r_k
        rk_hi = rk.astype(BF16)
        rk_lo = (rk - rk_hi.astype(F32)).astype(BF16)
        bonus = (_dot(rk_hi, bd_ref[...]) + _dot(rk_lo, bd_ref[...])) * v
        zg = zc[:, 3 * RW_WIDTH + 2 * LANE:RW_COLS]
        gate = _dot(_sigmoid(zg).astype(BF16), g2_ref[...])
        o_ref[...] = ((yn + bonus) * gate).astype(o_ref.dtype)


def _scan(z_rw, vecs, w2p, a2p, bd, tri, rev, y_fwd=None, g2=None):
    bsz, seq, _ = z_rw.shape
    rows = SCAN_ROWS
    nb = seq // rows
    n_chunks = rows // CHUNK

    def blk(i):
        return nb - 1 - i if rev else i

    in_specs = [
        pl.BlockSpec((None, rows, RW_COLS), lambda b, i: (b, blk(i), 0)),
        _const_spec(vecs.shape), _const_spec(w2p.shape),
        _const_spec(a2p.shape), _const_spec(bd.shape), _const_spec(tri.shape),
    ]
    args = [z_rw, vecs, w2p, a2p, bd, tri]
    scratch = [pltpu.VMEM((rows, RW_WIDTH), F32) for _ in range(8)]
    scratch += [pltpu.VMEM((n_chunks, 8, RW_WIDTH), F32),
                pltpu.VMEM((RW_HEADS, RW_HEAD, LANE), F32),
                pltpu.VMEM((rows, RW_WIDTH), F32), pltpu.VMEM((rows, RW_WIDTH), F32)]
    if rev:
        in_specs += [pl.BlockSpec((None, rows, RW_WIDTH), lambda b, i: (b, blk(i), 0)),
                     _const_spec(g2.shape)]
        args += [y_fwd, g2]
        out_dtype = BF16
    else:
        out_dtype = F32
    return pl.pallas_call(
        functools.partial(_scan_kernel, rev=rev),
        grid=(bsz, nb),
        in_specs=in_specs,
        out_specs=pl.BlockSpec((None, rows, RW_WIDTH), lambda b, i: (b, blk(i), 0)),
        out_shape=jax.ShapeDtypeStruct((bsz, seq, RW_WIDTH), out_dtype),
        scratch_shapes=scratch,
        compiler_params=pltpu.CompilerParams(dimension_semantics=("parallel", "arbitrary"),
                                             vmem_limit_bytes=VMEM_LIMIT),
        name="scan_bwd" if rev else "scan_fwd",
    )(*args)


def _rms(xf, g):
    return xf * lax.rsqrt(jnp.mean(xf * xf, axis=-1, keepdims=True) + RMS_EPS) * g


def _attn_kernel(zkv_ref, zq_ref, cos_ref, sin_ref, gq_ref, gkv_ref, wq1_ref, wk_ref, wvt_ref,
                 o_ref, k_s, vt_s):
    seq = zkv_ref.shape[0]
    tq = zq_ref.shape[0]
    kt = 256
    i = pl.program_id(1)

    @pl.when(i == 0)
    def _():
        def kv_body(t, carry):
            r0 = pl.multiple_of(t * kt, kt)
            zk = zkv_ref[pl.ds(r0, kt), :].astype(F32)
            kvn = _rms(zk[:, 0:KV_LORA], gkv_ref[...]).astype(BF16)
            kn = _dot(kvn, wk_ref[...])
            cs = cos_ref[pl.ds(r0, kt), :]
            sn = sin_ref[pl.ds(r0, kt), :]
            kpe = zk[:, KV_LORA:KV_LORA + LANE] * cs + zk[:, KV_LORA + LANE:KV_LORA + 2 * LANE] * sn
            for h in range(MLA_HEADS):
                sl = slice(h * HEAD_PAD, (h + 1) * HEAD_PAD)
                k_s[pl.ds(r0, kt), sl] = (kn[:, sl] + kpe).astype(BF16)
            vt = _dot_nt(wvt_ref[...], kvn).astype(BF16)
            ones = jnp.ones((VT_ROWS - V_HEAD, kt), BF16)
            for h in range(MLA_HEADS):
                vt_s[h * VT_ROWS:(h + 1) * VT_ROWS, pl.ds(r0, kt)] = jnp.concatenate(
                    [vt[h * V_HEAD:(h + 1) * V_HEAD], ones], axis=0)
            return carry
        lax.fori_loop(0, seq // kt, kv_body, 0)

    q0 = pl.multiple_of(i * tq, tq)
    qn = _rms(zq_ref[...].astype(F32), gq_ref[...]).astype(BF16)
    q1 = _dot(qn, wq1_ref[...])
    half = QK_ROPE // 2
    lane = lax.broadcasted_iota(jnp.int32, q1.shape, 1) % HEAD_PAD
    q2 = jnp.where((lane >= QK_NOPE) & (lane < QK_NOPE + half), -pltpu.roll(q1, q1.shape[1] - half, 1),
                   jnp.where((lane >= QK_NOPE + half) & (lane < QK_NOPE + QK_ROPE), pltpu.roll(q1, half, 1), 0.0))
    cs = cos_ref[pl.ds(q0, tq), :]
    sn = sin_ref[pl.ds(q0, tq), :]
    scale = (QK_NOPE + QK_ROPE) ** -0.5 * LOG2_E

    kparts = [slice(j * (seq // ATTN_KEY_PARTS), (j + 1) * (seq // ATTN_KEY_PARTS)) for j in range(ATTN_KEY_PARTS)]

    def scores(h):
        sl = slice(h * HEAD_PAD, (h + 1) * HEAD_PAD)
        qh = ((q1[:, sl] * cs + q2[:, sl] * sn) * scale).astype(BF16)
        return [_dot_nt(k_s[kp, sl], qh) for kp in kparts]

    outs = []
    st_next = scores(0)
    for h in range(MLA_HEADS):
        st = st_next
        if h + 1 < MLA_HEADS:
            st_next = scores(h + 1)
        m = functools.reduce(jnp.maximum, [jnp.max(t, axis=0, keepdims=True) for t in st])
        ps = [jnp.exp2(t - m).astype(BF16) for t in st]
        ol = sum(_dot(vt_s[h * VT_ROWS:(h + 1) * VT_ROWS, kp], p) for kp, p in zip(kparts, ps))
        outs.append(ol[0:V_HEAD] / ol[V_HEAD:V_HEAD + 1])
    o_ref[...] = jnp.concatenate(outs, axis=0).T.astype(o_ref.dtype)


def _attn(z_kv, z_q, cos_t, sin_t, gq, gkv, wq1, wk, wvt):
    bsz, seq, _ = z_kv.shape
    tq = 256
    kw = MLA_HEADS * HEAD_PAD
    return pl.pallas_call(
        _attn_kernel,
        grid=(bsz, seq // tq),
        in_specs=[pl.BlockSpec((None, seq, ZKV_COLS), lambda b, i: (b, 0, 0)),
                  pl.BlockSpec((None, tq, Q_LORA), lambda b, i: (b, i, 0)),
                  pl.BlockSpec((None, seq, LANE), lambda b, i: (b, 0, 0)),
                  pl.BlockSpec((None, seq, LANE), lambda b, i: (b, 0, 0)),
                  _const_spec(gq.shape), _const_spec(gkv.shape), _const_spec(wq1.shape),
                  _const_spec(wk.shape), _const_spec(wvt.shape)],
        out_specs=pl.BlockSpec((None, tq, MLA_HEADS * V_HEAD), lambda b, i: (b, i, 0)),
        out_shape=jax.ShapeDtypeStruct((bsz, seq, MLA_HEADS * V_HEAD), BF16),
        scratch_shapes=[pltpu.VMEM((seq, kw), BF16), pltpu.VMEM((MLA_HEADS * VT_ROWS, seq), BF16)],
        compiler_params=pltpu.CompilerParams(dimension_semantics=("parallel", "arbitrary"),
                                             vmem_limit_bytes=VMEM_LIMIT),
        name="attn",
    )(z_kv, z_q, cos_t, sin_t, gq, gkv, wq1, wk, wvt)


def _mix_kernel(yrw_ref, ymla_ref, ga_ref, gb_ref, x_ref, mod_ref, wr_ref, wm_ref, wo_ref, ln_ref, o_ref):
    gate1 = mod_ref[2:3, :]
    br_rw = _dot(yrw_ref[...], wr_ref[...])
    br_mla = _dot(ymla_ref[...], wm_ref[...])
    mixed = (_sigmoid(ga_ref[...].astype(F32)) * br_rw
             + _sigmoid(gb_ref[...].astype(F32)) * br_mla)
    out = _dot(mixed.astype(BF16), wo_ref[...])
    t = DN_ALPHA * x_ref[...] + (1.0 + gate1) * out
    o_ref[...] = _layer_norm(t, ln_ref[0:1, :], ln_ref[1:2, :])


def _mix(y_rw, y_mla, z_g, x, mod3, wr, wm, wo, ln):
    bsz, seq, _ = x.shape
    tm = 512
    row = lambda n: pl.BlockSpec((None, tm, n), lambda b, i: (b, i, 0))
    return pl.pallas_call(
        _mix_kernel,
        grid=(bsz, seq // tm),
        in_specs=[row(RW_WIDTH), row(MLA_HEADS * V_HEAD),
                  pl.BlockSpec((None, tm, D_MODEL), lambda b, i: (b, i, 0)),
                  pl.BlockSpec((None, tm, D_MODEL), lambda b, i: (b, i, 1)),
                  row(D_MODEL),
                  pl.BlockSpec((None, 6, D_MODEL), lambda b, i: (b, 0, 0)),
                  _const_spec(wr.shape), _const_spec(wm.shape), _const_spec(wo.shape),
                  _const_spec(ln.shape)],
        out_specs=row(D_MODEL),
        out_shape=jax.ShapeDtypeStruct((bsz, seq, D_MODEL), F32),
        compiler_params=pltpu.CompilerParams(dimension_semantics=("parallel", "parallel"),
                                             vmem_limit_bytes=VMEM_LIMIT),
        name="mix",
    )(y_rw, y_mla, z_g, z_g, x, mod3, wr, wm, wo, ln)


def _ffn_kernel(x_ref, mod_ref, w1_ref, w2_ref, ln_ref, o_ref):
    shift = mod_ref[3:4, :]
    scale = mod_ref[4:5, :]
    gate2 = mod_ref[5:6, :]
    x1 = x_ref[...]
    h = (x1 * (1.0 + scale) + shift).astype(BF16)
    kc = 1024
    acc = jnp.zeros(x1.shape, F32)
    for c in range(D_FF // kc):
        u = jnp.maximum(_dot(h, w1_ref[:, c * kc:(c + 1) * kc]), 0.0)
        acc = acc + _dot((u * u).astype(BF16), w2_ref[c * kc:(c + 1) * kc, :])
    t = DN_ALPHA * x1 + (1.0 + gate2) * acc
    o_ref[...] = _layer_norm(t, ln_ref[0:1, :], ln_ref[1:2, :])


def _ffn(x1, mod3, w1, w2, ln):
    bsz, seq, _ = x1.shape
    tm = 512
    row = pl.BlockSpec((None, tm, D_MODEL), lambda b, i: (b, i, 0))
    return pl.pallas_call(
        _ffn_kernel,
        grid=(bsz, seq // tm),
        in_specs=[row, pl.BlockSpec((None, 6, D_MODEL), lambda b, i: (b, 0, 0)),
                  _const_spec(w1.shape), _const_spec(w2.shape), _const_spec(ln.shape)],
        out_specs=row,
        out_shape=jax.ShapeDtypeStruct((bsz, seq, D_MODEL), F32),
        compiler_params=pltpu.CompilerParams(dimension_semantics=("parallel", "parallel"),
                                             vmem_limit_bytes=VMEM_LIMIT),
        name="ffn",
    )(x1, mod3, w1, w2, ln)


def _pad_cols(w, left, total):
    return jnp.pad(w, ((0, 0), (left, total - left - w.shape[1])))


def _inproj_weight(w_in):
    o = 0
    w_rw = w_in[:, o:o + RW_COLS]; o += RW_COLS
    w_q = w_in[:, o:o + Q_LORA]; o += Q_LORA
    w_kv = w_in[:, o:o + KV_LORA]; o += KV_LORA
    w_kr = w_in[:, o:o + QK_ROPE]; o += QK_ROPE
    w_g = w_in[:, o:o + 2 * D_MODEL]
    half = QK_ROPE // 2
    w_kr_rot = jnp.concatenate([-w_kr[:, half:], w_kr[:, :half]], axis=1)
    w_all = jnp.concatenate([w_rw, w_kv, _pad_cols(w_kr, QK_NOPE, LANE), _pad_cols(w_kr_rot, QK_NOPE, LANE),
                             w_q, w_g], axis=1)
    return w_all.astype(BF16)


def _mla_weights(w_uq, w_ukv):
    half = QK_ROPE // 2
    q = w_uq.reshape(Q_LORA, MLA_HEADS, QK_NOPE + QK_ROPE)
    q_nope, q_1, q_2 = q[..., :QK_NOPE], q[..., QK_NOPE:QK_NOPE + half], q[..., QK_NOPE + half:]
    zpad = jnp.zeros((Q_LORA, MLA_HEADS, HEAD_PAD - QK_NOPE - QK_ROPE), F32)
    wq1 = jnp.concatenate([q_nope, q_1, q_2, zpad], axis=-1).reshape(Q_LORA, MLA_HEADS * HEAD_PAD)
    kv = w_ukv.reshape(KV_LORA, MLA_HEADS, QK_NOPE + V_HEAD)
    zhalf = jnp.zeros((KV_LORA, MLA_HEADS, HEAD_PAD - QK_NOPE), F32)
    wk = jnp.concatenate([kv[..., :QK_NOPE], zhalf], axis=-1).reshape(KV_LORA, MLA_HEADS * HEAD_PAD)
    wvt = kv[..., QK_NOPE:].reshape(KV_LORA, MLA_HEADS * V_HEAD).T
    return wq1.astype(BF16), wk.astype(BF16), wvt.astype(BF16)


def _scan_constants(rev):
    rows = np.arange(SCAN_ROWS)
    same = (rows[:, None] // CHUNK) == (rows[None, :] // CHUNK)
    tri = same & ((rows[None, :] >= rows[:, None]) if rev else (rows[None, :] <= rows[:, None]))
    ch = np.arange(RW_WIDTH)
    bd = (ch[:, None] // RW_HEAD) == (ch[None, :] // RW_HEAD)
    return jnp.asarray(tri, BF16), jnp.asarray(bd, BF16)


def _lora_pair(w, d):
    zero = jnp.zeros_like(w[0])
    return jnp.concatenate([zero, w[1]] if d else [w[0], zero], axis=0).astype(BF16)


def kernel(x, c, positions, w_ada, b_ada, w_in, rw_conv, rw_w0, rw_w2, rw_a0, rw_a2, rw_k_k, rw_k_a, rw_r_k, rw_g2, rw_lnx_g, rw_lnx_b, mla_q_norm_g, mla_kv_norm_g, mla_w_uq, mla_w_ukv, w_br_rwkv, w_br_mla, w_out, ln1_g, ln1_b, w_ff1, w_ff2, ln2_g, ln2_b):
    bsz, seq, _ = x.shape
    cos_t, sin_t = _rope_tables(positions)
    for l in range(DEPTH):
        mod3 = _ada(c, w_ada[l], b_ada[l]).reshape(bsz, 6, D_MODEL)
        z_rw, z_kv, z_q, z_g = _inproj(x, mod3, _inproj_weight(w_in[l]), rw_conv[l])

        zero = jnp.zeros((RW_WIDTH,), F32)
        y_rw = None
        for d in (0, 1):
            vecs = jnp.stack([rw_w0[l, d], rw_a0[l, d], rw_k_k[l], rw_k_a[l], rw_r_k[l],
                              rw_lnx_g[l], rw_lnx_b[l], zero])
            tri, bd = _scan_constants(bool(d))
            y_rw = _scan(z_rw, vecs, _lora_pair(rw_w2[l], d), _lora_pair(rw_a2[l], d), bd, tri,
                         rev=bool(d), y_fwd=y_rw, g2=rw_g2[l].astype(BF16))

        wq1, wk, wvt = _mla_weights(mla_w_uq[l], mla_w_ukv[l])
        y_mla = _attn(z_kv, z_q, cos_t, sin_t, mla_q_norm_g[l].reshape(1, Q_LORA),
                      mla_kv_norm_g[l].reshape(1, KV_LORA), wq1, wk, wvt)

        x = _mix(y_rw, y_mla, z_g, x, mod3, w_br_rwkv[l].astype(BF16), w_br_mla[l].astype(BF16),
                 w_out[l].astype(BF16), jnp.stack([ln1_g[l], ln1_b[l]]))
        x = _ffn(x, mod3, w_ff1[l].astype(BF16), w_ff2[l].astype(BF16), jnp.stack([ln2_g[l], ln2_b[l]]))
    return x
```

```python
import functools

import numpy as np
import jax
import jax.numpy as jnp
from jax import lax
from jax.experimental import pallas as pl
from jax.experimental.pallas import tpu as pltpu

F32 = jnp.float32
BF16 = jnp.bfloat16

D_MODEL = 1024
RW_HEADS = 8
RW_HEAD = 64
RW_WIDTH = RW_HEADS * RW_HEAD
DECAY_LORA = 64
AAA_LORA = 64
GATE_LORA = 128
RW_COLS = 3 * RW_WIDTH + 2 * DECAY_LORA + 2 * AAA_LORA + GATE_LORA
MLA_HEADS = 8
QK_NOPE = 64
QK_ROPE = 32
V_HEAD = 64
Q_LORA = 384
KV_LORA = 256
ROPE_THETA = 10000.0
D_FF = 4 * D_MODEL
LN_EPS = 1e-5
RMS_EPS = 1e-6
GN_EPS = 64e-5
L2_EPS = 1e-12
DEPTH = 1
DN_ALPHA = (2.0 * DEPTH) ** 0.25
LOG2_E = 1.4426950408889634
DECAY_SCALE = 0.6065306597126334

LANE = 128
CHUNK = 64
SCAN_ROWS = 1024
SCAN_WAVE = 4
X_HALO_ROWS = 8
HEAD_PAD = 128
ATTN_KEY_PARTS = 4
VT_ROWS = 80
ZKV_COLS = KV_LORA + 2 * LANE
VMEM_LIMIT = 56 * 1024 * 1024


def _dot(a, b):
    return jnp.dot(a, b, preferred_element_type=F32)


def _dot_nt(a, b):
    return lax.dot_general(a, b, (((1,), (1,)), ((), ())), preferred_element_type=F32)


def _dot_tn(a, b):
    return lax.dot_general(a, b, (((0,), (0,)), ((), ())), preferred_element_type=F32)


def _const_spec(shape):
    zeros = (0,) * len(shape)
    return pl.BlockSpec(shape, lambda *_: zeros)


def _layer_norm(t, g, b):
    mu = jnp.mean(t, axis=-1, keepdims=True)
    d = t - mu
    var = jnp.mean(d * d, axis=-1, keepdims=True)
    return d * lax.rsqrt(var + LN_EPS) * g + b


def _ada_kernel(c_ref, w_ref, b_ref, o_ref):
    c = c_ref[...]
    act = c * jax.nn.sigmoid(c)
    o_ref[...] = _dot(act.astype(BF16), w_ref[...].astype(BF16)) + b_ref[...]


def _ada(c, w, b):
    bsz = c.shape[0]
    n = w.shape[1]
    tn = 1536
    return pl.pallas_call(
        _ada_kernel,
        grid=(n // tn,),
        in_specs=[_const_spec((bsz, D_MODEL)),
                  pl.BlockSpec((D_MODEL, tn), lambda j: (0, j)),
                  pl.BlockSpec((1, tn), lambda j: (0, j))],
        out_specs=pl.BlockSpec((bsz, tn), lambda j: (0, j)),
        out_shape=jax.ShapeDtypeStruct((bsz, n), F32),
        name="ada",
    )(c, w, b.reshape(1, n))


def _rope_kernel(pos_ref, inv_ref, place_ref, one_ref, cos_ref, sin_ref):
    ang = inv_ref[...] * pos_ref[...].astype(F32)

    def table(t):
        hi = t.astype(BF16)
        lo = (t - hi.astype(F32)).astype(BF16)
        return _dot_tn(hi, place_ref[...]) + _dot_tn(lo, place_ref[...])

    cos_ref[...] = table(jnp.cos(ang)) + one_ref[...]
    sin_ref[...] = table(jnp.sin(ang))


def _rope_tables(positions):
    bsz, seq = positions.shape
    half = QK_ROPE // 2
    inv = (ROPE_THETA ** (-np.arange(half, dtype=np.float32) / half)).reshape(half, 1)
    place = np.zeros((half, LANE), np.float32)
    place[np.arange(half), QK_NOPE + np.arange(half)] = 1.0
    place[np.arange(half), QK_NOPE + half + np.arange(half)] = 1.0
    one = np.zeros((1, LANE), np.float32)
    one[0, :QK_NOPE] = 1.0
    spec = pl.BlockSpec((None, seq, LANE), lambda b: (b, 0, 0))
    return pl.pallas_call(
        _rope_kernel,
        grid=(bsz,),
        in_specs=[pl.BlockSpec((None, 1, seq), lambda b: (b, 0, 0)), _const_spec((half, 1)),
                  _const_spec((half, LANE)), _const_spec((1, LANE))],
        out_specs=[spec, spec],
        out_shape=[jax.ShapeDtypeStruct((bsz, seq, LANE), F32)] * 2,
        name="rope",
    )(positions.reshape(bsz, 1, seq), jnp.asarray(inv), jnp.asarray(place, BF16), jnp.asarray(one))


def _inproj_kernel(x_ref, xp_ref, xn_ref, mod_ref, w_ref, cw_ref, zrw_ref, zkv_ref, zq_ref, zg_ref):
    i = pl.program_id(1)
    nb = pl.num_programs(1)
    rows = x_ref.shape[0]
    shift = mod_ref[0:1, :]
    scale = mod_ref[1:2, :]
    h = (x_ref[...] * (1.0 + scale) + shift).astype(BF16)
    col = RW_COLS
    for o_ref in (zkv_ref, zq_ref, zg_ref):
        n = o_ref.shape[-1]
        o_ref[...] = _dot(h, w_ref[:, col:col + n]).astype(o_ref.dtype)
        col += n

    z = _dot(h, w_ref[:, 0:RW_COLS])
    x_halo = jnp.concatenate([xp_ref[...], xn_ref[...]], axis=0)
    z_halo = _dot((x_halo * (1.0 + scale) + shift).astype(BF16), w_ref[:, 0:RW_COLS])
    prev_row = z_halo[X_HALO_ROWS - 1:X_HALO_ROWS, :] * jnp.where(i > 0, 1.0, 0.0)
    next_row = z_halo[X_HALO_ROWS:X_HALO_ROWS + 1, :] * jnp.where(i < nb - 1, 1.0, 0.0)
    row_id = lax.broadcasted_iota(jnp.int32, (rows, 1), 0)
    z_dn = jnp.where(row_id == 0, prev_row, pltpu.roll(z, 1, 0))
    z_up = jnp.where(row_id == rows - 1, next_row, pltpu.roll(z, rows - 1, 0))
    zrw_ref[...] = (cw_ref[0:1, :] * z_dn + cw_ref[1:2, :] * z + cw_ref[2:3, :] * z_up).astype(zrw_ref.dtype)


def _inproj(x, mod3, w_all, conv_w):
    bsz, seq, _ = x.shape
    tm = 512
    hpb = tm // X_HALO_ROWS
    n_halo = seq // X_HALO_ROWS
    widths = (RW_COLS, ZKV_COLS, Q_LORA, 2 * D_MODEL)
    return pl.pallas_call(
        _inproj_kernel,
        grid=(bsz, seq // tm),
        in_specs=[pl.BlockSpec((None, tm, D_MODEL), lambda b, i: (b, i, 0)),
                  pl.BlockSpec((None, X_HALO_ROWS, D_MODEL), lambda b, i: (b, jnp.maximum(i * hpb - 1, 0), 0)),
                  pl.BlockSpec((None, X_HALO_ROWS, D_MODEL),
                               lambda b, i: (b, jnp.minimum((i + 1) * hpb, n_halo - 1), 0)),
                  pl.BlockSpec((None, 6, D_MODEL), lambda b, i: (b, 0, 0)),
                  _const_spec(w_all.shape), _const_spec(conv_w.shape)],
        out_specs=[pl.BlockSpec((None, tm, n), lambda b, i: (b, i, 0)) for n in widths],
        out_shape=[jax.ShapeDtypeStruct((bsz, seq, n), BF16) for n in widths],
        compiler_params=pltpu.CompilerParams(dimension_semantics=("parallel", "parallel"),
                                             vmem_limit_bytes=VMEM_LIMIT),
        name="inproj",
    )(x, x, x, mod3, w_all, conv_w)


def _sigmoid(x):
    return 0.5 + 0.5 * jnp.tanh(0.5 * x)


def _scan_kernel(*refs, rev):
    if rev:
        (zm_ref, vec_ref, w2_ref, a2_ref, bd_ref, tri_ref,
         yf_ref, g2_ref, o_ref,
         at_s, rt_s, bh_s, kh_s, be_s, ke_s, v_s, vsw_s, eg_s, h_s, ya_s, yb_s) = refs
    else:
        (zm_ref, vec_ref, w2_ref, a2_ref, bd_ref, tri_ref,
         o_ref,
         at_s, rt_s, bh_s, kh_s, be_s, ke_s, v_s, vsw_s, eg_s, h_s, ya_s, yb_s) = refs
    rows = zm_ref.shape[0]
    n_chunks = rows // CHUNK
    i = pl.program_id(1)

    @pl.when(i == 0)
    def _():
        h_s[...] = jnp.zeros_like(h_s)

    def token_prep(wave):
        cs = sorted(wave)
        n_r = len(cs) * CHUNK
        rs = slice(cs[0] * CHUNK, cs[0] * CHUNK + n_r)
        zc = zm_ref[rs, :].astype(F32)
        r = zc[:, 0:RW_WIDTH]
        k = zc[:, RW_WIDTH:2 * RW_WIDTH]
        v = zc[:, 2 * RW_WIDTH:3 * RW_WIDTH]
        zw = zc[:, 3 * RW_WIDTH:3 * RW_WIDTH + LANE]
        za = zc[:, 3 * RW_WIDTH + LANE:3 * RW_WIDTH + 2 * LANE]
        w0 = vec_ref[0:1, :]
        a0 = vec_ref[1:2, :]
        k_k = vec_ref[2:3, :]
        k_a = vec_ref[3:4, :]
        v_s[rs, :] = v
        vsw_s[rs, :] = pltpu.roll(v, RW_HEAD, 1)
        w_lin = w0 + _dot(jnp.tanh(zw).astype(BF16), w2_ref[...])
        rate_lin = a0 + _dot(za.astype(BF16), a2_ref[...])
        kkv = k * k_k
        ssq = _dot((kkv * kkv).astype(BF16), bd_ref[...])
        yield
        lw = -DECAY_SCALE * _sigmoid(w_lin)
        lw_hi = lw.astype(BF16)
        lw_lo = (lw - lw_hi.astype(F32)).astype(BF16)
        tri = tri_ref[0:n_r, 0:n_r]
        g_in = _dot(tri, lw_hi) + _dot(tri, lw_lo)
        yield
        rate = _sigmoid(rate_lin)
        kk = kkv * lax.rsqrt(jnp.maximum(ssq, L2_EPS * L2_EPS))
        kd = k * (1.0 + (rate - 1.0) * k_a)
        av = -kk
        bv = kk * rate
        yield
        g_ex = g_in - lw
        last = 0 if rev else CHUNK - 1
        g_tot_rows = [g_in[j * CHUNK + last:j * CHUNK + last + 1, :] for j in range(len(cs))]
        eg_rows = [jnp.exp(t) for t in g_tot_rows]
        for j, c in enumerate(cs):
            eg_s[c] = jnp.broadcast_to(eg_rows[j], (8, RW_WIDTH))
        e_neg = jnp.exp(-g_in)
        e_end = e_neg * jnp.concatenate([jnp.broadcast_to(t, (CHUNK, RW_WIDTH)) for t in eg_rows], axis=0)
        at_s[rs, :] = av * jnp.exp(g_ex)
        rt_s[rs, :] = r * jnp.exp(g_in)
        yield
        bh_s[rs, :] = bv * e_neg
        kh_s[rs, :] = kd * e_neg
        be_s[rs, :] = bv * e_end
        ke_s[rs, :] = kd * e_end

    ri = lax.broadcasted_iota(jnp.int32, (CHUNK, CHUNK), 0)
    ci = lax.broadcasted_iota(jnp.int32, (CHUNK, CHUNK), 1)
    m_strict = (ci > ri) if rev else (ci < ri)
    m_incl = (ci >= ri) if rev else (ci <= ri)
    eye = jnp.where(ri == ci, 1.0, 0.0).astype(F32)
    lower = lax.broadcasted_iota(jnp.int32, (CHUNK, LANE), 1) < RW_HEAD
    eye_up = jnp.where(lax.broadcasted_iota(jnp.int32, (CHUNK, LANE), 1)
                       == lax.broadcasted_iota(jnp.int32, (CHUNK, LANE), 0) + RW_HEAD, 1.0, 0.0).astype(F32)
    zeros_half = jnp.zeros((CHUNK, RW_HEAD), F32)

    def pad(t):
        return jnp.concatenate([t, zeros_half], axis=1)

    heads = range(RW_HEADS)
    chunk_order = list(range(n_chunks - 1, -1, -1)) if rev else list(range(n_chunks))
    waves = [chunk_order[w:w + SCAN_WAVE] for w in range(0, n_chunks, SCAN_WAVE)]

    def ld(ref, item):
        c, h = item
        return ref[c * CHUNK:(c + 1) * CHUNK, h * RW_HEAD:(h + 1) * RW_HEAD]

    def ld_v_upper(item):
        c, h = item
        src = v_s if h % 2 else vsw_s
        blk = src[c * CHUNK:(c + 1) * CHUNK, (h // 2) * LANE:(h // 2 + 1) * LANE]
        return jnp.where(lower, 0.0, blk).astype(BF16)

    def ld_decay(item):
        c, h = item
        return eg_s[c][0:1, h * RW_HEAD:(h + 1) * RW_HEAD]

    def chunk_local(items, out):
        n_it = range(len(items))
        rt = [ld(rt_s, it) for it in items]
        atb = [ld(at_s, it).astype(BF16) for it in items]
        lhs = [jnp.concatenate([atb[i], rt[i].astype(BF16)], axis=0) for i in n_it]
        a_b = [_dot_nt(lhs[i], ld(bh_s, items[i]).astype(BF16)) for i in n_it]
        a_k = [_dot_nt(lhs[i], ld(kh_s, items[i]).astype(BF16)) for i in n_it]
        yield
        a_ab =[jnp.where(m_strict, a_b[i][0:CHUNK], 0.0) for i in n_it]
        a_ak = [jnp.where(m_strict, a_k[i][0:CHUNK], 0.0).astype(BF16) for i in n_it]
        a_rb = [jnp.where(m_incl, a_b[i][CHUNK:], 0.0).astype(BF16) for i in n_it]
        a_rk = [jnp.where(m_incl, a_k[i][CHUNK:], 0.0).astype(BF16) for i in n_it]
        v_up = [ld_v_upper(it) for it in items]
        akv = [_dot(a_ak[i], v_up[i]) for i in n_it]
        yield

        z = [pad(a_ab[i]) + eye_up for i in n_it]
        for _ in range(6):
            z = [_dot(z[i][:, 0:CHUNK].astype(BF16), z[i].astype(BF16)) + jnp.where(lower, 0.0, z[i])
                 for i in n_it]
            yield
        tb =[z[i][:, CHUNK:].astype(BF16) for i in n_it]

        w_m = [(pad(ld(at_s, items[i])) + akv[i]).astype(BF16) for i in n_it]
        pq = [_dot(tb[i], w_m[i]).astype(BF16) for i in n_it]
        yield
        ryq =[_dot(a_rb[i], pq[i]) for i in n_it]
        ry = [(rt[i] + ryq[i][:, 0:RW_HEAD]).astype(BF16) for i in n_it]
        y0 = [jnp.where(lower, 0.0, ryq[i]) + _dot(a_rk[i], v_up[i]) for i in n_it]
        yield
        be =[ld(be_s, it).astype(BF16) for it in items]
        ke = [ld(ke_s, it).astype(BF16) for it in items]
        mq = [_dot_tn(be[i], pq[i]) for i in n_it]
        m_m = [(mq[i][:, 0:RW_HEAD] + eye * ld_decay(items[i])).astype(BF16) for i in n_it]
        h0 = [jnp.where(lower, 0.0, mq[i]) + _dot_tn(ke[i], v_up[i]) for i in n_it]
        ry_m = [jnp.concatenate([ry[i], m_m[i]], axis=0) for i in n_it]
        out.update(ry_m=ry_m, y0=y0, h0=h0)

    hst = [h_s[h] for h in heads]

    def carried(wave, res):
        for ci, c in enumerate(wave):
            for h in heads:
                i = ci * RW_HEADS + h
                yh = _dot(res["ry_m"][i], hst[h].astype(BF16))
                y_dst = ya_s if h % 2 else yb_s
                y_dst[c * CHUNK:(c + 1) * CHUNK, (h // 2) * LANE:(h // 2 + 1) * LANE] = yh[0:CHUNK] + res["y0"][i]
                hst[h] = yh[CHUNK:] + res["h0"][i]
                if h % 4 == 3:
                    yield

    def emit(*gens):
        live = list(gens)
        while live:
            for g in list(live):
                if next(g, StopIteration) is StopIteration:
                    live.remove(g)

    results = [dict() for _ in waves]
    emit(token_prep(waves[0]))
    for w, wave in enumerate(waves):
        gens = [chunk_local([(c, h) for c in wave for h in heads], results[w])]
        if w + 1 < len(waves):
            gens.append(token_prep(waves[w + 1]))
        if w > 0:
            gens.append(carried(waves[w - 1], results[w - 1]))
        emit(*gens)
    emit(carried(waves[-1], results[-1]))
    for h in heads:
        h_s[h] = hst[h]

    y_dir = ya_s[...] + pltpu.roll(yb_s[...], RW_WIDTH - RW_HEAD, 1)
    if not rev:
        o_ref[...] = y_dir
    else:
        r_k = vec_ref[4:5, :]
        lnx_g = vec_ref[5:6, :]
        lnx_b = vec_ref[6:7, :]
        zc = zm_ref[...].astype(F32)
        r = zc[:, 0:RW_WIDTH]
        k = zc[:, RW_WIDTH:2 * RW_WIDTH]
        v = zc[:, 2 * RW_WIDTH:3 * RW_WIDTH]
        y = yf_ref[...] + y_dir
        inv_n = 1.0 / RW_HEAD
        mu = _dot(y.astype(BF16), bd_ref[...]) * inv_n
        d = y - mu
        var = _dot((d * d).astype(BF16), bd_ref[...]) * inv_n
        yn = d * lax.rsqrt(var + GN_EPS) * lnx_g + lnx_b
        rk = r * k * r_k
        rk_hi = rk.astype(BF16)
        rk_lo = (rk - rk_hi.astype(F32)).astype(BF16)
        bonus = (_dot(rk_hi, bd_ref[...]) + _dot(rk_lo, bd_ref[...])) * v
        zg = zc[:, 3 * RW_WIDTH + 2 * LANE:RW_COLS]
        gate = _dot(_sigmoid(zg).astype(BF16), g2_ref[...])
        o_ref[...] = ((yn + bonus) * gate).astype(o_ref.dtype)


def _scan(z_rw, vecs, w2p, a2p, bd, tri, rev, y_fwd=None, g2=None):
    bsz, seq, _ = z_rw.shape
    rows = SCAN_ROWS
    nb = seq // rows
    n_chunks = rows // CHUNK

    def blk(i):
        return nb - 1 - i if rev else i

    in_specs = [
        pl.BlockSpec((None, rows, RW_COLS), lambda b, i: (b, blk(i), 0)),
        _const_spec(vecs.shape), _const_spec(w2p.shape),
        _const_spec(a2p.shape), _const_spec(bd.shape), _const_spec(tri.shape),
    ]
    args = [z_rw, vecs, w2p, a2p, bd, tri]
    scratch = [pltpu.VMEM((rows, RW_WIDTH), F32) for _ in range(8)]
    scratch += [pltpu.VMEM((n_chunks, 8, RW_WIDTH), F32),
                pltpu.VMEM((RW_HEADS, RW_HEAD, LANE), F32),
                pltpu.VMEM((rows, RW_WIDTH), F32), pltpu.VMEM((rows, RW_WIDTH), F32)]
    if rev:
        in_specs += [pl.BlockSpec((None, rows, RW_WIDTH), lambda b, i: (b, blk(i), 0)),
                     _const_spec(g2.shape)]
        args += [y_fwd, g2]
        out_dtype = BF16
    else:
        out_dtype = F32
    return pl.pallas_call(
        functools.partial(_scan_kernel, rev=rev),
        grid=(bsz, nb),
        in_specs=in_specs,
        out_specs=pl.BlockSpec((None, rows, RW_WIDTH), lambda b, i: (b, blk(i), 0)),
        out_shape=jax.ShapeDtypeStruct((bsz, seq, RW_WIDTH), out_dtype),
        scratch_shapes=scratch,
        compiler_params=pltpu.CompilerParams(dimension_semantics=("parallel", "arbitrary"),
                                             vmem_limit_bytes=VMEM_LIMIT),
        name="scan_bwd" if rev else "scan_fwd",
    )(*args)


def _rms(xf, g):
    return xf * lax.rsqrt(jnp.mean(xf * xf, axis=-1, keepdims=True) + RMS_EPS) * g


def _attn_kernel(zkv_ref, zq_ref, cos_ref, sin_ref, gq_ref, gkv_ref, wq1_ref, wk_ref, wvt_ref,
                 o_ref, k_s, vt_s):
    seq = zkv_ref.shape[0]
    tq = zq_ref.shape[0]
    kt = 256
    i = pl.program_id(1)

    @pl.when(i == 0)
    def _():
        def kv_body(t, carry):
            r0 = pl.multiple_of(t * kt, kt)
            zk = zkv_ref[pl.ds(r0, kt), :].astype(F32)
            kvn = _rms(zk[:, 0:KV_LORA], gkv_ref[...]).astype(BF16)
            kn = _dot(kvn, wk_ref[...])
            cs = cos_ref[pl.ds(r0, kt), :]
            sn = sin_ref[pl.ds(r0, kt), :]
            kpe = zk[:, KV_LORA:KV_LORA + LANE] * cs + zk[:, KV_LORA + LANE:KV_LORA + 2 * LANE] * sn
            for h in range(MLA_HEADS):
                sl = slice(h * HEAD_PAD, (h + 1) * HEAD_PAD)
                k_s[pl.ds(r0, kt), sl] = (kn[:, sl] + kpe).astype(BF16)
            vt = _dot_nt(wvt_ref[...], kvn).astype(BF16)
            ones = jnp.ones((VT_ROWS - V_HEAD, kt), BF16)
            for h in range(MLA_HEADS):
                vt_s[h * VT_ROWS:(h + 1) * VT_ROWS, pl.ds(r0, kt)] = jnp.concatenate(
                    [vt[h * V_HEAD:(h + 1) * V_HEAD], ones], axis=0)
            return carry
        lax.fori_loop(0, seq // kt, kv_body, 0)

    q0 = pl.multiple_of(i * tq, tq)
    qn = _rms(zq_ref[...].astype(F32), gq_ref[...]).astype(BF16)
    q1 = _dot(qn, wq1_ref[...])
    half = QK_ROPE // 2
    lane = lax.broadcasted_iota(jnp.int32, q1.shape, 1) % HEAD_PAD
    q2 = jnp.where((lane >= QK_NOPE) & (lane < QK_NOPE + half), -pltpu.roll(q1, q1.shape[1] - half, 1),
                   jnp.where((lane >= QK_NOPE + half) & (lane < QK_NOPE + QK_ROPE), pltpu.roll(q1, half, 1), 0.0))
    cs = cos_ref[pl.ds(q0, tq), :]
    sn = sin_ref[pl.ds(q0, tq), :]
    scale = (QK_NOPE + QK_ROPE) ** -0.5 * LOG2_E

    kparts = [slice(j * (seq // ATTN_KEY_PARTS), (j + 1) * (seq // ATTN_KEY_PARTS)) for j in range(ATTN_KEY_PARTS)]

    def scores(h):
        sl = slice(h * HEAD_PAD, (h + 1) * HEAD_PAD)
        qh = ((q1[:, sl] * cs + q2[:, sl] * sn) * scale).astype(BF16)
        return [_dot_nt(k_s[kp, sl], qh) for kp in kparts]

    outs = []
    st_next = scores(0)
    for h in range(MLA_HEADS):
        st = st_next
        if h + 1 < MLA_HEADS:
            st_next = scores(h + 1)
        m = functools.reduce(jnp.maximum, [jnp.max(t, axis=0, keepdims=True) for t in st])
        ps = [jnp.exp2(t - m).astype(BF16) for t in st]
        ol = sum(_dot(vt_s[h * VT_ROWS:(h + 1) * VT_ROWS, kp], p) for kp, p in zip(kparts, ps))
        outs.append(ol[0:V_HEAD] / ol[V_HEAD:V_HEAD + 1])
    o_ref[...] = jnp.concatenate(outs, axis=0).T.astype(o_ref.dtype)


def _attn(z_kv, z_q, cos_t, sin_t, gq, gkv, wq1, wk, wvt):
    bsz, seq, _ = z_kv.shape
    tq = 512
    kw = MLA_HEADS * HEAD_PAD
    return pl.pallas_call(
        _attn_kernel,
        grid=(bsz, seq // tq),
        in_specs=[pl.BlockSpec((None, seq, ZKV_COLS), lambda b, i: (b, 0, 0)),
                  pl.BlockSpec((None, tq, Q_LORA), lambda b, i: (b, i, 0)),
                  pl.BlockSpec((None, seq, LANE), lambda b, i: (b, 0, 0)),
                  pl.BlockSpec((None, seq, LANE), lambda b, i: (b, 0, 0)),
                  _const_spec(gq.shape), _const_spec(gkv.shape), _const_spec(wq1.shape),
                  _const_spec(wk.shape), _const_spec(wvt.shape)],
        out_specs=pl.BlockSpec((None, tq, MLA_HEADS * V_HEAD), lambda b, i: (b, i, 0)),
        out_shape=jax.ShapeDtypeStruct((bsz, seq, MLA_HEADS * V_HEAD), BF16),
        scratch_shapes=[pltpu.VMEM((seq, kw), BF16), pltpu.VMEM((MLA_HEADS * VT_ROWS, seq), BF16)],
        compiler_params=pltpu.CompilerParams(dimension_semantics=("parallel", "arbitrary"),
                                             vmem_limit_bytes=VMEM_LIMIT),
        name="attn",
    )(z_kv, z_q, cos_t, sin_t, gq, gkv, wq1, wk, wvt)


def _mix_kernel(yrw_ref, ymla_ref, ga_ref, gb_ref, x_ref, mod_ref, wr_ref, wm_ref, wo_ref, ln_ref, o_ref):
    gate1 = mod_ref[2:3, :]
    br_rw = _dot(yrw_ref[...], wr_ref[...])
    br_mla = _dot(ymla_ref[...], wm_ref[...])
    mixed = (_sigmoid(ga_ref[...].astype(F32)) * br_rw
             + _sigmoid(gb_ref[...].astype(F32)) * br_mla)
    out = _dot(mixed.astype(BF16), wo_ref[...])
    t = DN_ALPHA * x_ref[...] + (1.0 + gate1) * out
    o_ref[...] = _layer_norm(t, ln_ref[0:1, :], ln_ref[1:2, :])


def _mix(y_rw, y_mla, z_g, x, mod3, wr, wm, wo, ln):
    bsz, seq, _ = x.shape
    tm = 512
    row = lambda n: pl.BlockSpec((None, tm, n), lambda b, i: (b, i, 0))
    return pl.pallas_call(
        _mix_kernel,
        grid=(bsz, seq // tm),
        in_specs=[row(RW_WIDTH), row(MLA_HEADS * V_HEAD),
                  pl.BlockSpec((None, tm, D_MODEL), lambda b, i: (b, i, 0)),
                  pl.BlockSpec((None, tm, D_MODEL), lambda b, i: (b, i, 1)),
                  row(D_MODEL),
                  pl.BlockSpec((None, 6, D_MODEL), lambda b, i: (b, 0, 0)),
                  _const_spec(wr.shape), _const_spec(wm.shape), _const_spec(wo.shape),
                  _const_spec(ln.shape)],
        out_specs=row(D_MODEL),
        out_shape=jax.ShapeDtypeStruct((bsz, seq, D_MODEL), F32),
        compiler_params=pltpu.CompilerParams(dimension_semantics=("parallel", "parallel"),
                                             vmem_limit_bytes=VMEM_LIMIT),
        name="mix",
    )(y_rw, y_mla, z_g, z_g, x, mod3, wr, wm, wo, ln)


def _ffn_kernel(x_ref, mod_ref, w1_ref, w2_ref, ln_ref, o_ref):
    shift = mod_ref[3:4, :]
    scale = mod_ref[4:5, :]
    gate2 = mod_ref[5:6, :]
    x1 = x_ref[...]
    h = (x1 * (1.0 + scale) + shift).astype(BF16)
    kc = 1024
    acc = jnp.zeros(x1.shape, F32)
    for c in range(D_FF // kc):
        u = jnp.maximum(_dot(h, w1_ref[:, c * kc:(c + 1) * kc]), 0.0)
        acc = acc + _dot((u * u).astype(BF16), w2_ref[c * kc:(c + 1) * kc, :])
    t = DN_ALPHA * x1 + (1.0 + gate2) * acc
    o_ref[...] = _layer_norm(t, ln_ref[0:1, :], ln_ref[1:2, :])


def _ffn(x1, mod3, w1, w2, ln):
    bsz, seq, _ = x1.shape
    tm = 512
    row = pl.BlockSpec((None, tm, D_MODEL), lambda b, i: (b, i, 0))
    return pl.pallas_call(
        _ffn_kernel,
        grid=(bsz, seq // tm),
        in_specs=[row, pl.BlockSpec((None, 6, D_MODEL), lambda b, i: (b, 0, 0)),
                  _const_spec(w1.shape), _const_spec(w2.shape), _const_spec(ln.shape)],
        out_specs=row,
        out_shape=jax.ShapeDtypeStruct((bsz, seq, D_MODEL), F32),
        compiler_params=pltpu.CompilerParams(dimension_semantics=("parallel", "parallel"),
                                             vmem_limit_bytes=VMEM_LIMIT),
        name="ffn",
    )(x1, mod3, w1, w2, ln)


def _pad_cols(w, left, total):
    return jnp.pad(w, ((0, 0), (left, total - left - w.shape[1])))


def _inproj_weight(w_in):
    o = 0
    w_rw = w_in[:, o:o + RW_COLS]; o += RW_COLS
    w_q = w_in[:, o:o + Q_LORA]; o += Q_LORA
    w_kv = w_in[:, o:o + KV_LORA]; o += KV_LORA
    w_kr = w_in[:, o:o + QK_ROPE]; o += QK_ROPE
    w_g = w_in[:, o:o + 2 * D_MODEL]
    half = QK_ROPE // 2
    w_kr_rot = jnp.concatenate([-w_kr[:, half:], w_kr[:, :half]], axis=1)
    w_all = jnp.concatenate([w_rw, w_kv, _pad_cols(w_kr, QK_NOPE, LANE), _pad_cols(w_kr_rot, QK_NOPE, LANE),
                             w_q, w_g], axis=1)
    return w_all.astype(BF16)


def _mla_weights(w_uq, w_ukv):
    half = QK_ROPE // 2
    q = w_uq.reshape(Q_LORA, MLA_HEADS, QK_NOPE + QK_ROPE)
    q_nope, q_1, q_2 = q[..., :QK_NOPE], q[..., QK_NOPE:QK_NOPE + half], q[..., QK_NOPE + half:]
    zpad = jnp.zeros((Q_LORA, MLA_HEADS, HEAD_PAD - QK_NOPE - QK_ROPE), F32)
    wq1 = jnp.concatenate([q_nope, q_1, q_2, zpad], axis=-1).reshape(Q_LORA, MLA_HEADS * HEAD_PAD)
    kv = w_ukv.reshape(KV_LORA, MLA_HEADS, QK_NOPE + V_HEAD)
    zhalf = jnp.zeros((KV_LORA, MLA_HEADS, HEAD_PAD - QK_NOPE), F32)
    wk = jnp.concatenate([kv[..., :QK_NOPE], zhalf], axis=-1).reshape(KV_LORA, MLA_HEADS * HEAD_PAD)
    wvt = kv[..., QK_NOPE:].reshape(KV_LORA, MLA_HEADS * V_HEAD).T
    return wq1.astype(BF16), wk.astype(BF16), wvt.astype(BF16)


def _scan_constants(rev):
    rows = np.arange(SCAN_ROWS)
    same = (rows[:, None] // CHUNK) == (rows[None, :] // CHUNK)
    tri = same & ((rows[None, :] >= rows[:, None]) if rev else (rows[None, :] <= rows[:, None]))
    ch = np.arange(RW_WIDTH)
    bd = (ch[:, None] // RW_HEAD) == (ch[None, :] // RW_HEAD)
    return jnp.asarray(tri, BF16), jnp.asarray(bd, BF16)


def _lora_pair(w, d):
    zero = jnp.zeros_like(w[0])
    return jnp.concatenate([zero, w[1]] if d else [w[0], zero], axis=0).astype(BF16)


def kernel(x, c, positions, w_ada, b_ada, w_in, rw_conv, rw_w0, rw_w2, rw_a0, rw_a2, rw_k_k, rw_k_a, rw_r_k, rw_g2, rw_lnx_g, rw_lnx_b, mla_q_norm_g, mla_kv_norm_g, mla_w_uq, mla_w_ukv, w_br_rwkv, w_br_mla, w_out, ln1_g, ln1_b, w_ff1, w_ff2, ln2_g, ln2_b):
    bsz, seq, _ = x.shape
    cos_t, sin_t = _rope_tables(positions)
    for l in range(DEPTH):
        mod3 = _ada(c, w_ada[l], b_ada[l]).reshape(bsz, 6, D_MODEL)
        z_rw, z_kv, z_q, z_g = _inproj(x, mod3, _inproj_weight(w_in[l]), rw_conv[l])

        zero = jnp.zeros((RW_WIDTH,), F32)
        y_rw = None
        for d in (0, 1):
            vecs = jnp.stack([rw_w0[l, d], rw_a0[l, d], rw_k_k[l], rw_k_a[l], rw_r_k[l],
                              rw_lnx_g[l], rw_lnx_b[l], zero])
            tri, bd = _scan_constants(bool(d))
            y_rw = _scan(z_rw, vecs, _lora_pair(rw_w2[l], d), _lora_pair(rw_a2[l], d), bd, tri,
                         rev=bool(d), y_fwd=y_rw, g2=rw_g2[l].astype(BF16))

        wq1, wk, wvt = _mla_weights(mla_w_uq[l], mla_w_ukv[l])
        y_mla = _attn(z_kv, z_q, cos_t, sin_t, mla_q_norm_g[l].reshape(1, Q_LORA),
                      mla_kv_norm_g[l].reshape(1, KV_LORA), wq1, wk, wvt)

        x = _mix(y_rw, y_mla, z_g, x, mod3, w_br_rwkv[l].astype(BF16), w_br_mla[l].astype(BF16),
                 w_out[l].astype(BF16), jnp.stack([ln1_g[l], ln1_b[l]]))
        x = _ffn(x, mod3, w_ff1[l].astype(BF16), w_ff2[l].astype(BF16), jnp.stack([ln2_g[l], ln2_b[l]]))
    return x
```

```python
import functools

import numpy as np
import jax
import jax.numpy as jnp
from jax import lax
from jax.experimental import pallas as pl
from jax.experimental.pallas import tpu as pltpu

F32 = jnp.float32
BF16 = jnp.bfloat16

D_MODEL = 1024
RW_HEADS = 8
RW_HEAD = 64
RW_WIDTH = RW_HEADS * RW_HEAD
DECAY_LORA = 64
AAA_LORA = 64
GATE_LORA = 128
RW_COLS = 3 * RW_WIDTH + 2 * DECAY_LORA + 2 * AAA_LORA + GATE_LORA
MLA_HEADS = 8
QK_NOPE = 64
QK_ROPE = 32
V_HEAD = 64
Q_LORA = 384
KV_LORA = 256
ROPE_THETA = 10000.0
D_FF = 4 * D_MODEL
LN_EPS = 1e-5
RMS_EPS = 1e-6
GN_EPS = 64e-5
L2_EPS = 1e-12
DEPTH = 1
DN_ALPHA = (2.0 * DEPTH) ** 0.25
LOG2_E = 1.4426950408889634
DECAY_SCALE = 0.6065306597126334

LANE = 128
CHUNK = 64
SCAN_ROWS = 1024
SCAN_WAVE = 4
X_HALO_ROWS = 8
HEAD_PAD = 128
ATTN_KEY_PARTS = 4
VT_ROWS = 80
ZKV_COLS = KV_LORA + 2 * LANE
VMEM_LIMIT = 56 * 1024 * 1024


def _dot(a, b):
    return jnp.dot(a, b, preferred_element_type=F32)


def _dot_nt(a, b):
    return lax.dot_general(a, b, (((1,), (1,)), ((), ())), preferred_element_type=F32)


def _dot_tn(a, b):
    return lax.dot_general(a, b, (((0,), (0,)), ((), ())), preferred_element_type=F32)


def _const_spec(shape):
    zeros = (0,) * len(shape)
    return pl.BlockSpec(shape, lambda *_: zeros)


def _layer_norm(t, g, b):
    mu = jnp.mean(t, axis=-1, keepdims=True)
    d = t - mu
    var = jnp.mean(d * d, axis=-1, keepdims=True)
    return d * lax.rsqrt(var + LN_EPS) * g + b


def _ada_kernel(c_ref, w_ref, b_ref, o_ref):
    c = c_ref[...]
    act = c * jax.nn.sigmoid(c)
    o_ref[...] = _dot(act.astype(BF16), w_ref[...].astype(BF16)) + b_ref[...]


def _ada(c, w, b):
    bsz = c.shape[0]
    n = w.shape[1]
    tn = 1536
    return pl.pallas_call(
        _ada_kernel,
        grid=(n // tn,),
        in_specs=[_const_spec((bsz, D_MODEL)),
                  pl.BlockSpec((D_MODEL, tn), lambda j: (0, j)),
                  pl.BlockSpec((1, tn), lambda j: (0, j))],
        out_specs=pl.BlockSpec((bsz, tn), lambda j: (0, j)),
        out_shape=jax.ShapeDtypeStruct((bsz, n), F32),
        name="ada",
    )(c, w, b.reshape(1, n))


def _rope_kernel(pos_ref, inv_ref, place_ref, one_ref, cos_ref, sin_ref):
    ang = inv_ref[...] * pos_ref[...].astype(F32)

    def table(t):
        hi = t.astype(BF16)
        lo = (t - hi.astype(F32)).astype(BF16)
        return _dot_tn(hi, place_ref[...]) + _dot_tn(lo, place_ref[...])

    cos_ref[...] = table(jnp.cos(ang)) + one_ref[...]
    sin_ref[...] = table(jnp.sin(ang))


def _rope_tables(positions):
    bsz, seq = positions.shape
    half = QK_ROPE // 2
    inv = (ROPE_THETA ** (-np.arange(half, dtype=np.float32) / half)).reshape(half, 1)
    place = np.zeros((half, LANE), np.float32)
    place[np.arange(half), QK_NOPE + np.arange(half)] = 1.0
    place[np.arange(half), QK_NOPE + half + np.arange(half)] = 1.0
    one = np.zeros((1, LANE), np.float32)
    one[0, :QK_NOPE] = 1.0
    spec = pl.BlockSpec((None, seq, LANE), lambda b: (b, 0, 0))
    return pl.pallas_call(
        _rope_kernel,
        grid=(bsz,),
        in_specs=[pl.BlockSpec((None, 1, seq), lambda b: (b, 0, 0)), _const_spec((half, 1)),
                  _const_spec((half, LANE)), _const_spec((1, LANE))],
        out_specs=[spec, spec],
        out_shape=[jax.ShapeDtypeStruct((bsz, seq, LANE), F32)] * 2,
        name="rope",
    )(positions.reshape(bsz, 1, seq), jnp.asarray(inv), jnp.asarray(place, BF16), jnp.asarray(one))


def _inproj_kernel(x_ref, xp_ref, xn_ref, mod_ref, w_ref, cw_ref, zrw_ref, zkv_ref, zq_ref, zg_ref):
    i = pl.program_id(1)
    nb = pl.num_programs(1)
    rows = x_ref.shape[0]
    shift = mod_ref[0:1, :]
    scale = mod_ref[1:2, :]
    h = (x_ref[...] * (1.0 + scale) + shift).astype(BF16)
    col = RW_COLS
    for o_ref in (zkv_ref, zq_ref, zg_ref):
        n = o_ref.shape[-1]
        o_ref[...] = _dot(h, w_ref[:, col:col + n]).astype(o_ref.dtype)
        col += n

    z = _dot(h, w_ref[:, 0:RW_COLS])
    x_halo = jnp.concatenate([xp_ref[...], xn_ref[...]], axis=0)
    z_halo = _dot((x_halo * (1.0 + scale) + shift).astype(BF16), w_ref[:, 0:RW_COLS])
    prev_row = z_halo[X_HALO_ROWS - 1:X_HALO_ROWS, :] * jnp.where(i > 0, 1.0, 0.0)
    next_row = z_halo[X_HALO_ROWS:X_HALO_ROWS + 1, :] * jnp.where(i < nb - 1, 1.0, 0.0)
    row_id = lax.broadcasted_iota(jnp.int32, (rows, 1), 0)
    z_dn = jnp.where(row_id == 0, prev_row, pltpu.roll(z, 1, 0))
    z_up = jnp.where(row_id == rows - 1, next_row, pltpu.roll(z, rows - 1, 0))
    zrw_ref[...] = (cw_ref[0:1, :] * z_dn + cw_ref[1:2, :] * z + cw_ref[2:3, :] * z_up).astype(zrw_ref.dtype)


def _inproj(x, mod3, w_all, conv_w):
    bsz, seq, _ = x.shape
    tm = 512
    hpb = tm // X_HALO_ROWS
    n_halo = seq // X_HALO_ROWS
    widths = (RW_COLS, ZKV_COLS, Q_LORA, 2 * D_MODEL)
    return pl.pallas_call(
        _inproj_kernel,
        grid=(bsz, seq // tm),
        in_specs=[pl.BlockSpec((None, tm, D_MODEL), lambda b, i: (b, i, 0)),
                  pl.BlockSpec((None, X_HALO_ROWS, D_MODEL), lambda b, i: (b, jnp.maximum(i * hpb - 1, 0), 0)),
                  pl.BlockSpec((None, X_HALO_ROWS, D_MODEL),
                               lambda b, i: (b, jnp.minimum((i + 1) * hpb, n_halo - 1), 0)),
                  pl.BlockSpec((None, 6, D_MODEL), lambda b, i: (b, 0, 0)),
                  _const_spec(w_all.shape), _const_spec(conv_w.shape)],
        out_specs=[pl.BlockSpec((None, tm, n), lambda b, i: (b, i, 0)) for n in widths],
        out_shape=[jax.ShapeDtypeStruct((bsz, seq, n), BF16) for n in widths],
        compiler_params=pltpu.CompilerParams(dimension_semantics=("parallel", "parallel"),
                                             vmem_limit_bytes=VMEM_LIMIT),
        name="inproj",
    )(x, x, x, mod3, w_all, conv_w)


def _sigmoid(x):
    return 0.5 + 0.5 * jnp.tanh(0.5 * x)


def _head_sums(x, bd_ref):
    return jnp.concatenate([_dot(x[:, p * LANE:(p + 1) * LANE], bd_ref[...])
                            for p in range(RW_WIDTH // LANE)], axis=1)


def _scan_kernel(*refs, rev):
    if rev:
        (zm_ref, vec_ref, w2_ref, a2_ref, bd_ref, tri_ref,
         yf_ref, g2_ref, o_ref,
         at_s, rt_s, bh_s, kh_s, be_s, ke_s, v_s, vsw_s, eg_s, h_s, ya_s, yb_s) = refs
    else:
        (zm_ref, vec_ref, w2_ref, a2_ref, bd_ref, tri_ref,
         o_ref,
         at_s, rt_s, bh_s, kh_s, be_s, ke_s, v_s, vsw_s, eg_s, h_s, ya_s, yb_s) = refs
    rows = zm_ref.shape[0]
    n_chunks = rows // CHUNK
    i = pl.program_id(1)

    @pl.when(i == 0)
    def _():
        h_s[...] = jnp.zeros_like(h_s)

    def token_prep(wave):
        cs = sorted(wave)
        n_r = len(cs) * CHUNK
        rs = slice(cs[0] * CHUNK, cs[0] * CHUNK + n_r)
        zc = zm_ref[rs, :].astype(F32)
        r = zc[:, 0:RW_WIDTH]
        k = zc[:, RW_WIDTH:2 * RW_WIDTH]
        v = zc[:, 2 * RW_WIDTH:3 * RW_WIDTH]
        zw = zc[:, 3 * RW_WIDTH:3 * RW_WIDTH + LANE]
        za = zc[:, 3 * RW_WIDTH + LANE:3 * RW_WIDTH + 2 * LANE]
        w0 = vec_ref[0:1, :]
        a0 = vec_ref[1:2, :]
        k_k = vec_ref[2:3, :]
        k_a = vec_ref[3:4, :]
        v_s[rs, :] = v
        vsw_s[rs, :] = pltpu.roll(v, RW_HEAD, 1)
        w_lin = w0 + _dot(jnp.tanh(zw).astype(BF16), w2_ref[...])
        rate_lin = a0 + _dot(za.astype(BF16), a2_ref[...])
        kkv = k * k_k
        ssq = _head_sums((kkv * kkv).astype(BF16), bd_ref)
        yield
        lw = -DECAY_SCALE * _sigmoid(w_lin)
        lw_hi = lw.astype(BF16)
        lw_lo = (lw - lw_hi.astype(F32)).astype(BF16)
        tri = tri_ref[...]
        g_in = jnp.concatenate([_dot(tri, lw_hi[j * CHUNK:(j + 1) * CHUNK]) + _dot(tri, lw_lo[j * CHUNK:(j + 1) * CHUNK])
                                for j in range(len(cs))], axis=0)
        yield
        rate = _sigmoid(rate_lin)
        kk = kkv * lax.rsqrt(jnp.maximum(ssq, L2_EPS * L2_EPS))
        kd = k * (1.0 + (rate - 1.0) * k_a)
        av = -kk
        bv = kk * rate
        yield
        g_ex = g_in - lw
        last = 0 if rev else CHUNK - 1
        g_tot_rows = [g_in[j * CHUNK + last:j * CHUNK + last + 1, :] for j in range(len(cs))]
        eg_rows = [jnp.exp(t) for t in g_tot_rows]
        for j, c in enumerate(cs):
            eg_s[c] = jnp.broadcast_to(eg_rows[j], (8, RW_WIDTH))
        e_neg = jnp.exp(-g_in)
        e_end = e_neg * jnp.concatenate([jnp.broadcast_to(t, (CHUNK, RW_WIDTH)) for t in eg_rows], axis=0)
        at_s[rs, :] = av * jnp.exp(g_ex)
        rt_s[rs, :] = r * jnp.exp(g_in)
        yield
        bh_s[rs, :] = bv * e_neg
        kh_s[rs, :] = kd * e_neg
        be_s[rs, :] = bv * e_end
        ke_s[rs, :] = kd * e_end

    ri = lax.broadcasted_iota(jnp.int32, (CHUNK, CHUNK), 0)
    ci = lax.broadcasted_iota(jnp.int32, (CHUNK, CHUNK), 1)
    m_strict = (ci > ri) if rev else (ci < ri)
    m_incl = (ci >= ri) if rev else (ci <= ri)
    eye = jnp.where(ri == ci, 1.0, 0.0).astype(F32)
    lower = lax.broadcasted_iota(jnp.int32, (CHUNK, LANE), 1) < RW_HEAD
    eye_up = jnp.where(lax.broadcasted_iota(jnp.int32, (CHUNK, LANE), 1)
                       == lax.broadcasted_iota(jnp.int32, (CHUNK, LANE), 0) + RW_HEAD, 1.0, 0.0).astype(F32)
    zeros_half = jnp.zeros((CHUNK, RW_HEAD), F32)

    def pad(t):
        return jnp.concatenate([t, zeros_half], axis=1)

    heads = range(RW_HEADS)
    chunk_order = list(range(n_chunks - 1, -1, -1)) if rev else list(range(n_chunks))
    waves = [chunk_order[w:w + SCAN_WAVE] for w in range(0, n_chunks, SCAN_WAVE)]

    def ld(ref, item):
        c, h = item
        return ref[c * CHUNK:(c + 1) * CHUNK, h * RW_HEAD:(h + 1) * RW_HEAD]

    def ld_v_upper(item):
        c, h = item
        src = v_s if h % 2 else vsw_s
        blk = src[c * CHUNK:(c + 1) * CHUNK, (h // 2) * LANE:(h // 2 + 1) * LANE]
        return jnp.where(lower, 0.0, blk).astype(BF16)

    def ld_decay(item):
        c, h = item
        return eg_s[c][0:1, h * RW_HEAD:(h + 1) * RW_HEAD]

    def chunk_local(items, out):
        n_it = range(len(items))
        rt = [ld(rt_s, it) for it in items]
        atb = [ld(at_s, it).astype(BF16) for it in items]
        lhs = [jnp.concatenate([atb[i], rt[i].astype(BF16)], axis=0) for i in n_it]
        a_b = [_dot_nt(lhs[i], ld(bh_s, items[i]).astype(BF16)) for i in n_it]
        a_k = [_dot_nt(lhs[i], ld(kh_s, items[i]).astype(BF16)) for i in n_it]
        yield
        a_ab =[jnp.where(m_strict, a_b[i][0:CHUNK], 0.0) for i in n_it]
        a_ak = [jnp.where(m_strict, a_k[i][0:CHUNK], 0.0).astype(BF16) for i in n_it]
        a_rb = [jnp.where(m_incl, a_b[i][CHUNK:], 0.0).astype(BF16) for i in n_it]
        a_rk = [jnp.where(m_incl, a_k[i][CHUNK:], 0.0).astype(BF16) for i in n_it]
        v_up = [ld_v_upper(it) for it in items]
        akv = [_dot(a_ak[i], v_up[i]) for i in n_it]
        yield

        z = [pad(a_ab[i]) + eye_up for i in n_it]
        for _ in range(6):
            z = [_dot(z[i][:, 0:CHUNK].astype(BF16), z[i].astype(BF16)) + jnp.where(lower, 0.0, z[i])
                 for i in n_it]
            yield
        tb =[z[i][:, CHUNK:].astype(BF16) for i in n_it]

        w_m = [(pad(ld(at_s, items[i])) + akv[i]).astype(BF16) for i in n_it]
        pq = [_dot(tb[i], w_m[i]).astype(BF16) for i in n_it]
        yield
        ryq =[_dot(a_rb[i], pq[i]) for i in n_it]
        ry = [(rt[i] + ryq[i][:, 0:RW_HEAD]).astype(BF16) for i in n_it]
        y0 = [jnp.where(lower, 0.0, ryq[i]) + _dot(a_rk[i], v_up[i]) for i in n_it]
        yield
        be =[ld(be_s, it).astype(BF16) for it in items]
        ke = [ld(ke_s, it).astype(BF16) for it in items]
        mq = [_dot_tn(be[i], pq[i]) for i in n_it]
        m_m = [(mq[i][:, 0:RW_HEAD] + eye * ld_decay(items[i])).astype(BF16) for i in n_it]
        h0 = [jnp.where(lower, 0.0, mq[i]) + _dot_tn(ke[i], v_up[i]) for i in n_it]
        ry_m = [jnp.concatenate([ry[i], m_m[i]], axis=0) for i in n_it]
        out.update(ry_m=ry_m, y0=y0, h0=h0)

    hst = [h_s[h] for h in heads]

    def carried(wave, res):
        for ci, c in enumerate(wave):
            for h in heads:
                i = ci * RW_HEADS + h
                yh = _dot(res["ry_m"][i], hst[h].astype(BF16))
                y_dst = ya_s if h % 2 else yb_s
                y_dst[c * CHUNK:(c + 1) * CHUNK, (h // 2) * LANE:(h // 2 + 1) * LANE] = yh[0:CHUNK] + res["y0"][i]
                hst[h] = yh[CHUNK:] + res["h0"][i]
                if h % 4 == 3:
                    yield

    def emit(*gens):
        live = list(gens)
        while live:
            for g in list(live):
                if next(g, StopIteration) is StopIteration:
                    live.remove(g)

    results = [dict() for _ in waves]
    emit(token_prep(waves[0]))
    for w, wave in enumerate(waves):
        gens = [chunk_local([(c, h) for c in wave for h in heads], results[w])]
        if w + 1 < len(waves):
            gens.append(token_prep(waves[w + 1]))
        if w > 0:
            gens.append(carried(waves[w - 1], results[w - 1]))
        emit(*gens)
    emit(carried(waves[-1], results[-1]))
    for h in heads:
        h_s[h] = hst[h]

    y_dir = ya_s[...] + pltpu.roll(yb_s[...], RW_WIDTH - RW_HEAD, 1)
    if not rev:
        o_ref[...] = y_dir
    else:
        r_k = vec_ref[4:5, :]
        lnx_g = vec_ref[5:6, :]
        lnx_b = vec_ref[6:7, :]
        zc = zm_ref[...].astype(F32)
        r = zc[:, 0:RW_WIDTH]
        k = zc[:, RW_WIDTH:2 * RW_WIDTH]
        v = zc[:, 2 * RW_WIDTH:3 * RW_WIDTH]
        y = yf_ref[...] + y_dir
        inv_n = 1.0 / RW_HEAD
        mu = _head_sums(y.astype(BF16), bd_ref) * inv_n
        d = y - mu
        var = _head_sums((d * d).astype(BF16), bd_ref) * inv_n
        yn = d * lax.rsqrt(var + GN_EPS) * lnx_g + lnx_b
        rk = r * k * r_k
        rk_hi = rk.astype(BF16)
        rk_lo = (rk - rk_hi.astype(F32)).astype(BF16)
        bonus = (_head_sums(rk_hi, bd_ref) + _head_sums(rk_lo, bd_ref)) * v
        zg = zc[:, 3 * RW_WIDTH + 2 * LANE:RW_COLS]
        gate = _dot(_sigmoid(zg).astype(BF16), g2_ref[...])
        o_ref[...] = ((yn + bonus) * gate).astype(o_ref.dtype)


def _scan(z_rw, vecs, w2p, a2p, bd, tri, rev, y_fwd=None, g2=None):
    bsz, seq, _ = z_rw.shape
    rows = SCAN_ROWS
    nb = seq // rows
    n_chunks = rows // CHUNK

    def blk(i):
        return nb - 1 - i if rev else i

    in_specs = [
        pl.BlockSpec((None, rows, RW_COLS), lambda b, i: (b, blk(i), 0)),
        _const_spec(vecs.shape), _const_spec(w2p.shape),
        _const_spec(a2p.shape), _const_spec(bd.shape), _const_spec(tri.shape),
    ]
    args = [z_rw, vecs, w2p, a2p, bd, tri]
    scratch = [pltpu.VMEM((rows, RW_WIDTH), F32) for _ in range(8)]
    scratch += [pltpu.VMEM((n_chunks, 8, RW_WIDTH), F32),
                pltpu.VMEM((RW_HEADS, RW_HEAD, LANE), F32),
                pltpu.VMEM((rows, RW_WIDTH), F32), pltpu.VMEM((rows, RW_WIDTH), F32)]
    if rev:
        in_specs += [pl.BlockSpec((None, rows, RW_WIDTH), lambda b, i: (b, blk(i), 0)),
                     _const_spec(g2.shape)]
        args += [y_fwd, g2]
        out_dtype = BF16
    else:
        out_dtype = F32
    return pl.pallas_call(
        functools.partial(_scan_kernel, rev=rev),
        grid=(bsz, nb),
        in_specs=in_specs,
        out_specs=pl.BlockSpec((None, rows, RW_WIDTH), lambda b, i: (b, blk(i), 0)),
        out_shape=jax.ShapeDtypeStruct((bsz, seq, RW_WIDTH), out_dtype),
        scratch_shapes=scratch,
        compiler_params=pltpu.CompilerParams(dimension_semantics=("parallel", "arbitrary"),
                                             vmem_limit_bytes=VMEM_LIMIT),
        name="scan_bwd" if rev else "scan_fwd",
    )(*args)


def _rms(xf, g):
    return xf * lax.rsqrt(jnp.mean(xf * xf, axis=-1, keepdims=True) + RMS_EPS) * g


def _attn_kernel(zkv_ref, zq_ref, cos_ref, sin_ref, gq_ref, gkv_ref, wq1_ref, wk_ref, wvt_ref,
                 o_ref, k_s, vt_s):
    seq = zkv_ref.shape[0]
    tq = zq_ref.shape[0]
    kt = 256
    i = pl.program_id(1)

    @pl.when(i == 0)
    def _():
        def kv_body(t, carry):
            r0 = pl.multiple_of(t * kt, kt)
            zk = zkv_ref[pl.ds(r0, kt), :].astype(F32)
            kvn = _rms(zk[:, 0:KV_LORA], gkv_ref[...]).astype(BF16)
            kn = _dot(kvn, wk_ref[...])
            cs = cos_ref[pl.ds(r0, kt), :]
            sn = sin_ref[pl.ds(r0, kt), :]
            kpe = zk[:, KV_LORA:KV_LORA + LANE] * cs + zk[:, KV_LORA + LANE:KV_LORA + 2 * LANE] * sn
            for h in range(MLA_HEADS):
                sl = slice(h * HEAD_PAD, (h + 1) * HEAD_PAD)
                k_s[pl.ds(r0, kt), sl] = (kn[:, sl] + kpe).astype(BF16)
            vt = _dot_nt(wvt_ref[...], kvn).astype(BF16)
            ones = jnp.ones((VT_ROWS - V_HEAD, kt), BF16)
            for h in range(MLA_HEADS):
                vt_s[h * VT_ROWS:(h + 1) * VT_ROWS, pl.ds(r0, kt)] = jnp.concatenate(
                    [vt[h * V_HEAD:(h + 1) * V_HEAD], ones], axis=0)
            return carry
        lax.fori_loop(0, seq // kt, kv_body, 0)

    q0 = pl.multiple_of(i * tq, tq)
    qn = _rms(zq_ref[...].astype(F32), gq_ref[...]).astype(BF16)
    q1 = _dot(qn, wq1_ref[...])
    half = QK_ROPE // 2
    lane = lax.broadcasted_iota(jnp.int32, q1.shape, 1) % HEAD_PAD
    q2 = jnp.where((lane >= QK_NOPE) & (lane < QK_NOPE + half), -pltpu.roll(q1, q1.shape[1] - half, 1),
                   jnp.where((lane >= QK_NOPE + half) & (lane < QK_NOPE + QK_ROPE), pltpu.roll(q1, half, 1), 0.0))
    cs = cos_ref[pl.ds(q0, tq), :]
    sn = sin_ref[pl.ds(q0, tq), :]
    scale = (QK_NOPE + QK_ROPE) ** -0.5 * LOG2_E

    kparts = [slice(j * (seq // ATTN_KEY_PARTS), (j + 1) * (seq // ATTN_KEY_PARTS)) for j in range(ATTN_KEY_PARTS)]

    def scores(h):
        sl = slice(h * HEAD_PAD, (h + 1) * HEAD_PAD)
        qh = ((q1[:, sl] * cs + q2[:, sl] * sn) * scale).astype(BF16)
        return [_dot_nt(k_s[kp, sl], qh) for kp in kparts]

    outs = []
    st_next = scores(0)
    for h in range(MLA_HEADS):
        st = st_next
        if h + 1 < MLA_HEADS:
            st_next = scores(h + 1)
        m = functools.reduce(jnp.maximum, [jnp.max(t, axis=0, keepdims=True) for t in st])
        ps = [jnp.exp2(t - m).astype(BF16) for t in st]
        ol = sum(_dot(vt_s[h * VT_ROWS:(h + 1) * VT_ROWS, kp], p) for kp, p in zip(kparts, ps))
        outs.append(ol[0:V_HEAD] / ol[V_HEAD:V_HEAD + 1])
    o_ref[...] = jnp.concatenate(outs, axis=0).T.astype(o_ref.dtype)


def _attn(z_kv, z_q, cos_t, sin_t, gq, gkv, wq1, wk, wvt):
    bsz, seq, _ = z_kv.shape
    tq = 512
    kw = MLA_HEADS * HEAD_PAD
    return pl.pallas_call(
        _attn_kernel,
        grid=(bsz, seq // tq),
        in_specs=[pl.BlockSpec((None, seq, ZKV_COLS), lambda b, i: (b, 0, 0)),
                  pl.BlockSpec((None, tq, Q_LORA), lambda b, i: (b, i, 0)),
                  pl.BlockSpec((None, seq, LANE), lambda b, i: (b, 0, 0)),
                  pl.BlockSpec((None, seq, LANE), lambda b, i: (b, 0, 0)),
                  _const_spec(gq.shape), _const_spec(gkv.shape), _const_spec(wq1.shape),
                  _const_spec(wk.shape), _const_spec(wvt.shape)],
        out_specs=pl.BlockSpec((None, tq, MLA_HEADS * V_HEAD), lambda b, i: (b, i, 0)),
        out_shape=jax.ShapeDtypeStruct((bsz, seq, MLA_HEADS * V_HEAD), BF16),
        scratch_shapes=[pltpu.VMEM((seq, kw), BF16), pltpu.VMEM((MLA_HEADS * VT_ROWS, seq), BF16)],
        compiler_params=pltpu.CompilerParams(dimension_semantics=("parallel", "arbitrary"),
                                             vmem_limit_bytes=VMEM_LIMIT),
        name="attn",
    )(z_kv, z_q, cos_t, sin_t, gq, gkv, wq1, wk, wvt)


def _mix_kernel(yrw_ref, ymla_ref, ga_ref, gb_ref, x_ref, mod_ref, wr_ref, wm_ref, wo_ref, ln_ref, o_ref):
    gate1 = mod_ref[2:3, :]
    br_rw = _dot(yrw_ref[...], wr_ref[...])
    br_mla = _dot(ymla_ref[...], wm_ref[...])
    mixed = (_sigmoid(ga_ref[...].astype(F32)) * br_rw
             + _sigmoid(gb_ref[...].astype(F32)) * br_mla)
    out = _dot(mixed.astype(BF16), wo_ref[...])
    t = DN_ALPHA * x_ref[...] + (1.0 + gate1) * out
    o_ref[...] = _layer_norm(t, ln_ref[0:1, :], ln_ref[1:2, :])


def _mix(y_rw, y_mla, z_g, x, mod3, wr, wm, wo, ln):
    bsz, seq, _ = x.shape
    tm = 512
    row = lambda n: pl.BlockSpec((None, tm, n), lambda b, i: (b, i, 0))
    return pl.pallas_call(
        _mix_kernel,
        grid=(bsz, seq // tm),
        in_specs=[row(RW_WIDTH), row(MLA_HEADS * V_HEAD),
                  pl.BlockSpec((None, tm, D_MODEL), lambda b, i: (b, i, 0)),
                  pl.BlockSpec((None, tm, D_MODEL), lambda b, i: (b, i, 1)),
                  row(D_MODEL),
                  pl.BlockSpec((None, 6, D_MODEL), lambda b, i: (b, 0, 0)),
                  _const_spec(wr.shape), _const_spec(wm.shape), _const_spec(wo.shape),
                  _const_spec(ln.shape)],
        out_specs=row(D_MODEL),
        out_shape=jax.ShapeDtypeStruct((bsz, seq, D_MODEL), F32),
        compiler_params=pltpu.CompilerParams(dimension_semantics=("parallel", "parallel"),
                                             vmem_limit_bytes=VMEM_LIMIT),
        name="mix",
    )(y_rw, y_mla, z_g, z_g, x, mod3, wr, wm, wo, ln)


def _ffn_kernel(x_ref, mod_ref, w1_ref, w2_ref, ln_ref, o_ref):
    shift = mod_ref[3:4, :]
    scale = mod_ref[4:5, :]
    gate2 = mod_ref[5:6, :]
    x1 = x_ref[...]
    h = (x1 * (1.0 + scale) + shift).astype(BF16)
    kc = 1024
    acc = jnp.zeros(x1.shape, F32)
    for c in range(D_FF // kc):
        u = jnp.maximum(_dot(h, w1_ref[:, c * kc:(c + 1) * kc]), 0.0)
        acc = acc + _dot((u * u).astype(BF16), w2_ref[c * kc:(c + 1) * kc, :])
    t = DN_ALPHA * x1 + (1.0 + gate2) * acc
    o_ref[...] = _layer_norm(t, ln_ref[0:1, :], ln_ref[1:2, :])


def _ffn(x1, mod3, w1, w2, ln):
    bsz, seq, _ = x1.shape
    tm = 512
    row = pl.BlockSpec((None, tm, D_MODEL), lambda b, i: (b, i, 0))
    return pl.pallas_call(
        _ffn_kernel,
        grid=(bsz, seq // tm),
        in_specs=[row, pl.BlockSpec((None, 6, D_MODEL), lambda b, i: (b, 0, 0)),
                  _const_spec(w1.shape), _const_spec(w2.shape), _const_spec(ln.shape)],
        out_specs=row,
        out_shape=jax.ShapeDtypeStruct((bsz, seq, D_MODEL), F32),
        compiler_params=pltpu.CompilerParams(dimension_semantics=("parallel", "parallel"),
                                             vmem_limit_bytes=VMEM_LIMIT),
        name="ffn",
    )(x1, mod3, w1, w2, ln)


def _pad_cols(w, left, total):
    return jnp.pad(w, ((0, 0), (left, total - left - w.shape[1])))


def _inproj_weight(w_in):
    o = 0
    w_rw = w_in[:, o:o + RW_COLS]; o += RW_COLS
    w_q = w_in[:, o:o + Q_LORA]; o += Q_LORA
    w_kv = w_in[:, o:o + KV_LORA]; o += KV_LORA
    w_kr = w_in[:, o:o + QK_ROPE]; o += QK_ROPE
    w_g = w_in[:, o:o + 2 * D_MODEL]
    half = QK_ROPE // 2
    w_kr_rot = jnp.concatenate([-w_kr[:, half:], w_kr[:, :half]], axis=1)
    w_all = jnp.concatenate([w_rw, w_kv, _pad_cols(w_kr, QK_NOPE, LANE), _pad_cols(w_kr_rot, QK_NOPE, LANE),
                             w_q, w_g], axis=1)
    return w_all.astype(BF16)


def _mla_weights(w_uq, w_ukv):
    half = QK_ROPE // 2
    q = w_uq.reshape(Q_LORA, MLA_HEADS, QK_NOPE + QK_ROPE)
    q_nope, q_1, q_2 = q[..., :QK_NOPE], q[..., QK_NOPE:QK_NOPE + half], q[..., QK_NOPE + half:]
    zpad = jnp.zeros((Q_LORA, MLA_HEADS, HEAD_PAD - QK_NOPE - QK_ROPE), F32)
    wq1 = jnp.concatenate([q_nope, q_1, q_2, zpad], axis=-1).reshape(Q_LORA, MLA_HEADS * HEAD_PAD)
    kv = w_ukv.reshape(KV_LORA, MLA_HEADS, QK_NOPE + V_HEAD)
    zhalf = jnp.zeros((KV_LORA, MLA_HEADS, HEAD_PAD - QK_NOPE), F32)
    wk = jnp.concatenate([kv[..., :QK_NOPE], zhalf], axis=-1).reshape(KV_LORA, MLA_HEADS * HEAD_PAD)
    wvt = kv[..., QK_NOPE:].reshape(KV_LORA, MLA_HEADS * V_HEAD).T
    return wq1.astype(BF16), wk.astype(BF16), wvt.astype(BF16)


def _scan_constants(rev):
    rows = np.arange(CHUNK)
    tri = (rows[None, :] >= rows[:, None]) if rev else (rows[None, :] <= rows[:, None])
    ch = np.arange(LANE)
    bd = (ch[:, None] // RW_HEAD) == (ch[None, :] // RW_HEAD)
    return jnp.asarray(tri, BF16), jnp.asarray(bd, BF16)


def _lora_pair(w, d):
    zero = jnp.zeros_like(w[0])
    return jnp.concatenate([zero, w[1]] if d else [w[0], zero], axis=0).astype(BF16)


def kernel(x, c, positions, w_ada, b_ada, w_in, rw_conv, rw_w0, rw_w2, rw_a0, rw_a2, rw_k_k, rw_k_a, rw_r_k, rw_g2, rw_lnx_g, rw_lnx_b, mla_q_norm_g, mla_kv_norm_g, mla_w_uq, mla_w_ukv, w_br_rwkv, w_br_mla, w_out, ln1_g, ln1_b, w_ff1, w_ff2, ln2_g, ln2_b):
    bsz, seq, _ = x.shape
    cos_t, sin_t = _rope_tables(positions)
    for l in range(DEPTH):
        mod3 = _ada(c, w_ada[l], b_ada[l]).reshape(bsz, 6, D_MODEL)
        z_rw, z_kv, z_q, z_g = _inproj(x, mod3, _inproj_weight(w_in[l]), rw_conv[l])

        zero = jnp.zeros((RW_WIDTH,), F32)
        y_rw = None
        for d in (0, 1):
            vecs = jnp.stack([rw_w0[l, d], rw_a0[l, d], rw_k_k[l], rw_k_a[l], rw_r_k[l],
                              rw_lnx_g[l], rw_lnx_b[l], zero])
            tri, bd = _scan_constants(bool(d))
            y_rw = _scan(z_rw, vecs, _lora_pair(rw_w2[l], d), _lora_pair(rw_a2[l], d), bd, tri,
                         rev=bool(d), y_fwd=y_rw, g2=rw_g2[l].astype(BF16))

        wq1, wk, wvt = _mla_weights(mla_w_uq[l], mla_w_ukv[l])
        y_mla = _attn(z_kv, z_q, cos_t, sin_t, mla_q_norm_g[l].reshape(1, Q_LORA),
                      mla_kv_norm_g[l].reshape(1, KV_LORA), wq1, wk, wvt)

        x = _mix(y_rw, y_mla, z_g, x, mod3, w_br_rwkv[l].astype(BF16), w_br_mla[l].astype(BF16),
                 w_out[l].astype(BF16), jnp.stack([ln1_g[l], ln1_b[l]]))
        x = _ffn(x, mod3, w_ff1[l].astype(BF16), w_ff2[l].astype(BF16), jnp.stack([ln2_g[l], ln2_b[l]]))
    return x
```

```python
import functools

import numpy as np
import jax
import jax.numpy as jnp
from jax import lax
from jax.experimental import pallas as pl
from jax.experimental.pallas import tpu as pltpu

F32 = jnp.float32
BF16 = jnp.bfloat16

D_MODEL = 1024
RW_HEADS = 8
RW_HEAD = 64
RW_WIDTH = RW_HEADS * RW_HEAD
DECAY_LORA = 64
AAA_LORA = 64
GATE_LORA = 128
RW_COLS = 3 * RW_WIDTH + 2 * DECAY_LORA + 2 * AAA_LORA + GATE_LORA
MLA_HEADS = 8
QK_NOPE = 64
QK_ROPE = 32
V_HEAD = 64
Q_LORA = 384
KV_LORA = 256
ROPE_THETA = 10000.0
D_FF = 4 * D_MODEL
LN_EPS = 1e-5
RMS_EPS = 1e-6
GN_EPS = 64e-5
L2_EPS = 1e-12
DEPTH = 1
DN_ALPHA = (2.0 * DEPTH) ** 0.25
LOG2_E = 1.4426950408889634
DECAY_SCALE = 0.6065306597126334

LANE = 128
CHUNK = 64
SCAN_ROWS = 1024
SCAN_WAVE = 4
X_HALO_ROWS = 8
HEAD_PAD = 128
ATTN_KEY_PARTS = 4
VT_ROWS = 80
ZKV_COLS = KV_LORA + 2 * LANE
VMEM_LIMIT = 56 * 1024 * 1024


def _dot(a, b):
    return jnp.dot(a, b, preferred_element_type=F32)


def _dot_nt(a, b):
    return lax.dot_general(a, b, (((1,), (1,)), ((), ())), preferred_element_type=F32)


def _dot_tn(a, b):
    return lax.dot_general(a, b, (((0,), (0,)), ((), ())), preferred_element_type=F32)


def _const_spec(shape):
    zeros = (0,) * len(shape)
    return pl.BlockSpec(shape, lambda *_: zeros)


def _layer_norm(t, g, b):
    mu = jnp.mean(t, axis=-1, keepdims=True)
    d = t - mu
    var = jnp.mean(d * d, axis=-1, keepdims=True)
    return d * lax.rsqrt(var + LN_EPS) * g + b


def _ada_kernel(c_ref, w_ref, b_ref, o_ref):
    c = c_ref[...]
    act = c * jax.nn.sigmoid(c)
    o_ref[...] = _dot(act.astype(BF16), w_ref[...].astype(BF16)) + b_ref[...]


def _ada(c, w, b):
    bsz = c.shape[0]
    n = w.shape[1]
    tn = 1536
    return pl.pallas_call(
        _ada_kernel,
        grid=(n // tn,),
        in_specs=[_const_spec((bsz, D_MODEL)),
                  pl.BlockSpec((D_MODEL, tn), lambda j: (0, j)),
                  pl.BlockSpec((1, tn), lambda j: (0, j))],
        out_specs=pl.BlockSpec((bsz, tn), lambda j: (0, j)),
        out_shape=jax.ShapeDtypeStruct((bsz, n), F32),
        name="ada",
    )(c, w, b.reshape(1, n))


def _rope_kernel(pos_ref, inv_ref, place_ref, one_ref, cos_ref, sin_ref):
    ang = inv_ref[...] * pos_ref[...].astype(F32)

    def table(t):
        hi = t.astype(BF16)
        lo = (t - hi.astype(F32)).astype(BF16)
        return _dot_tn(hi, place_ref[...]) + _dot_tn(lo, place_ref[...])

    cos_ref[...] = table(jnp.cos(ang)) + one_ref[...]
    sin_ref[...] = table(jnp.sin(ang))


def _rope_tables(positions):
    bsz, seq = positions.shape
    half = QK_ROPE // 2
    inv = (ROPE_THETA ** (-np.arange(half, dtype=np.float32) / half)).reshape(half, 1)
    place = np.zeros((half, LANE), np.float32)
    place[np.arange(half), QK_NOPE + np.arange(half)] = 1.0
    place[np.arange(half), QK_NOPE + half + np.arange(half)] = 1.0
    one = np.zeros((1, LANE), np.float32)
    one[0, :QK_NOPE] = 1.0
    spec = pl.BlockSpec((None, seq, LANE), lambda b: (b, 0, 0))
    return pl.pallas_call(
        _rope_kernel,
        grid=(bsz,),
        in_specs=[pl.BlockSpec((None, 1, seq), lambda b: (b, 0, 0)), _const_spec((half, 1)),
                  _const_spec((half, LANE)), _const_spec((1, LANE))],
        out_specs=[spec, spec],
        out_shape=[jax.ShapeDtypeStruct((bsz, seq, LANE), F32)] * 2,
        name="rope",
    )(positions.reshape(bsz, 1, seq), jnp.asarray(inv), jnp.asarray(place, BF16), jnp.asarray(one))


def _inproj_kernel(x_ref, xp_ref, xn_ref, mod_ref, w_ref, cw_ref, zrw_ref, zkv_ref, zq_ref, zg_ref):
    i = pl.program_id(1)
    nb = pl.num_programs(1)
    rows = x_ref.shape[0]
    shift = mod_ref[0:1, :]
    scale = mod_ref[1:2, :]
    h = (x_ref[...] * (1.0 + scale) + shift).astype(BF16)
    col = RW_COLS
    for o_ref in (zkv_ref, zq_ref, zg_ref):
        n = o_ref.shape[-1]
        o_ref[...] = _dot(h, w_ref[:, col:col + n]).astype(o_ref.dtype)
        col += n

    z = _dot(h, w_ref[:, 0:RW_COLS])
    x_halo = jnp.concatenate([xp_ref[...], xn_ref[...]], axis=0)
    z_halo = _dot((x_halo * (1.0 + scale) + shift).astype(BF16), w_ref[:, 0:RW_COLS])
    prev_row = z_halo[X_HALO_ROWS - 1:X_HALO_ROWS, :] * jnp.where(i > 0, 1.0, 0.0)
    next_row = z_halo[X_HALO_ROWS:X_HALO_ROWS + 1, :] * jnp.where(i < nb - 1, 1.0, 0.0)
    row_id = lax.broadcasted_iota(jnp.int32, (rows, 1), 0)
    z_dn = jnp.where(row_id == 0, prev_row, pltpu.roll(z, 1, 0))
    z_up = jnp.where(row_id == rows - 1, next_row, pltpu.roll(z, rows - 1, 0))
    zrw_ref[...] = (cw_ref[0:1, :] * z_dn + cw_ref[1:2, :] * z + cw_ref[2:3, :] * z_up).astype(zrw_ref.dtype)


def _inproj(x, mod3, w_all, conv_w):
    bsz, seq, _ = x.shape
    tm = 512
    hpb = tm // X_HALO_ROWS
    n_halo = seq // X_HALO_ROWS
    widths = (RW_COLS, ZKV_COLS, Q_LORA, 2 * D_MODEL)
    return pl.pallas_call(
        _inproj_kernel,
        grid=(bsz, seq // tm),
        in_specs=[pl.BlockSpec((None, tm, D_MODEL), lambda b, i: (b, i, 0)),
                  pl.BlockSpec((None, X_HALO_ROWS, D_MODEL), lambda b, i: (b, jnp.maximum(i * hpb - 1, 0), 0)),
                  pl.BlockSpec((None, X_HALO_ROWS, D_MODEL),
                               lambda b, i: (b, jnp.minimum((i + 1) * hpb, n_halo - 1), 0)),
                  pl.BlockSpec((None, 6, D_MODEL), lambda b, i: (b, 0, 0)),
                  _const_spec(w_all.shape), _const_spec(conv_w.shape)],
        out_specs=[pl.BlockSpec((None, tm, n), lambda b, i: (b, i, 0)) for n in widths],
        out_shape=[jax.ShapeDtypeStruct((bsz, seq, n), BF16) for n in widths],
        compiler_params=pltpu.CompilerParams(dimension_semantics=("parallel", "parallel"),
                                             vmem_limit_bytes=VMEM_LIMIT),
        name="inproj",
    )(x, x, x, mod3, w_all, conv_w)


def _sigmoid(x):
    return 0.5 + 0.5 * jnp.tanh(0.5 * x)


def _head_sums(x, bd_ref):
    return jnp.concatenate([_dot(x[:, p * LANE:(p + 1) * LANE], bd_ref[...])
                            for p in range(RW_WIDTH // LANE)], axis=1)


def _scan_kernel(*refs, rev):
    if rev:
        (zm_ref, vec_ref, w2_ref, a2_ref, bd_ref, tri_ref,
         yf_ref, g2_ref, o_ref,
         at_s, rt_s, bh_s, kh_s, be_s, ke_s, v_s, vsw_s, eg_s, h_s, ya_s, yb_s) = refs
    else:
        (zm_ref, vec_ref, w2_ref, a2_ref, bd_ref, tri_ref,
         o_ref,
         at_s, rt_s, bh_s, kh_s, be_s, ke_s, v_s, vsw_s, eg_s, h_s, ya_s, yb_s) = refs
    rows = zm_ref.shape[0]
    n_chunks = rows // CHUNK
    i = pl.program_id(1)

    @pl.when(i == 0)
    def _():
        h_s[...] = jnp.zeros_like(h_s)

    def token_prep(wave):
        cs = sorted(wave)
        n_r = len(cs) * CHUNK
        rs = slice(cs[0] * CHUNK, cs[0] * CHUNK + n_r)
        zc = zm_ref[rs, :].astype(F32)
        r = zc[:, 0:RW_WIDTH]
        k = zc[:, RW_WIDTH:2 * RW_WIDTH]
        v = zc[:, 2 * RW_WIDTH:3 * RW_WIDTH]
        zw = zc[:, 3 * RW_WIDTH:3 * RW_WIDTH + LANE]
        za = zc[:, 3 * RW_WIDTH + LANE:3 * RW_WIDTH + 2 * LANE]
        w0 = vec_ref[0:1, :]
        a0 = vec_ref[1:2, :]
        k_k = vec_ref[2:3, :]
        k_a = vec_ref[3:4, :]
        v_s[rs, :] = v
        vsw_s[rs, :] = pltpu.roll(v, RW_HEAD, 1)
        w_lin = w0 + _dot(jnp.tanh(zw).astype(BF16), w2_ref[...])
        rate_lin = a0 + _dot(za.astype(BF16), a2_ref[...])
        kkv = k * k_k
        ssq = _head_sums((kkv * kkv).astype(BF16), bd_ref)
        yield
        lw = -DECAY_SCALE * _sigmoid(w_lin)
        lw_hi = lw.astype(BF16)
        lw_lo = (lw - lw_hi.astype(F32)).astype(BF16)
        tri = tri_ref[...]
        g_in = jnp.concatenate([_dot(tri, lw_hi[j * CHUNK:(j + 1) * CHUNK]) + _dot(tri, lw_lo[j * CHUNK:(j + 1) * CHUNK])
                                for j in range(len(cs))], axis=0)
        yield
        rate = _sigmoid(rate_lin)
        kk = kkv * lax.rsqrt(jnp.maximum(ssq, L2_EPS * L2_EPS))
        kd = k * (1.0 + (rate - 1.0) * k_a)
        av = -kk
        bv = kk * rate
        yield
        g_ex = g_in - lw
        last = 0 if rev else CHUNK - 1
        g_tot_rows = [g_in[j * CHUNK + last:j * CHUNK + last + 1, :] for j in range(len(cs))]
        eg_rows = [jnp.exp(t) for t in g_tot_rows]
        for j, c in enumerate(cs):
            eg_s[c] = jnp.broadcast_to(eg_rows[j], (8, RW_WIDTH))
        e_neg = jnp.exp(-g_in)
        e_end = e_neg * jnp.concatenate([jnp.broadcast_to(t, (CHUNK, RW_WIDTH)) for t in eg_rows], axis=0)
        at_s[rs, :] = av * jnp.exp(g_ex)
        rt_s[rs, :] = r * jnp.exp(g_in)
        yield
        bh_s[rs, :] = bv * e_neg
        kh_s[rs, :] = kd * e_neg
        be_s[rs, :] = bv * e_end
        ke_s[rs, :] = kd * e_end

    ri = lax.broadcasted_iota(jnp.int32, (CHUNK, CHUNK), 0)
    ci = lax.broadcasted_iota(jnp.int32, (CHUNK, CHUNK), 1)
    eye = jnp.where(ri == ci, 1.0, 0.0).astype(F32)
    ri2 = lax.broadcasted_iota(jnp.int32, (CHUNK, LANE), 0)
    ci2 = lax.broadcasted_iota(jnp.int32, (CHUNK, LANE), 1) % CHUNK
    m2_strict = (ci2 > ri2) if rev else (ci2 < ri2)
    m2_incl = (ci2 >= ri2) if rev else (ci2 <= ri2)
    zeros_tile = jnp.zeros((CHUNK, LANE), BF16)
    lower = lax.broadcasted_iota(jnp.int32, (CHUNK, LANE), 1) < RW_HEAD
    eye_up = jnp.where(lax.broadcasted_iota(jnp.int32, (CHUNK, LANE), 1)
                       == lax.broadcasted_iota(jnp.int32, (CHUNK, LANE), 0) + RW_HEAD, 1.0, 0.0).astype(F32)
    zeros_half = jnp.zeros((CHUNK, RW_HEAD), F32)

    def pad(t):
        return jnp.concatenate([t, zeros_half], axis=1)

    heads = range(RW_HEADS)
    chunk_order = list(range(n_chunks - 1, -1, -1)) if rev else list(range(n_chunks))
    waves = [chunk_order[w:w + SCAN_WAVE] for w in range(0, n_chunks, SCAN_WAVE)]

    def ld(ref, item):
        c, h = item
        return ref[c * CHUNK:(c + 1) * CHUNK, h * RW_HEAD:(h + 1) * RW_HEAD]

    def ld_v_upper(item):
        c, h = item
        src = v_s if h % 2 else vsw_s
        blk = src[c * CHUNK:(c + 1) * CHUNK, (h // 2) * LANE:(h // 2 + 1) * LANE]
        return jnp.where(lower, 0.0, blk).astype(BF16)

    def ld_decay(item):
        c, h = item
        return eg_s[c][0:1, h * RW_HEAD:(h + 1) * RW_HEAD]

    def chunk_local(items, out):
        n_it = range(len(items))
        rt = [ld(rt_s, it) for it in items]
        lhs = [jnp.concatenate([ld(at_s, items[i]).astype(BF16), rt[i].astype(BF16)], axis=0)
               for i in n_it]
        rhs = [jnp.concatenate([ld(bh_s, it).astype(BF16), ld(kh_s, it).astype(BF16)], axis=0)
               for it in items]
        a_all = [_dot_nt(lhs[i], rhs[i]) for i in n_it]
        yield
        top = [jnp.where(m2_strict, a_all[i][0:CHUNK], 0.0) for i in n_it]
        bot = [jnp.where(m2_incl, a_all[i][CHUNK:], 0.0).astype(BF16) for i in n_it]
        v_up = [ld_v_upper(it) for it in items]
        akv = [_dot(top[i].astype(BF16), jnp.concatenate([zeros_tile, v_up[i]], axis=0)) for i in n_it]
        yield

        z = [jnp.where(lower, top[i], 0.0) + eye_up for i in n_it]
        for _ in range(6):
            z = [_dot(z[i][:, 0:CHUNK].astype(BF16), z[i].astype(BF16)) + jnp.where(lower, 0.0, z[i])
                 for i in n_it]
            yield
        tb =[z[i][:, CHUNK:].astype(BF16) for i in n_it]

        w_m = [(pad(ld(at_s, items[i])) + akv[i]).astype(BF16) for i in n_it]
        pq = [_dot(tb[i], w_m[i]).astype(BF16) for i in n_it]
        pv = [jnp.concatenate([pq[i], v_up[i]], axis=0) for i in n_it]
        yield
        ryq =[_dot(bot[i], pv[i]) for i in n_it]
        ry = [(rt[i] + ryq[i][:, 0:RW_HEAD]).astype(BF16) for i in n_it]
        y0 = [jnp.where(lower, 0.0, ryq[i]) for i in n_it]
        yield
        bke = [jnp.concatenate([ld(be_s, it).astype(BF16), ld(ke_s, it).astype(BF16)], axis=0)
               for it in items]
        mq = [_dot_tn(bke[i], pv[i]) for i in n_it]
        m_m = [(mq[i][:, 0:RW_HEAD] + eye * ld_decay(items[i])).astype(BF16) for i in n_it]
        h0 = [jnp.where(lower, 0.0, mq[i]) for i in n_it]
        ry_m = [jnp.concatenate([ry[i], m_m[i]], axis=0) for i in n_it]
        out.update(ry_m=ry_m, y0=y0, h0=h0)

    hst = [h_s[h] for h in heads]

    def carried(wave, res):
        for ci, c in enumerate(wave):
            for h in heads:
                i = ci * RW_HEADS + h
                yh = _dot(res["ry_m"][i], hst[h].astype(BF16))
                y_dst = ya_s if h % 2 else yb_s
                y_dst[c * CHUNK:(c + 1) * CHUNK, (h // 2) * LANE:(h // 2 + 1) * LANE] = yh[0:CHUNK] + res["y0"][i]
                hst[h] = yh[CHUNK:] + res["h0"][i]
                if h % 4 == 3:
                    yield

    def emit(*gens):
        live = list(gens)
        while live:
            for g in list(live):
                if next(g, StopIteration) is StopIteration:
                    live.remove(g)

    results = [dict() for _ in waves]
    emit(token_prep(waves[0]))
    for w, wave in enumerate(waves):
        gens = [chunk_local([(c, h) for c in wave for h in heads], results[w])]
        if w + 1 < len(waves):
            gens.append(token_prep(waves[w + 1]))
        if w > 0:
            gens.append(carried(waves[w - 1], results[w - 1]))
        emit(*gens)
    emit(carried(waves[-1], results[-1]))
    for h in heads:
        h_s[h] = hst[h]

    y_dir = ya_s[...] + pltpu.roll(yb_s[...], RW_WIDTH - RW_HEAD, 1)
    if not rev:
        o_ref[...] = y_dir
    else:
        r_k = vec_ref[4:5, :]
        lnx_g = vec_ref[5:6, :]
        lnx_b = vec_ref[6:7, :]
        zc = zm_ref[...].astype(F32)
        r = zc[:, 0:RW_WIDTH]
        k = zc[:, RW_WIDTH:2 * RW_WIDTH]
        v = zc[:, 2 * RW_WIDTH:3 * RW_WIDTH]
        y = yf_ref[...] + y_dir
        inv_n = 1.0 / RW_HEAD
        mu = _head_sums(y.astype(BF16), bd_ref) * inv_n
        d = y - mu
        var = _head_sums((d * d).astype(BF16), bd_ref) * inv_n
        yn = d * lax.rsqrt(var + GN_EPS) * lnx_g + lnx_b
        rk = r * k * r_k
        rk_hi = rk.astype(BF16)
        rk_lo = (rk - rk_hi.astype(F32)).astype(BF16)
        bonus = (_head_sums(rk_hi, bd_ref) + _head_sums(rk_lo, bd_ref)) * v
        zg = zc[:, 3 * RW_WIDTH + 2 * LANE:RW_COLS]
        gate = _dot(_sigmoid(zg).astype(BF16), g2_ref[...])
        o_ref[...] = ((yn + bonus) * gate).astype(o_ref.dtype)


def _scan(z_rw, vecs, w2p, a2p, bd, tri, rev, y_fwd=None, g2=None):
    bsz, seq, _ = z_rw.shape
    rows = SCAN_ROWS
    nb = seq // rows
    n_chunks = rows // CHUNK

    def blk(i):
        return nb - 1 - i if rev else i

    in_specs = [
        pl.BlockSpec((None, rows, RW_COLS), lambda b, i: (b, blk(i), 0)),
        _const_spec(vecs.shape), _const_spec(w2p.shape),
        _const_spec(a2p.shape), _const_spec(bd.shape), _const_spec(tri.shape),
    ]
    args = [z_rw, vecs, w2p, a2p, bd, tri]
    scratch = [pltpu.VMEM((rows, RW_WIDTH), F32) for _ in range(8)]
    scratch += [pltpu.VMEM((n_chunks, 8, RW_WIDTH), F32),
                pltpu.VMEM((RW_HEADS, RW_HEAD, LANE), F32),
                pltpu.VMEM((rows, RW_WIDTH), F32), pltpu.VMEM((rows, RW_WIDTH), F32)]
    if rev:
        in_specs += [pl.BlockSpec((None, rows, RW_WIDTH), lambda b, i: (b, blk(i), 0)),
                     _const_spec(g2.shape)]
        args += [y_fwd, g2]
        out_dtype = BF16
    else:
        out_dtype = F32
    return pl.pallas_call(
        functools.partial(_scan_kernel, rev=rev),
        grid=(bsz, nb),
        in_specs=in_specs,
        out_specs=pl.BlockSpec((None, rows, RW_WIDTH), lambda b, i: (b, blk(i), 0)),
        out_shape=jax.ShapeDtypeStruct((bsz, seq, RW_WIDTH), out_dtype),
        scratch_shapes=scratch,
        compiler_params=pltpu.CompilerParams(dimension_semantics=("parallel", "arbitrary"),
                                             vmem_limit_bytes=VMEM_LIMIT),
        name="scan_bwd" if rev else "scan_fwd",
    )(*args)


def _rms(xf, g):
    return xf * lax.rsqrt(jnp.mean(xf * xf, axis=-1, keepdims=True) + RMS_EPS) * g


def _attn_kernel(zkv_ref, zq_ref, cos_ref, sin_ref, gq_ref, gkv_ref, wq1_ref, wk_ref, wvt_ref,
                 o_ref, k_s, vt_s):
    seq = zkv_ref.shape[0]
    tq = zq_ref.shape[0]
    kt = 256
    i = pl.program_id(1)

    @pl.when(i == 0)
    def _():
        def kv_body(t, carry):
            r0 = pl.multiple_of(t * kt, kt)
            zk = zkv_ref[pl.ds(r0, kt), :].astype(F32)
            kvn = _rms(zk[:, 0:KV_LORA], gkv_ref[...]).astype(BF16)
            kn = _dot(kvn, wk_ref[...])
            cs = cos_ref[pl.ds(r0, kt), :]
            sn = sin_ref[pl.ds(r0, kt), :]
            kpe = zk[:, KV_LORA:KV_LORA + LANE] * cs + zk[:, KV_LORA + LANE:KV_LORA + 2 * LANE] * sn
            for h in range(MLA_HEADS):
                sl = slice(h * HEAD_PAD, (h + 1) * HEAD_PAD)
                k_s[pl.ds(r0, kt), sl] = (kn[:, sl] + kpe).astype(BF16)
            vt = _dot_nt(wvt_ref[...], kvn).astype(BF16)
            ones = jnp.ones((VT_ROWS - V_HEAD, kt), BF16)
            for h in range(MLA_HEADS):
                vt_s[h * VT_ROWS:(h + 1) * VT_ROWS, pl.ds(r0, kt)] = jnp.concatenate(
                    [vt[h * V_HEAD:(h + 1) * V_HEAD], ones], axis=0)
            return carry
        lax.fori_loop(0, seq // kt, kv_body, 0)

    q0 = pl.multiple_of(i * tq, tq)
    qn = _rms(zq_ref[...].astype(F32), gq_ref[...]).astype(BF16)
    q1 = _dot(qn, wq1_ref[...])
    half = QK_ROPE // 2
    lane = lax.broadcasted_iota(jnp.int32, q1.shape, 1) % HEAD_PAD
    q2 = jnp.where((lane >= QK_NOPE) & (lane < QK_NOPE + half), -pltpu.roll(q1, q1.shape[1] - half, 1),
                   jnp.where((lane >= QK_NOPE + half) & (lane < QK_NOPE + QK_ROPE), pltpu.roll(q1, half, 1), 0.0))
    cs = cos_ref[pl.ds(q0, tq), :]
    sn = sin_ref[pl.ds(q0, tq), :]
    scale = (QK_NOPE + QK_ROPE) ** -0.5 * LOG2_E

    kparts = [slice(j * (seq // ATTN_KEY_PARTS), (j + 1) * (seq // ATTN_KEY_PARTS)) for j in range(ATTN_KEY_PARTS)]

    def scores(h):
        sl = slice(h * HEAD_PAD, (h + 1) * HEAD_PAD)
        qh = ((q1[:, sl] * cs + q2[:, sl] * sn) * scale).astype(BF16)
        return [_dot_nt(k_s[kp, sl], qh) for kp in kparts]

    outs = []
    st_next = scores(0)
    for h in range(MLA_HEADS):
        st = st_next
        if h + 1 < MLA_HEADS:
            st_next = scores(h + 1)
        m = functools.reduce(jnp.maximum, [jnp.max(t, axis=0, keepdims=True) for t in st])
        ps = [jnp.exp2(t - m).astype(BF16) for t in st]
        ol = sum(_dot(vt_s[h * VT_ROWS:(h + 1) * VT_ROWS, kp], p) for kp, p in zip(kparts, ps))
        outs.append(ol[0:V_HEAD] / ol[V_HEAD:V_HEAD + 1])
    o_ref[...] = jnp.concatenate(outs, axis=0).T.astype(o_ref.dtype)


def _attn(z_kv, z_q, cos_t, sin_t, gq, gkv, wq1, wk, wvt):
    bsz, seq, _ = z_kv.shape
    tq = 512
    kw = MLA_HEADS * HEAD_PAD
    return pl.pallas_call(
        _attn_kernel,
        grid=(bsz, seq // tq),
        in_specs=[pl.BlockSpec((None, seq, ZKV_COLS), lambda b, i: (b, 0, 0)),
                  pl.BlockSpec((None, tq, Q_LORA), lambda b, i: (b, i, 0)),
                  pl.BlockSpec((None, seq, LANE), lambda b, i: (b, 0, 0)),
                  pl.BlockSpec((None, seq, LANE), lambda b, i: (b, 0, 0)),
                  _const_spec(gq.shape), _const_spec(gkv.shape), _const_spec(wq1.shape),
                  _const_spec(wk.shape), _const_spec(wvt.shape)],
        out_specs=pl.BlockSpec((None, tq, MLA_HEADS * V_HEAD), lambda b, i: (b, i, 0)),
        out_shape=jax.ShapeDtypeStruct((bsz, seq, MLA_HEADS * V_HEAD), BF16),
        scratch_shapes=[pltpu.VMEM((seq, kw), BF16), pltpu.VMEM((MLA_HEADS * VT_ROWS, seq), BF16)],
        compiler_params=pltpu.CompilerParams(dimension_semantics=("parallel", "arbitrary"),
                                             vmem_limit_bytes=VMEM_LIMIT),
        name="attn",
    )(z_kv, z_q, cos_t, sin_t, gq, gkv, wq1, wk, wvt)


def _mix_kernel(yrw_ref, ymla_ref, ga_ref, gb_ref, x_ref, mod_ref, wr_ref, wm_ref, wo_ref, ln_ref, o_ref):
    gate1 = mod_ref[2:3, :]
    br_rw = _dot(yrw_ref[...], wr_ref[...])
    br_mla = _dot(ymla_ref[...], wm_ref[...])
    mixed = (_sigmoid(ga_ref[...].astype(F32)) * br_rw
             + _sigmoid(gb_ref[...].astype(F32)) * br_mla)
    out = _dot(mixed.astype(BF16), wo_ref[...])
    t = DN_ALPHA * x_ref[...] + (1.0 + gate1) * out
    o_ref[...] = _layer_norm(t, ln_ref[0:1, :], ln_ref[1:2, :])


def _mix(y_rw, y_mla, z_g, x, mod3, wr, wm, wo, ln):
    bsz, seq, _ = x.shape
    tm = 512
    row = lambda n: pl.BlockSpec((None, tm, n), lambda b, i: (b, i, 0))
    return pl.pallas_call(
        _mix_kernel,
        grid=(bsz, seq // tm),
        in_specs=[row(RW_WIDTH), row(MLA_HEADS * V_HEAD),
                  pl.BlockSpec((None, tm, D_MODEL), lambda b, i: (b, i, 0)),
                  pl.BlockSpec((None, tm, D_MODEL), lambda b, i: (b, i, 1)),
                  row(D_MODEL),
                  pl.BlockSpec((None, 6, D_MODEL), lambda b, i: (b, 0, 0)),
                  _const_spec(wr.shape), _const_spec(wm.shape), _const_spec(wo.shape),
                  _const_spec(ln.shape)],
        out_specs=row(D_MODEL),
        out_shape=jax.ShapeDtypeStruct((bsz, seq, D_MODEL), F32),
        compiler_params=pltpu.CompilerParams(dimension_semantics=("parallel", "parallel"),
                                             vmem_limit_bytes=VMEM_LIMIT),
        name="mix",
    )(y_rw, y_mla, z_g, z_g, x, mod3, wr, wm, wo, ln)


def _ffn_kernel(x_ref, mod_ref, w1_ref, w2_ref, ln_ref, o_ref):
    shift = mod_ref[3:4, :]
    scale = mod_ref[4:5, :]
    gate2 = mod_ref[5:6, :]
    x1 = x_ref[...]
    h = (x1 * (1.0 + scale) + shift).astype(BF16)
    kc = 1024
    acc = jnp.zeros(x1.shape, F32)
    for c in range(D_FF // kc):
        u = jnp.maximum(_dot(h, w1_ref[:, c * kc:(c + 1) * kc]), 0.0)
        acc = acc + _dot((u * u).astype(BF16), w2_ref[c * kc:(c + 1) * kc, :])
    t = DN_ALPHA * x1 + (1.0 + gate2) * acc
    o_ref[...] = _layer_norm(t, ln_ref[0:1, :], ln_ref[1:2, :])


def _ffn(x1, mod3, w1, w2, ln):
    bsz, seq, _ = x1.shape
    tm = 512
    row = pl.BlockSpec((None, tm, D_MODEL), lambda b, i: (b, i, 0))
    return pl.pallas_call(
        _ffn_kernel,
        grid=(bsz, seq // tm),
        in_specs=[row, pl.BlockSpec((None, 6, D_MODEL), lambda b, i: (b, 0, 0)),
                  _const_spec(w1.shape), _const_spec(w2.shape), _const_spec(ln.shape)],
        out_specs=row,
        out_shape=jax.ShapeDtypeStruct((bsz, seq, D_MODEL), F32),
        compiler_params=pltpu.CompilerParams(dimension_semantics=("parallel", "parallel"),
                                             vmem_limit_bytes=VMEM_LIMIT),
        name="ffn",
    )(x1, mod3, w1, w2, ln)


def _pad_cols(w, left, total):
    return jnp.pad(w, ((0, 0), (left, total - left - w.shape[1])))


def _inproj_weight(w_in):
    o = 0
    w_rw = w_in[:, o:o + RW_COLS]; o += RW_COLS
    w_q = w_in[:, o:o + Q_LORA]; o += Q_LORA
    w_kv = w_in[:, o:o + KV_LORA]; o += KV_LORA
    w_kr = w_in[:, o:o + QK_ROPE]; o += QK_ROPE
    w_g = w_in[:, o:o + 2 * D_MODEL]
    half = QK_ROPE // 2
    w_kr_rot = jnp.concatenate([-w_kr[:, half:], w_kr[:, :half]], axis=1)
    w_all = jnp.concatenate([w_rw, w_kv, _pad_cols(w_kr, QK_NOPE, LANE), _pad_cols(w_kr_rot, QK_NOPE, LANE),
                             w_q, w_g], axis=1)
    return w_all.astype(BF16)


def _mla_weights(w_uq, w_ukv):
    half = QK_ROPE // 2
    q = w_uq.reshape(Q_LORA, MLA_HEADS, QK_NOPE + QK_ROPE)
    q_nope, q_1, q_2 = q[..., :QK_NOPE], q[..., QK_NOPE:QK_NOPE + half], q[..., QK_NOPE + half:]
    zpad = jnp.zeros((Q_LORA, MLA_HEADS, HEAD_PAD - QK_NOPE - QK_ROPE), F32)
    wq1 = jnp.concatenate([q_nope, q_1, q_2, zpad], axis=-1).reshape(Q_LORA, MLA_HEADS * HEAD_PAD)
    kv = w_ukv.reshape(KV_LORA, MLA_HEADS, QK_NOPE + V_HEAD)
    zhalf = jnp.zeros((KV_LORA, MLA_HEADS, HEAD_PAD - QK_NOPE), F32)
    wk = jnp.concatenate([kv[..., :QK_NOPE], zhalf], axis=-1).reshape(KV_LORA, MLA_HEADS * HEAD_PAD)
    wvt = kv[..., QK_NOPE:].reshape(KV_LORA, MLA_HEADS * V_HEAD).T
    return wq1.astype(BF16), wk.astype(BF16), wvt.astype(BF16)


def _scan_constants(rev):
    rows = np.arange(CHUNK)
    tri = (rows[None, :] >= rows[:, None]) if rev else (rows[None, :] <= rows[:, None])
    ch = np.arange(LANE)
    bd = (ch[:, None] // RW_HEAD) == (ch[None, :] // RW_HEAD)
    return jnp.asarray(tri, BF16), jnp.asarray(bd, BF16)


def _lora_pair(w, d):
    zero = jnp.zeros_like(w[0])
    return jnp.concatenate([zero, w[1]] if d else [w[0], zero], axis=0).astype(BF16)


def kernel(x, c, positions, w_ada, b_ada, w_in, rw_conv, rw_w0, rw_w2, rw_a0, rw_a2, rw_k_k, rw_k_a, rw_r_k, rw_g2, rw_lnx_g, rw_lnx_b, mla_q_norm_g, mla_kv_norm_g, mla_w_uq, mla_w_ukv, w_br_rwkv, w_br_mla, w_out, ln1_g, ln1_b, w_ff1, w_ff2, ln2_g, ln2_b):
    bsz, seq, _ = x.shape
    cos_t, sin_t = _rope_tables(positions)
    for l in range(DEPTH):
        mod3 = _ada(c, w_ada[l], b_ada[l]).reshape(bsz, 6, D_MODEL)
        z_rw, z_kv, z_q, z_g = _inproj(x, mod3, _inproj_weight(w_in[l]), rw_conv[l])

        zero = jnp.zeros((RW_WIDTH,), F32)
        y_rw = None
        for d in (0, 1):
            vecs = jnp.stack([rw_w0[l, d], rw_a0[l, d], rw_k_k[l], rw_k_a[l], rw_r_k[l],
                              rw_lnx_g[l], rw_lnx_b[l], zero])
            tri, bd = _scan_constants(bool(d))
            y_rw = _scan(z_rw, vecs, _lora_pair(rw_w2[l], d), _lora_pair(rw_a2[l], d), bd, tri,
                         rev=bool(d), y_fwd=y_rw, g2=rw_g2[l].astype(BF16))

        wq1, wk, wvt = _mla_weights(mla_w_uq[l], mla_w_ukv[l])
        y_mla = _attn(z_kv, z_q, cos_t, sin_t, mla_q_norm_g[l].reshape(1, Q_LORA),
                      mla_kv_norm_g[l].reshape(1, KV_LORA), wq1, wk, wvt)

        x = _mix(y_rw, y_mla, z_g, x, mod3, w_br_rwkv[l].astype(BF16), w_br_mla[l].astype(BF16),
                 w_out[l].astype(BF16), jnp.stack([ln1_g[l], ln1_b[l]]))
        x = _ffn(x, mod3, w_ff1[l].astype(BF16), w_ff2[l].astype(BF16), jnp.stack([ln2_g[l], ln2_b[l]]))
    return x
```

```python
import functools

import numpy as np
import jax
import jax.numpy as jnp
from jax import lax
from jax.experimental import pallas as pl
from jax.experimental.pallas import tpu as pltpu

F32 = jnp.float32
BF16 = jnp.bfloat16

D_MODEL = 1024
RW_HEADS = 8
RW_HEAD = 64
RW_WIDTH = RW_HEADS * RW_HEAD
DECAY_LORA = 64
AAA_LORA = 64
GATE_LORA = 128
RW_COLS = 3 * RW_WIDTH + 2 * DECAY_LORA + 2 * AAA_LORA + GATE_LORA
MLA_HEADS = 8
QK_NOPE = 64
QK_ROPE = 32
V_HEAD = 64
Q_LORA = 384
KV_LORA = 256
ROPE_THETA = 10000.0
D_FF = 4 * D_MODEL
LN_EPS = 1e-5
RMS_EPS = 1e-6
GN_EPS = 64e-5
L2_EPS = 1e-12
DEPTH = 1
DN_ALPHA = (2.0 * DEPTH) ** 0.25
LOG2_E = 1.4426950408889634
DECAY_SCALE = 0.6065306597126334

LANE = 128
F32_SUBLANES = 8
BF16_SUBLANES = 16
VMEM_LIMIT = 56 * 1024 * 1024

ROW_TILE = 512
ADA_COL_TILE = 1536
FFN_HIDDEN_TILE = 1024
CHUNK = 64
SCAN_ROWS = 1024
SCAN_WAVE = 4
X_HALO_ROWS = F32_SUBLANES
HEAD_PAD = LANE
ATTN_Q_TILE = 512
ATTN_KV_TILE = 256
ATTN_KEY_PARTS = 4
VT_ROWS = V_HEAD + BF16_SUBLANES
ZKV_COLS = KV_LORA + 2 * LANE


def _dot(a, b):
    return jnp.dot(a, b, preferred_element_type=F32)


def _dot_nt(a, b):
    return lax.dot_general(a, b, (((1,), (1,)), ((), ())), preferred_element_type=F32)


def _dot_tn(a, b):
    return lax.dot_general(a, b, (((0,), (0,)), ((), ())), preferred_element_type=F32)


def _const_spec(shape):
    zeros = (0,) * len(shape)
    return pl.BlockSpec(shape, lambda *_: zeros)


def _layer_norm(t, g, b):
    mu = jnp.mean(t, axis=-1, keepdims=True)
    d = t - mu
    var = jnp.mean(d * d, axis=-1, keepdims=True)
    return d * lax.rsqrt(var + LN_EPS) * g + b


def _ada_kernel(c_ref, w_ref, b_ref, o_ref):
    c = c_ref[...]
    act = c * jax.nn.sigmoid(c)
    o_ref[...] = _dot(act.astype(BF16), w_ref[...].astype(BF16)) + b_ref[...]


def _ada(c, w, b):
    bsz = c.shape[0]
    n = w.shape[1]
    tn = ADA_COL_TILE
    return pl.pallas_call(
        _ada_kernel,
        grid=(n // tn,),
        in_specs=[_const_spec((bsz, D_MODEL)),
                  pl.BlockSpec((D_MODEL, tn), lambda j: (0, j)),
                  pl.BlockSpec((1, tn), lambda j: (0, j))],
        out_specs=pl.BlockSpec((bsz, tn), lambda j: (0, j)),
        out_shape=jax.ShapeDtypeStruct((bsz, n), F32),
        name="ada",
    )(c, w, b.reshape(1, n))


def _rope_kernel(pos_ref, inv_ref, place_ref, one_ref, cos_ref, sin_ref):
    ang = inv_ref[...] * pos_ref[...].astype(F32)

    def table(t):
        hi = t.astype(BF16)
        lo = (t - hi.astype(F32)).astype(BF16)
        return _dot_tn(hi, place_ref[...]) + _dot_tn(lo, place_ref[...])

    cos_ref[...] = table(jnp.cos(ang)) + one_ref[...]
    sin_ref[...] = table(jnp.sin(ang))


def _rope_tables(positions):
    bsz, seq = positions.shape
    half = QK_ROPE // 2
    inv = (ROPE_THETA ** (-np.arange(half, dtype=np.float32) / half)).reshape(half, 1)
    place = np.zeros((half, LANE), np.float32)
    place[np.arange(half), QK_NOPE + np.arange(half)] = 1.0
    place[np.arange(half), QK_NOPE + half + np.arange(half)] = 1.0
    one = np.zeros((1, LANE), np.float32)
    one[0, :QK_NOPE] = 1.0
    spec = pl.BlockSpec((None, seq, LANE), lambda b: (b, 0, 0))
    return pl.pallas_call(
        _rope_kernel,
        grid=(bsz,),
        in_specs=[pl.BlockSpec((None, 1, seq), lambda b: (b, 0, 0)), _const_spec((half, 1)),
                  _const_spec((half, LANE)), _const_spec((1, LANE))],
        out_specs=[spec, spec],
        out_shape=[jax.ShapeDtypeStruct((bsz, seq, LANE), F32)] * 2,
        name="rope",
    )(positions.reshape(bsz, 1, seq), jnp.asarray(inv), jnp.asarray(place, BF16), jnp.asarray(one))


def _inproj_kernel(x_ref, xp_ref, xn_ref, mod_ref, w_ref, cw_ref, zrw_ref, zkv_ref, zq_ref, zg_ref):
    i = pl.program_id(1)
    nb = pl.num_programs(1)
    rows = x_ref.shape[0]
    shift = mod_ref[0:1, :]
    scale = mod_ref[1:2, :]
    h = (x_ref[...] * (1.0 + scale) + shift).astype(BF16)
    col = RW_COLS
    for o_ref in (zkv_ref, zq_ref, zg_ref):
        n = o_ref.shape[-1]
        o_ref[...] = _dot(h, w_ref[:, col:col + n]).astype(o_ref.dtype)
        col += n

    z = _dot(h, w_ref[:, 0:RW_COLS])
    x_halo = jnp.concatenate([xp_ref[...], xn_ref[...]], axis=0)
    z_halo = _dot((x_halo * (1.0 + scale) + shift).astype(BF16), w_ref[:, 0:RW_COLS])
    prev_row = z_halo[X_HALO_ROWS - 1:X_HALO_ROWS, :] * jnp.where(i > 0, 1.0, 0.0)
    next_row = z_halo[X_HALO_ROWS:X_HALO_ROWS + 1, :] * jnp.where(i < nb - 1, 1.0, 0.0)
    row_id = lax.broadcasted_iota(jnp.int32, (rows, 1), 0)
    z_dn = jnp.where(row_id == 0, prev_row, pltpu.roll(z, 1, 0))
    z_up = jnp.where(row_id == rows - 1, next_row, pltpu.roll(z, rows - 1, 0))
    zrw_ref[...] = (cw_ref[0:1, :] * z_dn + cw_ref[1:2, :] * z + cw_ref[2:3, :] * z_up).astype(zrw_ref.dtype)


def _inproj(x, mod3, w_all, conv_w):
    bsz, seq, _ = x.shape
    tm = ROW_TILE
    hpb = tm // X_HALO_ROWS
    n_halo = seq // X_HALO_ROWS
    widths = (RW_COLS, ZKV_COLS, Q_LORA, 2 * D_MODEL)
    return pl.pallas_call(
        _inproj_kernel,
        grid=(bsz, seq // tm),
        in_specs=[pl.BlockSpec((None, tm, D_MODEL), lambda b, i: (b, i, 0)),
                  pl.BlockSpec((None, X_HALO_ROWS, D_MODEL), lambda b, i: (b, jnp.maximum(i * hpb - 1, 0), 0)),
                  pl.BlockSpec((None, X_HALO_ROWS, D_MODEL),
                               lambda b, i: (b, jnp.minimum((i + 1) * hpb, n_halo - 1), 0)),
                  pl.BlockSpec((None, 6, D_MODEL), lambda b, i: (b, 0, 0)),
                  _const_spec(w_all.shape), _const_spec(conv_w.shape)],
        out_specs=[pl.BlockSpec((None, tm, n), lambda b, i: (b, i, 0)) for n in widths],
        out_shape=[jax.ShapeDtypeStruct((bsz, seq, n), BF16) for n in widths],
        compiler_params=pltpu.CompilerParams(dimension_semantics=("parallel", "parallel"),
                                             vmem_limit_bytes=VMEM_LIMIT),
        name="inproj",
    )(x, x, x, mod3, w_all, conv_w)


def _sigmoid(x):
    return 0.5 + 0.5 * jnp.tanh(0.5 * x)


def _head_sums(x, bd_ref):
    return jnp.concatenate([_dot(x[:, p * LANE:(p + 1) * LANE], bd_ref[...])
                            for p in range(RW_WIDTH // LANE)], axis=1)


def _scan_kernel(*refs, rev):
    if rev:
        (zm_ref, vec_ref, w2_ref, a2_ref, bd_ref, tri_ref,
         yf_ref, g2_ref, o_ref,
         at_s, rt_s, bh_s, kh_s, be_s, ke_s, v_s, vsw_s, eg_s, h_s, ya_s, yb_s) = refs
    else:
        (zm_ref, vec_ref, w2_ref, a2_ref, bd_ref, tri_ref,
         o_ref,
         at_s, rt_s, bh_s, kh_s, be_s, ke_s, v_s, vsw_s, eg_s, h_s, ya_s, yb_s) = refs
    rows = zm_ref.shape[0]
    n_chunks = rows // CHUNK
    i = pl.program_id(1)

    @pl.when(i == 0)
    def _():
        h_s[...] = jnp.zeros_like(h_s)

    def token_prep(wave):
        cs = sorted(wave)
        n_r = len(cs) * CHUNK
        rs = slice(cs[0] * CHUNK, cs[0] * CHUNK + n_r)
        zc = zm_ref[rs, :].astype(F32)
        r = zc[:, 0:RW_WIDTH]
        k = zc[:, RW_WIDTH:2 * RW_WIDTH]
        v = zc[:, 2 * RW_WIDTH:3 * RW_WIDTH]
        zw = zc[:, 3 * RW_WIDTH:3 * RW_WIDTH + LANE]
        za = zc[:, 3 * RW_WIDTH + LANE:3 * RW_WIDTH + 2 * LANE]
        w0 = vec_ref[0:1, :]
        a0 = vec_ref[1:2, :]
        k_k = vec_ref[2:3, :]
        k_a = vec_ref[3:4, :]
        v_s[rs, :] = v
        vsw_s[rs, :] = pltpu.roll(v, RW_HEAD, 1)
        w_lin = w0 + _dot(jnp.tanh(zw).astype(BF16), w2_ref[...])
        rate_lin = a0 + _dot(za.astype(BF16), a2_ref[...])
        kkv = k * k_k
        ssq = _head_sums((kkv * kkv).astype(BF16), bd_ref)
        yield
        lw = -DECAY_SCALE * _sigmoid(w_lin)
        lw_hi = lw.astype(BF16)
        lw_lo = (lw - lw_hi.astype(F32)).astype(BF16)
        tri = tri_ref[...]
        g_in = jnp.concatenate([_dot(tri, lw_hi[j * CHUNK:(j + 1) * CHUNK]) + _dot(tri, lw_lo[j * CHUNK:(j + 1) * CHUNK])
                                for j in range(len(cs))], axis=0)
        yield
        rate = _sigmoid(rate_lin)
        kk = kkv * lax.rsqrt(jnp.maximum(ssq, L2_EPS * L2_EPS))
        kd = k * (1.0 + (rate - 1.0) * k_a)
        av = -kk
        bv = kk * rate
        yield
        g_ex = g_in - lw
        last = 0 if rev else CHUNK - 1
        g_tot_rows = [g_in[j * CHUNK + last:j * CHUNK + last + 1, :] for j in range(len(cs))]
        eg_rows = [jnp.exp(t) for t in g_tot_rows]
        for j, c in enumerate(cs):
            eg_s[c] = jnp.broadcast_to(eg_rows[j], (8, RW_WIDTH))
        e_neg = jnp.exp(-g_in)
        e_end = e_neg * jnp.concatenate([jnp.broadcast_to(t, (CHUNK, RW_WIDTH)) for t in eg_rows], axis=0)
        at_s[rs, :] = av * jnp.exp(g_ex)
        rt_s[rs, :] = r * jnp.exp(g_in)
        yield
        bh_s[rs, :] = bv * e_neg
        kh_s[rs, :] = kd * e_neg
        be_s[rs, :] = bv * e_end
        ke_s[rs, :] = kd * e_end

    ri = lax.broadcasted_iota(jnp.int32, (CHUNK, CHUNK), 0)
    ci = lax.broadcasted_iota(jnp.int32, (CHUNK, CHUNK), 1)
    eye = jnp.where(ri == ci, 1.0, 0.0).astype(F32)
    ri2 = lax.broadcasted_iota(jnp.int32, (CHUNK, LANE), 0)
    ci2 = lax.broadcasted_iota(jnp.int32, (CHUNK, LANE), 1) % CHUNK
    m2_strict = (ci2 > ri2) if rev else (ci2 < ri2)
    m2_incl = (ci2 >= ri2) if rev else (ci2 <= ri2)
    zeros_tile = jnp.zeros((CHUNK, LANE), BF16)
    lower = lax.broadcasted_iota(jnp.int32, (CHUNK, LANE), 1) < RW_HEAD
    eye_up = jnp.where(lax.broadcasted_iota(jnp.int32, (CHUNK, LANE), 1)
                       == lax.broadcasted_iota(jnp.int32, (CHUNK, LANE), 0) + RW_HEAD, 1.0, 0.0).astype(F32)
    zeros_half = jnp.zeros((CHUNK, RW_HEAD), F32)

    def pad(t):
        return jnp.concatenate([t, zeros_half], axis=1)

    heads = range(RW_HEADS)
    chunk_order = list(range(n_chunks - 1, -1, -1)) if rev else list(range(n_chunks))
    waves = [chunk_order[w:w + SCAN_WAVE] for w in range(0, n_chunks, SCAN_WAVE)]

    def ld(ref, item):
        c, h = item
        return ref[c * CHUNK:(c + 1) * CHUNK, h * RW_HEAD:(h + 1) * RW_HEAD]

    def ld_v_upper(item):
        c, h = item
        src = v_s if h % 2 else vsw_s
        blk = src[c * CHUNK:(c + 1) * CHUNK, (h // 2) * LANE:(h // 2 + 1) * LANE]
        return jnp.where(lower, 0.0, blk).astype(BF16)

    def ld_decay(item):
        c, h = item
        return eg_s[c][0:1, h * RW_HEAD:(h + 1) * RW_HEAD]

    def chunk_local(items, out):
        n_it = range(len(items))
        rt = [ld(rt_s, it) for it in items]
        lhs = [jnp.concatenate([ld(at_s, items[i]).astype(BF16), rt[i].astype(BF16)], axis=0)
               for i in n_it]
        rhs = [jnp.concatenate([ld(bh_s, it).astype(BF16), ld(kh_s, it).astype(BF16)], axis=0)
               for it in items]
        a_all = [_dot_nt(lhs[i], rhs[i]) for i in n_it]
        yield
        top = [jnp.where(m2_strict, a_all[i][0:CHUNK], 0.0) for i in n_it]
        bot = [jnp.where(m2_incl, a_all[i][CHUNK:], 0.0).astype(BF16) for i in n_it]
        v_up = [ld_v_upper(it) for it in items]
        akv = [_dot(top[i].astype(BF16), jnp.concatenate([zeros_tile, v_up[i]], axis=0)) for i in n_it]
        yield

        z = [jnp.where(lower, top[i], 0.0) + eye_up for i in n_it]
        for _ in range(6):
            z = [_dot(z[i][:, 0:CHUNK].astype(BF16), z[i].astype(BF16)) + jnp.where(lower, 0.0, z[i])
                 for i in n_it]
            yield
        tb =[z[i][:, CHUNK:].astype(BF16) for i in n_it]

        w_m = [(pad(ld(at_s, items[i])) + akv[i]).astype(BF16) for i in n_it]
        pq = [_dot(tb[i], w_m[i]).astype(BF16) for i in n_it]
        pv = [jnp.concatenate([pq[i], v_up[i]], axis=0) for i in n_it]
        yield
        ryq =[_dot(bot[i], pv[i]) for i in n_it]
        ry = [(rt[i] + ryq[i][:, 0:RW_HEAD]).astype(BF16) for i in n_it]
        y0 = [jnp.where(lower, 0.0, ryq[i]) for i in n_it]
        yield
        bke = [jnp.concatenate([ld(be_s, it).astype(BF16), ld(ke_s, it).astype(BF16)], axis=0)
               for it in items]
        mq = [_dot_tn(bke[i], pv[i]) for i in n_it]
        m_m = [(mq[i][:, 0:RW_HEAD] + eye * ld_decay(items[i])).astype(BF16) for i in n_it]
        h0 = [jnp.where(lower, 0.0, mq[i]) for i in n_it]
        ry_m = [jnp.concatenate([ry[i], m_m[i]], axis=0) for i in n_it]
        out.update(ry_m=ry_m, y0=y0, h0=h0)

    hst = [h_s[h] for h in heads]

    def carried(wave, res):
        for ci, c in enumerate(wave):
            for h in heads:
                i = ci * RW_HEADS + h
                yh = _dot(res["ry_m"][i], hst[h].astype(BF16))
                y_dst = ya_s if h % 2 else yb_s
                y_dst[c * CHUNK:(c + 1) * CHUNK, (h // 2) * LANE:(h // 2 + 1) * LANE] = yh[0:CHUNK] + res["y0"][i]
                hst[h] = yh[CHUNK:] + res["h0"][i]
                if h % 4 == 3:
                    yield

    def emit(*gens):
        live = list(gens)
        while live:
            for g in list(live):
                if next(g, StopIteration) is StopIteration:
                    live.remove(g)

    results = [dict() for _ in waves]
    emit(token_prep(waves[0]))
    for w, wave in enumerate(waves):
        gens = [chunk_local([(c, h) for c in wave for h in heads], results[w])]
        if w + 1 < len(waves):
            gens.append(token_prep(waves[w + 1]))
        if w > 0:
            gens.append(carried(waves[w - 1], results[w - 1]))
        emit(*gens)
    emit(carried(waves[-1], results[-1]))
    for h in heads:
        h_s[h] = hst[h]

    y_dir = ya_s[...] + pltpu.roll(yb_s[...], RW_WIDTH - RW_HEAD, 1)
    if not rev:
        o_ref[...] = y_dir
    else:
        r_k = vec_ref[4:5, :]
        lnx_g = vec_ref[5:6, :]
        lnx_b = vec_ref[6:7, :]
        zc = zm_ref[...].astype(F32)
        r = zc[:, 0:RW_WIDTH]
        k = zc[:, RW_WIDTH:2 * RW_WIDTH]
        v = zc[:, 2 * RW_WIDTH:3 * RW_WIDTH]
        y = yf_ref[...] + y_dir
        inv_n = 1.0 / RW_HEAD
        mu = _head_sums(y.astype(BF16), bd_ref) * inv_n
        d = y - mu
        var = _head_sums((d * d).astype(BF16), bd_ref) * inv_n
        yn = d * lax.rsqrt(var + GN_EPS) * lnx_g + lnx_b
        rk = r * k * r_k
        rk_hi = rk.astype(BF16)
        rk_lo = (rk - rk_hi.astype(F32)).astype(BF16)
        bonus = (_head_sums(rk_hi, bd_ref) + _head_sums(rk_lo, bd_ref)) * v
        zg = zc[:, 3 * RW_WIDTH + 2 * LANE:RW_COLS]
        gate = _dot(_sigmoid(zg).astype(BF16), g2_ref[...])
        o_ref[...] = ((yn + bonus) * gate).astype(o_ref.dtype)


def _scan(z_rw, vecs, w2p, a2p, bd, tri, rev, y_fwd=None, g2=None):
    bsz, seq, _ = z_rw.shape
    rows = SCAN_ROWS
    nb = seq // rows
    n_chunks = rows // CHUNK

    def blk(i):
        return nb - 1 - i if rev else i

    in_specs = [
        pl.BlockSpec((None, rows, RW_COLS), lambda b, i: (b, blk(i), 0)),
        _const_spec(vecs.shape), _const_spec(w2p.shape),
        _const_spec(a2p.shape), _const_spec(bd.shape), _const_spec(tri.shape),
    ]
    args = [z_rw, vecs, w2p, a2p, bd, tri]
    scratch = [pltpu.VMEM((rows, RW_WIDTH), F32) for _ in range(8)]
    scratch += [pltpu.VMEM((n_chunks, 8, RW_WIDTH), F32),
                pltpu.VMEM((RW_HEADS, RW_HEAD, LANE), F32),
                pltpu.VMEM((rows, RW_WIDTH), F32), pltpu.VMEM((rows, RW_WIDTH), F32)]
    if rev:
        in_specs += [pl.BlockSpec((None, rows, RW_WIDTH), lambda b, i: (b, blk(i), 0)),
                     _const_spec(g2.shape)]
        args += [y_fwd, g2]
        out_dtype = BF16
    else:
        out_dtype = F32
    return pl.pallas_call(
        functools.partial(_scan_kernel, rev=rev),
        grid=(bsz, nb),
        in_specs=in_specs,
        out_specs=pl.BlockSpec((None, rows, RW_WIDTH), lambda b, i: (b, blk(i), 0)),
        out_shape=jax.ShapeDtypeStruct((bsz, seq, RW_WIDTH), out_dtype),
        scratch_shapes=scratch,
        compiler_params=pltpu.CompilerParams(dimension_semantics=("parallel", "arbitrary"),
                                             vmem_limit_bytes=VMEM_LIMIT),
        name="scan_bwd" if rev else "scan_fwd",
    )(*args)


def _rms(xf, g):
    return xf * lax.rsqrt(jnp.mean(xf * xf, axis=-1, keepdims=True) + RMS_EPS) * g


def _attn_kernel(zkv_ref, zq_ref, cos_ref, sin_ref, gq_ref, gkv_ref, wq1_ref, wk_ref, wvt_ref,
                 o_ref, k_s, vt_s):
    seq = zkv_ref.shape[0]
    tq = zq_ref.shape[0]
    kt = ATTN_KV_TILE
    i = pl.program_id(1)

    @pl.when(i == 0)
    def _():
        def kv_body(t, carry):
            r0 = pl.multiple_of(t * kt, kt)
            zk = zkv_ref[pl.ds(r0, kt), :].astype(F32)
            kvn = _rms(zk[:, 0:KV_LORA], gkv_ref[...]).astype(BF16)
            kn = _dot(kvn, wk_ref[...])
            cs = cos_ref[pl.ds(r0, kt), :]
            sn = sin_ref[pl.ds(r0, kt), :]
            kpe = zk[:, KV_LORA:KV_LORA + LANE] * cs + zk[:, KV_LORA + LANE:KV_LORA + 2 * LANE] * sn
            for h in range(MLA_HEADS):
                sl = slice(h * HEAD_PAD, (h + 1) * HEAD_PAD)
                k_s[pl.ds(r0, kt), sl] = (kn[:, sl] + kpe).astype(BF16)
            vt = _dot_nt(wvt_ref[...], kvn).astype(BF16)
            ones = jnp.ones((VT_ROWS - V_HEAD, kt), BF16)
            for h in range(MLA_HEADS):
                vt_s[h * VT_ROWS:(h + 1) * VT_ROWS, pl.ds(r0, kt)] = jnp.concatenate(
                    [vt[h * V_HEAD:(h + 1) * V_HEAD], ones], axis=0)
            return carry
        lax.fori_loop(0, seq // kt, kv_body, 0)

    q0 = pl.multiple_of(i * tq, tq)
    qn = _rms(zq_ref[...].astype(F32), gq_ref[...]).astype(BF16)
    q1 = _dot(qn, wq1_ref[...])
    half = QK_ROPE // 2
    lane = lax.broadcasted_iota(jnp.int32, q1.shape, 1) % HEAD_PAD
    q2 = jnp.where((lane >= QK_NOPE) & (lane < QK_NOPE + half), -pltpu.roll(q1, q1.shape[1] - half, 1),
                   jnp.where((lane >= QK_NOPE + half) & (lane < QK_NOPE + QK_ROPE), pltpu.roll(q1, half, 1), 0.0))
    cs = cos_ref[pl.ds(q0, tq), :]
    sn = sin_ref[pl.ds(q0, tq), :]
    scale = (QK_NOPE + QK_ROPE) ** -0.5 * LOG2_E

    kparts = [slice(j * (seq // ATTN_KEY_PARTS), (j + 1) * (seq // ATTN_KEY_PARTS)) for j in range(ATTN_KEY_PARTS)]

    def scores(h):
        sl = slice(h * HEAD_PAD, (h + 1) * HEAD_PAD)
        qh = ((q1[:, sl] * cs + q2[:, sl] * sn) * scale).astype(BF16)
        return [_dot_nt(k_s[kp, sl], qh) for kp in kparts]

    outs = []
    st_next = scores(0)
    for h in range(MLA_HEADS):
        st = st_next
        if h + 1 < MLA_HEADS:
            st_next = scores(h + 1)
        m = functools.reduce(jnp.maximum, [jnp.max(t, axis=0, keepdims=True) for t in st])
        ps = [jnp.exp2(t - m).astype(BF16) for t in st]
        ol = sum(_dot(vt_s[h * VT_ROWS:(h + 1) * VT_ROWS, kp], p) for kp, p in zip(kparts, ps))
        outs.append(ol[0:V_HEAD] / ol[V_HEAD:V_HEAD + 1])
    o_ref[...] = jnp.concatenate(outs, axis=0).T.astype(o_ref.dtype)


def _attn(z_kv, z_q, cos_t, sin_t, gq, gkv, wq1, wk, wvt):
    bsz, seq, _ = z_kv.shape
    tq = ATTN_Q_TILE
    kw = MLA_HEADS * HEAD_PAD
    return pl.pallas_call(
        _attn_kernel,
        grid=(bsz, seq // tq),
        in_specs=[pl.BlockSpec((None, seq, ZKV_COLS), lambda b, i: (b, 0, 0)),
                  pl.BlockSpec((None, tq, Q_LORA), lambda b, i: (b, i, 0)),
                  pl.BlockSpec((None, seq, LANE), lambda b, i: (b, 0, 0)),
                  pl.BlockSpec((None, seq, LANE), lambda b, i: (b, 0, 0)),
                  _const_spec(gq.shape), _const_spec(gkv.shape), _const_spec(wq1.shape),
                  _const_spec(wk.shape), _const_spec(wvt.shape)],
        out_specs=pl.BlockSpec((None, tq, MLA_HEADS * V_HEAD), lambda b, i: (b, i, 0)),
        out_shape=jax.ShapeDtypeStruct((bsz, seq, MLA_HEADS * V_HEAD), BF16),
        scratch_shapes=[pltpu.VMEM((seq, kw), BF16), pltpu.VMEM((MLA_HEADS * VT_ROWS, seq), BF16)],
        compiler_params=pltpu.CompilerParams(dimension_semantics=("parallel", "arbitrary"),
                                             vmem_limit_bytes=VMEM_LIMIT),
        name="attn",
    )(z_kv, z_q, cos_t, sin_t, gq, gkv, wq1, wk, wvt)


def _mix_kernel(yrw_ref, ymla_ref, ga_ref, gb_ref, x_ref, mod_ref, wr_ref, wm_ref, wo_ref, ln_ref, o_ref):
    gate1 = mod_ref[2:3, :]
    br_rw = _dot(yrw_ref[...], wr_ref[...])
    br_mla = _dot(ymla_ref[...], wm_ref[...])
    mixed = (_sigmoid(ga_ref[...].astype(F32)) * br_rw
             + _sigmoid(gb_ref[...].astype(F32)) * br_mla)
    out = _dot(mixed.astype(BF16), wo_ref[...])
    t = DN_ALPHA * x_ref[...] + (1.0 + gate1) * out
    o_ref[...] = _layer_norm(t, ln_ref[0:1, :], ln_ref[1:2, :])


def _mix(y_rw, y_mla, z_g, x, mod3, wr, wm, wo, ln):
    bsz, seq, _ = x.shape
    tm = ROW_TILE
    row = lambda n: pl.BlockSpec((None, tm, n), lambda b, i: (b, i, 0))
    return pl.pallas_call(
        _mix_kernel,
        grid=(bsz, seq // tm),
        in_specs=[row(RW_WIDTH), row(MLA_HEADS * V_HEAD),
                  pl.BlockSpec((None, tm, D_MODEL), lambda b, i: (b, i, 0)),
                  pl.BlockSpec((None, tm, D_MODEL), lambda b, i: (b, i, 1)),
                  row(D_MODEL),
                  pl.BlockSpec((None, 6, D_MODEL), lambda b, i: (b, 0, 0)),
                  _const_spec(wr.shape), _const_spec(wm.shape), _const_spec(wo.shape),
                  _const_spec(ln.shape)],
        out_specs=row(D_MODEL),
        out_shape=jax.ShapeDtypeStruct((bsz, seq, D_MODEL), F32),
        compiler_params=pltpu.CompilerParams(dimension_semantics=("parallel", "parallel"),
                                             vmem_limit_bytes=VMEM_LIMIT),
        name="mix",
    )(y_rw, y_mla, z_g, z_g, x, mod3, wr, wm, wo, ln)


def _ffn_kernel(x_ref, mod_ref, w1_ref, w2_ref, ln_ref, o_ref):
    shift = mod_ref[3:4, :]
    scale = mod_ref[4:5, :]
    gate2 = mod_ref[5:6, :]
    x1 = x_ref[...]
    h = (x1 * (1.0 + scale) + shift).astype(BF16)
    kc = FFN_HIDDEN_TILE
    acc = jnp.zeros(x1.shape, F32)
    for c in range(D_FF // kc):
        u = jnp.maximum(_dot(h, w1_ref[:, c * kc:(c + 1) * kc]), 0.0)
        acc = acc + _dot((u * u).astype(BF16), w2_ref[c * kc:(c + 1) * kc, :])
    t = DN_ALPHA * x1 + (1.0 + gate2) * acc
    o_ref[...] = _layer_norm(t, ln_ref[0:1, :], ln_ref[1:2, :])


def _ffn(x1, mod3, w1, w2, ln):
    bsz, seq, _ = x1.shape
    tm = ROW_TILE
    row = pl.BlockSpec((None, tm, D_MODEL), lambda b, i: (b, i, 0))
    return pl.pallas_call(
        _ffn_kernel,
        grid=(bsz, seq // tm),
        in_specs=[row, pl.BlockSpec((None, 6, D_MODEL), lambda b, i: (b, 0, 0)),
                  _const_spec(w1.shape), _const_spec(w2.shape), _const_spec(ln.shape)],
        out_specs=row,
        out_shape=jax.ShapeDtypeStruct((bsz, seq, D_MODEL), F32),
        compiler_params=pltpu.CompilerParams(dimension_semantics=("parallel", "parallel"),
                                             vmem_limit_bytes=VMEM_LIMIT),
        name="ffn",
    )(x1, mod3, w1, w2, ln)


def _pad_cols(w, left, total):
    return jnp.pad(w, ((0, 0), (left, total - left - w.shape[1])))


def _inproj_weight(w_in):
    o = 0
    w_rw = w_in[:, o:o + RW_COLS]; o += RW_COLS
    w_q = w_in[:, o:o + Q_LORA]; o += Q_LORA
    w_kv = w_in[:, o:o + KV_LORA]; o += KV_LORA
    w_kr = w_in[:, o:o + QK_ROPE]; o += QK_ROPE
    w_g = w_in[:, o:o + 2 * D_MODEL]
    half = QK_ROPE // 2
    w_kr_rot = jnp.concatenate([-w_kr[:, half:], w_kr[:, :half]], axis=1)
    w_all = jnp.concatenate([w_rw, w_kv, _pad_cols(w_kr, QK_NOPE, LANE), _pad_cols(w_kr_rot, QK_NOPE, LANE),
                             w_q, w_g], axis=1)
    return w_all.astype(BF16)


def _mla_weights(w_uq, w_ukv):
    half = QK_ROPE // 2
    q = w_uq.reshape(Q_LORA, MLA_HEADS, QK_NOPE + QK_ROPE)
    q_nope, q_1, q_2 = q[..., :QK_NOPE], q[..., QK_NOPE:QK_NOPE + half], q[..., QK_NOPE + half:]
    zpad = jnp.zeros((Q_LORA, MLA_HEADS, HEAD_PAD - QK_NOPE - QK_ROPE), F32)
    wq1 = jnp.concatenate([q_nope, q_1, q_2, zpad], axis=-1).reshape(Q_LORA, MLA_HEADS * HEAD_PAD)
    kv = w_ukv.reshape(KV_LORA, MLA_HEADS, QK_NOPE + V_HEAD)
    zhalf = jnp.zeros((KV_LORA, MLA_HEADS, HEAD_PAD - QK_NOPE), F32)
    wk = jnp.concatenate([kv[..., :QK_NOPE], zhalf], axis=-1).reshape(KV_LORA, MLA_HEADS * HEAD_PAD)
    wvt = kv[..., QK_NOPE:].reshape(KV_LORA, MLA_HEADS * V_HEAD).T
    return wq1.astype(BF16), wk.astype(BF16), wvt.astype(BF16)


def _scan_constants(rev):
    rows = np.arange(CHUNK)
    tri = (rows[None, :] >= rows[:, None]) if rev else (rows[None, :] <= rows[:, None])
    ch = np.arange(LANE)
    bd = (ch[:, None] // RW_HEAD) == (ch[None, :] // RW_HEAD)
    return jnp.asarray(tri, BF16), jnp.asarray(bd, BF16)


def _lora_pair(w, d):
    zero = jnp.zeros_like(w[0])
    return jnp.concatenate([zero, w[1]] if d else [w[0], zero], axis=0).astype(BF16)


def kernel(x, c, positions, w_ada, b_ada, w_in, rw_conv, rw_w0, rw_w2, rw_a0, rw_a2, rw_k_k, rw_k_a, rw_r_k, rw_g2, rw_lnx_g, rw_lnx_b, mla_q_norm_g, mla_kv_norm_g, mla_w_uq, mla_w_ukv, w_br_rwkv, w_br_mla, w_out, ln1_g, ln1_b, w_ff1, w_ff2, ln2_g, ln2_b):
    bsz, seq, d_model = x.shape
    assert d_model == D_MODEL and x.dtype == F32 and c.shape == (bsz, D_MODEL)
    assert seq % SCAN_ROWS == 0 and seq % ATTN_Q_TILE == 0 and seq % ROW_TILE == 0
    assert seq % (ATTN_KEY_PARTS * ATTN_KV_TILE) == 0 and (6 * D_MODEL) % ADA_COL_TILE == 0
    cos_t, sin_t = _rope_tables(positions)
    for l in range(DEPTH):
        mod3 = _ada(c, w_ada[l], b_ada[l]).reshape(bsz, 6, D_MODEL)
        z_rw, z_kv, z_q, z_g = _inproj(x, mod3, _inproj_weight(w_in[l]), rw_conv[l])

        zero = jnp.zeros((RW_WIDTH,), F32)
        y_rw = None
        for d in (0, 1):
            vecs = jnp.stack([rw_w0[l, d], rw_a0[l, d], rw_k_k[l], rw_k_a[l], rw_r_k[l],
                              rw_lnx_g[l], rw_lnx_b[l], zero])
            tri, bd = _scan_constants(bool(d))
            y_rw = _scan(z_rw, vecs, _lora_pair(rw_w2[l], d), _lora_pair(rw_a2[l], d), bd, tri,
                         rev=bool(d), y_fwd=y_rw, g2=rw_g2[l].astype(BF16))

        wq1, wk, wvt = _mla_weights(mla_w_uq[l], mla_w_ukv[l])
        y_mla = _attn(z_kv, z_q, cos_t, sin_t, mla_q_norm_g[l].reshape(1, Q_LORA),
                      mla_kv_norm_g[l].reshape(1, KV_LORA), wq1, wk, wvt)

        x = _mix(y_rw, y_mla, z_g, x, mod3, w_br_rwkv[l].astype(BF16), w_br_mla[l].astype(BF16),
                 w_out[l].astype(BF16), jnp.stack([ln1_g[l], ln1_b[l]]))
        x = _ffn(x, mod3, w_ff1[l].astype(BF16), w_ff2[l].astype(BF16), jnp.stack([ln2_g[l], ln2_b[l]]))
    return x
```

```python
import functools

import numpy as np
import jax
import jax.numpy as jnp
from jax import lax
from jax.experimental import pallas as pl
from jax.experimental.pallas import tpu as pltpu

F32 = jnp.float32
BF16 = jnp.bfloat16

D_MODEL = 1024
RW_HEADS = 8
RW_HEAD = 64
RW_WIDTH = RW_HEADS * RW_HEAD
DECAY_LORA = 64
AAA_LORA = 64
GATE_LORA = 128
RW_COLS = 3 * RW_WIDTH + 2 * DECAY_LORA + 2 * AAA_LORA + GATE_LORA
MLA_HEADS = 8
QK_NOPE = 64
QK_ROPE = 32
V_HEAD = 64
Q_LORA = 384
KV_LORA = 256
ROPE_THETA = 10000.0
D_FF = 4 * D_MODEL
LN_EPS = 1e-5
RMS_EPS = 1e-6
GN_EPS = 64e-5
L2_EPS = 1e-12
DEPTH = 1
DN_ALPHA = (2.0 * DEPTH) ** 0.25
LOG2_E = 1.4426950408889634
DECAY_SCALE = 0.6065306597126334

LANE = 128
F32_SUBLANES = 8
BF16_SUBLANES = 16
VMEM_LIMIT = 56 * 1024 * 1024

ROW_TILE = 512
ADA_COL_TILE = 1536
FFN_HIDDEN_TILE = 1024
CHUNK = 64
SCAN_ROWS = 1024
SCAN_WAVE = 4
X_HALO_ROWS = F32_SUBLANES
HEAD_PAD = LANE
ATTN_Q_TILE = 512
ATTN_KV_TILE = 256
ATTN_KEY_PARTS = 4
VT_ROWS = V_HEAD + BF16_SUBLANES
ZKV_COLS = KV_LORA + 2 * LANE


def _dot(a, b):
    return jnp.dot(a, b, preferred_element_type=F32)


def _dot_nt(a, b):
    return lax.dot_general(a, b, (((1,), (1,)), ((), ())), preferred_element_type=F32)


def _dot_tn(a, b):
    return lax.dot_general(a, b, (((0,), (0,)), ((), ())), preferred_element_type=F32)


def _const_spec(shape):
    zeros = (0,) * len(shape)
    return pl.BlockSpec(shape, lambda *_: zeros)


def _layer_norm(t, g, b):
    mu = jnp.mean(t, axis=-1, keepdims=True)
    d = t - mu
    var = jnp.mean(d * d, axis=-1, keepdims=True)
    return d * lax.rsqrt(var + LN_EPS) * g + b


def _ada_kernel(c_ref, w_ref, b_ref, o_ref):
    c = c_ref[...]
    act = c * jax.nn.sigmoid(c)
    o_ref[...] = _dot(act.astype(BF16), w_ref[...].astype(BF16)) + b_ref[...]


def _ada(c, w, b):
    bsz = c.shape[0]
    n = w.shape[1]
    tn = ADA_COL_TILE
    return pl.pallas_call(
        _ada_kernel,
        grid=(n // tn,),
        in_specs=[_const_spec((bsz, D_MODEL)),
                  pl.BlockSpec((D_MODEL, tn), lambda j: (0, j)),
                  pl.BlockSpec((1, tn), lambda j: (0, j))],
        out_specs=pl.BlockSpec((bsz, tn), lambda j: (0, j)),
        out_shape=jax.ShapeDtypeStruct((bsz, n), F32),
        name="ada",
    )(c, w, b.reshape(1, n))


def _rope_kernel(pos_ref, inv_ref, place_ref, one_ref, cos_ref, sin_ref):
    ang = inv_ref[...] * pos_ref[...].astype(F32)

    def table(t):
        hi = t.astype(BF16)
        lo = (t - hi.astype(F32)).astype(BF16)
        return _dot_tn(hi, place_ref[...]) + _dot_tn(lo, place_ref[...])

    cos_ref[...] = table(jnp.cos(ang)) + one_ref[...]
    sin_ref[...] = table(jnp.sin(ang))


def _rope_tables(positions):
    bsz, seq = positions.shape
    half = QK_ROPE // 2
    inv = (ROPE_THETA ** (-np.arange(half, dtype=np.float32) / half)).reshape(half, 1)
    place = np.zeros((half, LANE), np.float32)
    place[np.arange(half), QK_NOPE + np.arange(half)] = 1.0
    place[np.arange(half), QK_NOPE + half + np.arange(half)] = 1.0
    one = np.zeros((1, LANE), np.float32)
    one[0, :QK_NOPE] = 1.0
    spec = pl.BlockSpec((None, seq, LANE), lambda b: (b, 0, 0))
    return pl.pallas_call(
        _rope_kernel,
        grid=(bsz,),
        in_specs=[pl.BlockSpec((None, 1, seq), lambda b: (b, 0, 0)), _const_spec((half, 1)),
                  _const_spec((half, LANE)), _const_spec((1, LANE))],
        out_specs=[spec, spec],
        out_shape=[jax.ShapeDtypeStruct((bsz, seq, LANE), F32)] * 2,
        name="rope",
    )(positions.reshape(bsz, 1, seq), jnp.asarray(inv), jnp.asarray(place, BF16), jnp.asarray(one))


def _inproj_kernel(x_ref, xp_ref, xn_ref, mod_ref, w_ref, cw_ref, zrw_ref, zkv_ref, zq_ref, zg_ref):
    i = pl.program_id(1)
    nb = pl.num_programs(1)
    rows = x_ref.shape[0]
    shift = mod_ref[0:1, :]
    scale = mod_ref[1:2, :]
    h = (x_ref[...] * (1.0 + scale) + shift).astype(BF16)
    col = RW_COLS
    for o_ref in (zkv_ref, zq_ref, zg_ref):
        n = o_ref.shape[-1]
        o_ref[...] = _dot(h, w_ref[:, col:col + n]).astype(o_ref.dtype)
        col += n

    z = _dot(h, w_ref[:, 0:RW_COLS])
    x_halo = jnp.concatenate([xp_ref[...], xn_ref[...]], axis=0)
    z_halo = _dot((x_halo * (1.0 + scale) + shift).astype(BF16), w_ref[:, 0:RW_COLS])
    prev_row = z_halo[X_HALO_ROWS - 1:X_HALO_ROWS, :] * jnp.where(i > 0, 1.0, 0.0)
    next_row = z_halo[X_HALO_ROWS:X_HALO_ROWS + 1, :] * jnp.where(i < nb - 1, 1.0, 0.0)
    row_id = lax.broadcasted_iota(jnp.int32, (rows, 1), 0)
    z_dn = jnp.where(row_id == 0, prev_row, pltpu.roll(z, 1, 0))
    z_up = jnp.where(row_id == rows - 1, next_row, pltpu.roll(z, rows - 1, 0))
    zrw_ref[...] = (cw_ref[0:1, :] * z_dn + cw_ref[1:2, :] * z + cw_ref[2:3, :] * z_up).astype(zrw_ref.dtype)


def _inproj(x, mod3, w_all, conv_w):
    bsz, seq, _ = x.shape
    tm = ROW_TILE
    hpb = tm // X_HALO_ROWS
    n_halo = seq // X_HALO_ROWS
    widths = (RW_COLS, ZKV_COLS, Q_LORA, 2 * D_MODEL)
    return pl.pallas_call(
        _inproj_kernel,
        grid=(bsz, seq // tm),
        in_specs=[pl.BlockSpec((None, tm, D_MODEL), lambda b, i: (b, i, 0)),
                  pl.BlockSpec((None, X_HALO_ROWS, D_MODEL), lambda b, i: (b, jnp.maximum(i * hpb - 1, 0), 0)),
                  pl.BlockSpec((None, X_HALO_ROWS, D_MODEL),
                               lambda b, i: (b, jnp.minimum((i + 1) * hpb, n_halo - 1), 0)),
                  pl.BlockSpec((None, 6, D_MODEL), lambda b, i: (b, 0, 0)),
                  _const_spec(w_all.shape), _const_spec(conv_w.shape)],
        out_specs=[pl.BlockSpec((None, tm, n), lambda b, i: (b, i, 0)) for n in widths],
        out_shape=[jax.ShapeDtypeStruct((bsz, seq, n), BF16) for n in widths],
        compiler_params=pltpu.CompilerParams(dimension_semantics=("parallel", "parallel"),
                                             vmem_limit_bytes=VMEM_LIMIT),
        name="inproj",
    )(x, x, x, mod3, w_all, conv_w)


def _sigmoid(x):
    return 0.5 + 0.5 * jnp.tanh(0.5 * x)


def _head_sums(x, bd_ref):
    return jnp.concatenate([_dot(x[:, p * LANE:(p + 1) * LANE], bd_ref[...])
                            for p in range(RW_WIDTH // LANE)], axis=1)


def _scan_kernel(*refs, rev):
    if rev:
        (zm_ref, vec_ref, w2_ref, a2_ref, bd_ref, tri_ref,
         yf_ref, g2_ref, o_ref,
         at_s, rt_s, bh_s, kh_s, be_s, ke_s, v_s, vsw_s, eg_s, h_s, ya_s, yb_s) = refs
    else:
        (zm_ref, vec_ref, w2_ref, a2_ref, bd_ref, tri_ref,
         o_ref,
         at_s, rt_s, bh_s, kh_s, be_s, ke_s, v_s, vsw_s, eg_s, h_s, ya_s, yb_s) = refs
    rows = zm_ref.shape[0]
    n_chunks = rows // CHUNK
    i = pl.program_id(1)

    @pl.when(i == 0)
    def _():
        h_s[...] = jnp.zeros_like(h_s)

    def token_prep(wave):
        cs = sorted(wave)
        n_r = len(cs) * CHUNK
        rs = slice(cs[0] * CHUNK, cs[0] * CHUNK + n_r)
        zc = zm_ref[rs, :].astype(F32)
        r = zc[:, 0:RW_WIDTH]
        k = zc[:, RW_WIDTH:2 * RW_WIDTH]
        v = zc[:, 2 * RW_WIDTH:3 * RW_WIDTH]
        zw = zc[:, 3 * RW_WIDTH:3 * RW_WIDTH + LANE]
        za = zc[:, 3 * RW_WIDTH + LANE:3 * RW_WIDTH + 2 * LANE]
        w0 = vec_ref[0:1, :]
        a0 = vec_ref[1:2, :]
        k_k = vec_ref[2:3, :]
        k_a = vec_ref[3:4, :]
        v_s[rs, :] = v
        vsw_s[rs, :] = pltpu.roll(v, RW_HEAD, 1)
        w_lin = w0 + _dot(jnp.tanh(zw).astype(BF16), w2_ref[...])
        rate_lin = a0 + _dot(za.astype(BF16), a2_ref[...])
        kkv = k * k_k
        ssq = _head_sums((kkv * kkv).astype(BF16), bd_ref)
        yield
        lw = -DECAY_SCALE * _sigmoid(w_lin)
        lw_hi = lw.astype(BF16)
        lw_lo = (lw - lw_hi.astype(F32)).astype(BF16)
        tri = tri_ref[...]
        g_in = jnp.concatenate([_dot(tri, lw_hi[j * CHUNK:(j + 1) * CHUNK]) + _dot(tri, lw_lo[j * CHUNK:(j + 1) * CHUNK])
                                for j in range(len(cs))], axis=0)
        yield
        rate = _sigmoid(rate_lin)
        kk = kkv * lax.rsqrt(jnp.maximum(ssq, L2_EPS * L2_EPS))
        kd = k * (1.0 + (rate - 1.0) * k_a)
        av = -kk
        bv = kk * rate
        yield
        g_ex = g_in - lw
        last = 0 if rev else CHUNK - 1
        g_tot_rows = [g_in[j * CHUNK + last:j * CHUNK + last + 1, :] for j in range(len(cs))]
        eg_rows = [jnp.exp(t) for t in g_tot_rows]
        for j, c in enumerate(cs):
            eg_s[c] = jnp.broadcast_to(eg_rows[j], (8, RW_WIDTH))
        e_neg = jnp.exp(-g_in)
        e_end = e_neg * jnp.concatenate([jnp.broadcast_to(t, (CHUNK, RW_WIDTH)) for t in eg_rows], axis=0)
        at_s[rs, :] = av * jnp.exp(g_ex)
        rt_s[rs, :] = r * jnp.exp(g_in)
        yield
        bh_s[rs, :] = bv * e_neg
        kh_s[rs, :] = kd * e_neg
        be_s[rs, :] = bv * e_end
        ke_s[rs, :] = kd * e_end

    ri = lax.broadcasted_iota(jnp.int32, (CHUNK, CHUNK), 0)
    ci = lax.broadcasted_iota(jnp.int32, (CHUNK, CHUNK), 1)
    eye = jnp.where(ri == ci, 1.0, 0.0).astype(F32)
    ri2 = lax.broadcasted_iota(jnp.int32, (CHUNK, LANE), 0)
    ci2 = lax.broadcasted_iota(jnp.int32, (CHUNK, LANE), 1) % CHUNK
    m2_strict = (ci2 > ri2) if rev else (ci2 < ri2)
    m2_incl = (ci2 >= ri2) if rev else (ci2 <= ri2)
    zeros_tile = jnp.zeros((CHUNK, LANE), BF16)
    lower = lax.broadcasted_iota(jnp.int32, (CHUNK, LANE), 1) < RW_HEAD
    eye_up = jnp.where(lax.broadcasted_iota(jnp.int32, (CHUNK, LANE), 1)
                       == lax.broadcasted_iota(jnp.int32, (CHUNK, LANE), 0) + RW_HEAD, 1.0, 0.0).astype(F32)
    zeros_half = jnp.zeros((CHUNK, RW_HEAD), F32)

    def pad(t):
        return jnp.concatenate([t, zeros_half], axis=1)

    heads = range(RW_HEADS)
    chunk_order = list(range(n_chunks - 1, -1, -1)) if rev else list(range(n_chunks))
    waves = [chunk_order[w:w + SCAN_WAVE] for w in range(0, n_chunks, SCAN_WAVE)]

    def ld(ref, item):
        c, h = item
        return ref[c * CHUNK:(c + 1) * CHUNK, h * RW_HEAD:(h + 1) * RW_HEAD]

    def ld_v_upper(item):
        c, h = item
        src = v_s if h % 2 else vsw_s
        blk = src[c * CHUNK:(c + 1) * CHUNK, (h // 2) * LANE:(h // 2 + 1) * LANE]
        return jnp.where(lower, 0.0, blk).astype(BF16)

    def ld_decay(item):
        c, h = item
        return eg_s[c][0:1, h * RW_HEAD:(h + 1) * RW_HEAD]

    def chunk_local(items, out):
        n_it = range(len(items))
        rt = [ld(rt_s, it) for it in items]
        lhs = [jnp.concatenate([ld(at_s, items[i]).astype(BF16), rt[i].astype(BF16)], axis=0)
               for i in n_it]
        rhs = [jnp.concatenate([ld(bh_s, it).astype(BF16), ld(kh_s, it).astype(BF16)], axis=0)
               for it in items]
        a_all = [_dot_nt(lhs[i], rhs[i]) for i in n_it]
        yield
        top = [jnp.where(m2_strict, a_all[i][0:CHUNK], 0.0) for i in n_it]
        bot = [jnp.where(m2_incl, a_all[i][CHUNK:], 0.0).astype(BF16) for i in n_it]
        v_up = [ld_v_upper(it) for it in items]
        akv = [_dot(top[i].astype(BF16), jnp.concatenate([zeros_tile, v_up[i]], axis=0)) for i in n_it]
        yield

        z = [jnp.where(lower, top[i], 0.0) + eye_up for i in n_it]
        for _ in range(6):
            z = [_dot(z[i][:, 0:CHUNK].astype(BF16), z[i].astype(BF16)) + jnp.where(lower, 0.0, z[i])
                 for i in n_it]
            yield
        tb =[z[i][:, CHUNK:].astype(BF16) for i in n_it]

        w_m = [(pad(ld(at_s, items[i])) + akv[i]).astype(BF16) for i in n_it]
        pq = [_dot(tb[i], w_m[i]).astype(BF16) for i in n_it]
        pv = [jnp.concatenate([pq[i], v_up[i]], axis=0) for i in n_it]
        yield
        ryq =[_dot(bot[i], pv[i]) for i in n_it]
        ry = [(rt[i] + ryq[i][:, 0:RW_HEAD]).astype(BF16) for i in n_it]
        y0 = [jnp.where(lower, 0.0, ryq[i]) for i in n_it]
        yield
        bke = [jnp.concatenate([ld(be_s, it).astype(BF16), ld(ke_s, it).astype(BF16)], axis=0)
               for it in items]
        mq = [_dot_tn(bke[i], pv[i]) for i in n_it]
        m_m = [(mq[i][:, 0:RW_HEAD] + eye * ld_decay(items[i])).astype(BF16) for i in n_it]
        h0 = [jnp.where(lower, 0.0, mq[i]) for i in n_it]
        ry_m = [jnp.concatenate([ry[i], m_m[i]], axis=0) for i in n_it]
        out.update(ry_m=ry_m, y0=y0, h0=h0)

    hst = [h_s[h] for h in heads]

    def carried(wave, res):
        for ci, c in enumerate(wave):
            for h in heads:
                i = ci * RW_HEADS + h
                yh = _dot(res["ry_m"][i], hst[h].astype(BF16))
                y_dst = ya_s if h % 2 else yb_s
                y_dst[c * CHUNK:(c + 1) * CHUNK, (h // 2) * LANE:(h // 2 + 1) * LANE] = yh[0:CHUNK] + res["y0"][i]
                hst[h] = yh[CHUNK:] + res["h0"][i]
                if h % 4 == 3:
                    yield

    def emit(*gens):
        live = list(gens)
        while live:
            for g in list(live):
                if next(g, StopIteration) is StopIteration:
                    live.remove(g)

    results = [dict() for _ in waves]
    emit(token_prep(waves[0]))
    for w, wave in enumerate(waves):
        gens = [chunk_local([(c, h) for c in wave for h in heads], results[w])]
        if w + 1 < len(waves):
            gens.append(token_prep(waves[w + 1]))
        if w > 0:
            gens.append(carried(waves[w - 1], results[w - 1]))
        emit(*gens)
    emit(carried(waves[-1], results[-1]))
    for h in heads:
        h_s[h] = hst[h]

    y_dir = ya_s[...] + pltpu.roll(yb_s[...], RW_WIDTH - RW_HEAD, 1)
    if not rev:
        o_ref[...] = y_dir
    else:
        r_k = vec_ref[4:5, :]
        lnx_g = vec_ref[5:6, :]
        lnx_b = vec_ref[6:7, :]
        zc = zm_ref[...].astype(F32)
        r = zc[:, 0:RW_WIDTH]
        k = zc[:, RW_WIDTH:2 * RW_WIDTH]
        v = zc[:, 2 * RW_WIDTH:3 * RW_WIDTH]
        y = yf_ref[...] + y_dir
        inv_n = 1.0 / RW_HEAD
        mu = _head_sums(y.astype(BF16), bd_ref) * inv_n
        d = y - mu
        var = _head_sums((d * d).astype(BF16), bd_ref) * inv_n
        yn = d * lax.rsqrt(var + GN_EPS) * lnx_g + lnx_b
        rk = r * k * r_k
        rk_hi = rk.astype(BF16)
        rk_lo = (rk - rk_hi.astype(F32)).astype(BF16)
        bonus = (_head_sums(rk_hi, bd_ref) + _head_sums(rk_lo, bd_ref)) * v
        zg = zc[:, 3 * RW_WIDTH + 2 * LANE:RW_COLS]
        gate = _dot(_sigmoid(zg).astype(BF16), g2_ref[...])
        o_ref[...] = ((yn + bonus) * gate).astype(o_ref.dtype)


def _scan(z_rw, vecs, w2p, a2p, bd, tri, rev, y_fwd=None, g2=None):
    bsz, seq, _ = z_rw.shape
    rows = SCAN_ROWS
    nb = seq // rows
    n_chunks = rows // CHUNK

    def blk(i):
        return nb - 1 - i if rev else i

    in_specs = [
        pl.BlockSpec((None, rows, RW_COLS), lambda b, i: (b, blk(i), 0)),
        _const_spec(vecs.shape), _const_spec(w2p.shape),
        _const_spec(a2p.shape), _const_spec(bd.shape), _const_spec(tri.shape),
    ]
    args = [z_rw, vecs, w2p, a2p, bd, tri]
    scratch = [pltpu.VMEM((rows, RW_WIDTH), F32) for _ in range(8)]
    scratch += [pltpu.VMEM((n_chunks, 8, RW_WIDTH), F32),
                pltpu.VMEM((RW_HEADS, RW_HEAD, LANE), F32),
                pltpu.VMEM((rows, RW_WIDTH), F32), pltpu.VMEM((rows, RW_WIDTH), F32)]
    if rev:
        in_specs += [pl.BlockSpec((None, rows, RW_WIDTH), lambda b, i: (b, blk(i), 0)),
                     _const_spec(g2.shape)]
        args += [y_fwd, g2]
        out_dtype = BF16
    else:
        out_dtype = F32
    return pl.pallas_call(
        functools.partial(_scan_kernel, rev=rev),
        grid=(bsz, nb),
        in_specs=in_specs,
        out_specs=pl.BlockSpec((None, rows, RW_WIDTH), lambda b, i: (b, blk(i), 0)),
        out_shape=jax.ShapeDtypeStruct((bsz, seq, RW_WIDTH), out_dtype),
        scratch_shapes=scratch,
        compiler_params=pltpu.CompilerParams(dimension_semantics=("parallel", "arbitrary"),
                                             vmem_limit_bytes=VMEM_LIMIT),
        name="scan_bwd" if rev else "scan_fwd",
    )(*args)


def _rms(xf, g):
    return xf * lax.rsqrt(jnp.mean(xf * xf, axis=-1, keepdims=True) + RMS_EPS) * g


def _attn_kernel(zkv_ref, zq_ref, cos_ref, sin_ref, gq_ref, gkv_ref, wq1_ref, wk_ref, wvt_ref,
                 o_ref, k_s, vt_s):
    seq = zkv_ref.shape[0]
    tq = zq_ref.shape[0]
    kt = ATTN_KV_TILE
    i = pl.program_id(1)

    @pl.when(i == 0)
    def _():
        def kv_body(t, carry):
            r0 = pl.multiple_of(t * kt, kt)
            zk = zkv_ref[pl.ds(r0, kt), :].astype(F32)
            kvn = _rms(zk[:, 0:KV_LORA], gkv_ref[...]).astype(BF16)
            kn = _dot(kvn, wk_ref[...])
            cs = cos_ref[pl.ds(r0, kt), :]
            sn = sin_ref[pl.ds(r0, kt), :]
            kpe = zk[:, KV_LORA:KV_LORA + LANE] * cs + zk[:, KV_LORA + LANE:KV_LORA + 2 * LANE] * sn
            for h in range(MLA_HEADS):
                sl = slice(h * HEAD_PAD, (h + 1) * HEAD_PAD)
                k_s[pl.ds(r0, kt), sl] = (kn[:, sl] + kpe).astype(BF16)
            vt = _dot_nt(wvt_ref[...], kvn).astype(BF16)
            ones = jnp.ones((VT_ROWS - V_HEAD, kt), BF16)
            for h in range(MLA_HEADS):
                vt_s[h * VT_ROWS:(h + 1) * VT_ROWS, pl.ds(r0, kt)] = jnp.concatenate(
                    [vt[h * V_HEAD:(h + 1) * V_HEAD], ones], axis=0)
            return carry
        lax.fori_loop(0, seq // kt, kv_body, 0)

    q0 = pl.multiple_of(i * tq, tq)
    qn = _rms(zq_ref[...].astype(F32), gq_ref[...]).astype(BF16)
    q1 = _dot(qn, wq1_ref[...])
    half = QK_ROPE // 2
    lane = lax.broadcasted_iota(jnp.int32, q1.shape, 1) % HEAD_PAD
    q2 = jnp.where((lane >= QK_NOPE) & (lane < QK_NOPE + half), -pltpu.roll(q1, q1.shape[1] - half, 1),
                   jnp.where((lane >= QK_NOPE + half) & (lane < QK_NOPE + QK_ROPE), pltpu.roll(q1, half, 1), 0.0))
    cs = cos_ref[pl.ds(q0, tq), :]
    sn = sin_ref[pl.ds(q0, tq), :]
    scale = (QK_NOPE + QK_ROPE) ** -0.5 * LOG2_E

    kparts = [slice(j * (seq // ATTN_KEY_PARTS), (j + 1) * (seq // ATTN_KEY_PARTS)) for j in range(ATTN_KEY_PARTS)]

    qh = [((q1[:, h * HEAD_PAD:(h + 1) * HEAD_PAD] * cs + q2[:, h * HEAD_PAD:(h + 1) * HEAD_PAD] * sn) * scale
           ).astype(BF16) for h in range(MLA_HEADS)]

    def scores(step):
        h, j = divmod(step, ATTN_KEY_PARTS)
        return _dot_nt(k_s[kparts[j], h * HEAD_PAD:(h + 1) * HEAD_PAD], qh[h])

    outs = []
    n_steps = MLA_HEADS * ATTN_KEY_PARTS
    st_next = scores(0)
    for step in range(n_steps):
        h, j = divmod(step, ATTN_KEY_PARTS)
        st = st_next
        if step + 1 < n_steps:
            st_next = scores(step + 1)
        m_part = jnp.max(st, axis=0, keepdims=True)
        if j == 0:
            m_run = m_part
        else:
            m_new = jnp.maximum(m_run, m_part)
            acc = acc * jnp.exp2(m_run - m_new)
            m_run = m_new
        pv = _dot(vt_s[h * VT_ROWS:(h + 1) * VT_ROWS, kparts[j]], jnp.exp2(st - m_run).astype(BF16))
        acc = pv if j == 0 else acc + pv
        if j == ATTN_KEY_PARTS - 1:
            outs.append(acc[0:V_HEAD] / acc[V_HEAD:V_HEAD + 1])
    o_ref[...] = jnp.concatenate(outs, axis=0).T.astype(o_ref.dtype)


def _attn(z_kv, z_q, cos_t, sin_t, gq, gkv, wq1, wk, wvt):
    bsz, seq, _ = z_kv.shape
    tq = ATTN_Q_TILE
    kw = MLA_HEADS * HEAD_PAD
    return pl.pallas_call(
        _attn_kernel,
        grid=(bsz, seq // tq),
        in_specs=[pl.BlockSpec((None, seq, ZKV_COLS), lambda b, i: (b, 0, 0)),
                  pl.BlockSpec((None, tq, Q_LORA), lambda b, i: (b, i, 0)),
                  pl.BlockSpec((None, seq, LANE), lambda b, i: (b, 0, 0)),
                  pl.BlockSpec((None, seq, LANE), lambda b, i: (b, 0, 0)),
                  _const_spec(gq.shape), _const_spec(gkv.shape), _const_spec(wq1.shape),
                  _const_spec(wk.shape), _const_spec(wvt.shape)],
        out_specs=pl.BlockSpec((None, tq, MLA_HEADS * V_HEAD), lambda b, i: (b, i, 0)),
        out_shape=jax.ShapeDtypeStruct((bsz, seq, MLA_HEADS * V_HEAD), BF16),
        scratch_shapes=[pltpu.VMEM((seq, kw), BF16), pltpu.VMEM((MLA_HEADS * VT_ROWS, seq), BF16)],
        compiler_params=pltpu.CompilerParams(dimension_semantics=("parallel", "arbitrary"),
                                             vmem_limit_bytes=VMEM_LIMIT),
        name="attn",
    )(z_kv, z_q, cos_t, sin_t, gq, gkv, wq1, wk, wvt)


def _mix_kernel(yrw_ref, ymla_ref, ga_ref, gb_ref, x_ref, mod_ref, wr_ref, wm_ref, wo_ref, ln_ref, o_ref):
    gate1 = mod_ref[2:3, :]
    br_rw = _dot(yrw_ref[...], wr_ref[...])
    br_mla = _dot(ymla_ref[...], wm_ref[...])
    mixed = (_sigmoid(ga_ref[...].astype(F32)) * br_rw
             + _sigmoid(gb_ref[...].astype(F32)) * br_mla)
    out = _dot(mixed.astype(BF16), wo_ref[...])
    t = DN_ALPHA * x_ref[...] + (1.0 + gate1) * out
    o_ref[...] = _layer_norm(t, ln_ref[0:1, :], ln_ref[1:2, :])


def _mix(y_rw, y_mla, z_g, x, mod3, wr, wm, wo, ln):
    bsz, seq, _ = x.shape
    tm = ROW_TILE
    row = lambda n: pl.BlockSpec((None, tm, n), lambda b, i: (b, i, 0))
    return pl.pallas_call(
        _mix_kernel,
        grid=(bsz, seq // tm),
        in_specs=[row(RW_WIDTH), row(MLA_HEADS * V_HEAD),
                  pl.BlockSpec((None, tm, D_MODEL), lambda b, i: (b, i, 0)),
                  pl.BlockSpec((None, tm, D_MODEL), lambda b, i: (b, i, 1)),
                  row(D_MODEL),
                  pl.BlockSpec((None, 6, D_MODEL), lambda b, i: (b, 0, 0)),
                  _const_spec(wr.shape), _const_spec(wm.shape), _const_spec(wo.shape),
                  _const_spec(ln.shape)],
        out_specs=row(D_MODEL),
        out_shape=jax.ShapeDtypeStruct((bsz, seq, D_MODEL), F32),
        compiler_params=pltpu.CompilerParams(dimension_semantics=("parallel", "parallel"),
                                             vmem_limit_bytes=VMEM_LIMIT),
        name="mix",
    )(y_rw, y_mla, z_g, z_g, x, mod3, wr, wm, wo, ln)


def _ffn_kernel(x_ref, mod_ref, w1_ref, w2_ref, ln_ref, o_ref):
    shift = mod_ref[3:4, :]
    scale = mod_ref[4:5, :]
    gate2 = mod_ref[5:6, :]
    x1 = x_ref[...]
    h = (x1 * (1.0 + scale) + shift).astype(BF16)
    kc = FFN_HIDDEN_TILE
    acc = jnp.zeros(x1.shape, F32)
    for c in range(D_FF // kc):
        u = jnp.maximum(_dot(h, w1_ref[:, c * kc:(c + 1) * kc]), 0.0)
        acc = acc + _dot((u * u).astype(BF16), w2_ref[c * kc:(c + 1) * kc, :])
    t = DN_ALPHA * x1 + (1.0 + gate2) * acc
    o_ref[...] = _layer_norm(t, ln_ref[0:1, :], ln_ref[1:2, :])


def _ffn(x1, mod3, w1, w2, ln):
    bsz, seq, _ = x1.shape
    tm = ROW_TILE
    row = pl.BlockSpec((None, tm, D_MODEL), lambda b, i: (b, i, 0))
    return pl.pallas_call(
        _ffn_kernel,
        grid=(bsz, seq // tm),
        in_specs=[row, pl.BlockSpec((None, 6, D_MODEL), lambda b, i: (b, 0, 0)),
                  _const_spec(w1.shape), _const_spec(w2.shape), _const_spec(ln.shape)],
        out_specs=row,
        out_shape=jax.ShapeDtypeStruct((bsz, seq, D_MODEL), F32),
        compiler_params=pltpu.CompilerParams(dimension_semantics=("parallel", "parallel"),
                                             vmem_limit_bytes=VMEM_LIMIT),
        name="ffn",
    )(x1, mod3, w1, w2, ln)


def _pad_cols(w, left, total):
    return jnp.pad(w, ((0, 0), (left, total - left - w.shape[1])))


def _inproj_weight(w_in):
    o = 0
    w_rw = w_in[:, o:o + RW_COLS]; o += RW_COLS
    w_q = w_in[:, o:o + Q_LORA]; o += Q_LORA
    w_kv = w_in[:, o:o + KV_LORA]; o += KV_LORA
    w_kr = w_in[:, o:o + QK_ROPE]; o += QK_ROPE
    w_g = w_in[:, o:o + 2 * D_MODEL]
    half = QK_ROPE // 2
    w_kr_rot = jnp.concatenate([-w_kr[:, half:], w_kr[:, :half]], axis=1)
    w_all = jnp.concatenate([w_rw, w_kv, _pad_cols(w_kr, QK_NOPE, LANE), _pad_cols(w_kr_rot, QK_NOPE, LANE),
                             w_q, w_g], axis=1)
    return w_all.astype(BF16)


def _mla_weights(w_uq, w_ukv):
    half = QK_ROPE // 2
    q = w_uq.reshape(Q_LORA, MLA_HEADS, QK_NOPE + QK_ROPE)
    q_nope, q_1, q_2 = q[..., :QK_NOPE], q[..., QK_NOPE:QK_NOPE + half], q[..., QK_NOPE + half:]
    zpad = jnp.zeros((Q_LORA, MLA_HEADS, HEAD_PAD - QK_NOPE - QK_ROPE), F32)
    wq1 = jnp.concatenate([q_nope, q_1, q_2, zpad], axis=-1).reshape(Q_LORA, MLA_HEADS * HEAD_PAD)
    kv = w_ukv.reshape(KV_LORA, MLA_HEADS, QK_NOPE + V_HEAD)
    zhalf = jnp.zeros((KV_LORA, MLA_HEADS, HEAD_PAD - QK_NOPE), F32)
    wk = jnp.concatenate([kv[..., :QK_NOPE], zhalf], axis=-1).reshape(KV_LORA, MLA_HEADS * HEAD_PAD)
    wvt = kv[..., QK_NOPE:].reshape(KV_LORA, MLA_HEADS * V_HEAD).T
    return wq1.astype(BF16), wk.astype(BF16), wvt.astype(BF16)


def _scan_constants(rev):
    rows = np.arange(CHUNK)
    tri = (rows[None, :] >= rows[:, None]) if rev else (rows[None, :] <= rows[:, None])
    ch = np.arange(LANE)
    bd = (ch[:, None] // RW_HEAD) == (ch[None, :] // RW_HEAD)
    return jnp.asarray(tri, BF16), jnp.asarray(bd, BF16)


def _lora_pair(w, d):
    zero = jnp.zeros_like(w[0])
    return jnp.concatenate([zero, w[1]] if d else [w[0], zero], axis=0).astype(BF16)


def kernel(x, c, positions, w_ada, b_ada, w_in, rw_conv, rw_w0, rw_w2, rw_a0, rw_a2, rw_k_k, rw_k_a, rw_r_k, rw_g2, rw_lnx_g, rw_lnx_b, mla_q_norm_g, mla_kv_norm_g, mla_w_uq, mla_w_ukv, w_br_rwkv, w_br_mla, w_out, ln1_g, ln1_b, w_ff1, w_ff2, ln2_g, ln2_b):
    bsz, seq, d_model = x.shape
    assert d_model == D_MODEL and x.dtype == F32 and c.shape == (bsz, D_MODEL)
    assert seq % SCAN_ROWS == 0 and seq % ATTN_Q_TILE == 0 and seq % ROW_TILE == 0
    assert seq % (ATTN_KEY_PARTS * ATTN_KV_TILE) == 0 and (6 * D_MODEL) % ADA_COL_TILE == 0
    cos_t, sin_t = _rope_tables(positions)
    for l in range(DEPTH):
        mod3 = _ada(c, w_ada[l], b_ada[l]).reshape(bsz, 6, D_MODEL)
        z_rw, z_kv, z_q, z_g = _inproj(x, mod3, _inproj_weight(w_in[l]), rw_conv[l])

        zero = jnp.zeros((RW_WIDTH,), F32)
        y_rw = None
        for d in (0, 1):
            vecs = jnp.stack([rw_w0[l, d], rw_a0[l, d], rw_k_k[l], rw_k_a[l], rw_r_k[l],
                              rw_lnx_g[l], rw_lnx_b[l], zero])
            tri, bd = _scan_constants(bool(d))
            y_rw = _scan(z_rw, vecs, _lora_pair(rw_w2[l], d), _lora_pair(rw_a2[l], d), bd, tri,
                         rev=bool(d), y_fwd=y_rw, g2=rw_g2[l].astype(BF16))

        wq1, wk, wvt = _mla_weights(mla_w_uq[l], mla_w_ukv[l])
        y_mla = _attn(z_kv, z_q, cos_t, sin_t, mla_q_norm_g[l].reshape(1, Q_LORA),
                      mla_kv_norm_g[l].reshape(1, KV_LORA), wq1, wk, wvt)

        x = _mix(y_rw, y_mla, z_g, x, mod3, w_br_rwkv[l].astype(BF16), w_br_mla[l].astype(BF16),
                 w_out[l].astype(BF16), jnp.stack([ln1_g[l], ln1_b[l]]))
        x = _ffn(x, mod3, w_ff1[l].astype(BF16), w_ff2[l].astype(BF16), jnp.stack([ln2_g[l], ln2_b[l]]))
    return x
```

```python
import functools

import numpy as np
import jax
import jax.numpy as jnp
from jax import lax
from jax.experimental import pallas as pl
from jax.experimental.pallas import tpu as pltpu

F32 = jnp.float32
BF16 = jnp.bfloat16

D_MODEL = 1024
RW_HEADS = 8
RW_HEAD = 64
RW_WIDTH = RW_HEADS * RW_HEAD
DECAY_LORA = 64
AAA_LORA = 64
GATE_LORA = 128
RW_COLS = 3 * RW_WIDTH + 2 * DECAY_LORA + 2 * AAA_LORA + GATE_LORA
MLA_HEADS = 8
QK_NOPE = 64
QK_ROPE = 32
V_HEAD = 64
Q_LORA = 384
KV_LORA = 256
ROPE_THETA = 10000.0
D_FF = 4 * D_MODEL
LN_EPS = 1e-5
RMS_EPS = 1e-6
GN_EPS = 64e-5
L2_EPS = 1e-12
DEPTH = 1
DN_ALPHA = (2.0 * DEPTH) ** 0.25
LOG2_E = 1.4426950408889634
DECAY_SCALE = 0.6065306597126334

LANE = 128
F32_SUBLANES = 8
BF16_SUBLANES = 16
VMEM_LIMIT = 56 * 1024 * 1024

ROW_TILE = 512
ADA_COL_TILE = 1536
FFN_HIDDEN_TILE = 1024
CHUNK = 64
SCAN_ROWS = 1024
SCAN_WAVE = 4
X_HALO_ROWS = F32_SUBLANES
HEAD_PAD = LANE
ATTN_Q_TILE = 1024
ATTN_KV_TILE = 256
ATTN_KEY_PARTS = 4
VT_ROWS = V_HEAD + BF16_SUBLANES
ZKV_COLS = KV_LORA + 2 * LANE


def _dot(a, b):
    return jnp.dot(a, b, preferred_element_type=F32)


def _dot_nt(a, b):
    return lax.dot_general(a, b, (((1,), (1,)), ((), ())), preferred_element_type=F32)


def _dot_tn(a, b):
    return lax.dot_general(a, b, (((0,), (0,)), ((), ())), preferred_element_type=F32)


def _const_spec(shape):
    zeros = (0,) * len(shape)
    return pl.BlockSpec(shape, lambda *_: zeros)


def _layer_norm(t, g, b):
    mu = jnp.mean(t, axis=-1, keepdims=True)
    d = t - mu
    var = jnp.mean(d * d, axis=-1, keepdims=True)
    return d * lax.rsqrt(var + LN_EPS) * g + b


def _ada_kernel(c_ref, w_ref, b_ref, o_ref):
    c = c_ref[...]
    act = c * jax.nn.sigmoid(c)
    o_ref[...] = _dot(act.astype(BF16), w_ref[...].astype(BF16)) + b_ref[...]


def _ada(c, w, b):
    bsz = c.shape[0]
    n = w.shape[1]
    tn = ADA_COL_TILE
    return pl.pallas_call(
        _ada_kernel,
        grid=(n // tn,),
        in_specs=[_const_spec((bsz, D_MODEL)),
                  pl.BlockSpec((D_MODEL, tn), lambda j: (0, j)),
                  pl.BlockSpec((1, tn), lambda j: (0, j))],
        out_specs=pl.BlockSpec((bsz, tn), lambda j: (0, j)),
        out_shape=jax.ShapeDtypeStruct((bsz, n), F32),
        name="ada",
    )(c, w, b.reshape(1, n))


def _rope_kernel(pos_ref, inv_ref, place_ref, one_ref, cos_ref, sin_ref):
    ang = inv_ref[...] * pos_ref[...].astype(F32)

    def table(t):
        hi = t.astype(BF16)
        lo = (t - hi.astype(F32)).astype(BF16)
        return _dot_tn(hi, place_ref[...]) + _dot_tn(lo, place_ref[...])

    cos_ref[...] = table(jnp.cos(ang)) + one_ref[...]
    sin_ref[...] = table(jnp.sin(ang))


def _rope_tables(positions):
    bsz, seq = positions.shape
    half = QK_ROPE // 2
    inv = (ROPE_THETA ** (-np.arange(half, dtype=np.float32) / half)).reshape(half, 1)
    place = np.zeros((half, LANE), np.float32)
    place[np.arange(half), QK_NOPE + np.arange(half)] = 1.0
    place[np.arange(half), QK_NOPE + half + np.arange(half)] = 1.0
    one = np.zeros((1, LANE), np.float32)
    one[0, :QK_NOPE] = 1.0
    spec = pl.BlockSpec((None, seq, LANE), lambda b: (b, 0, 0))
    return pl.pallas_call(
        _rope_kernel,
        grid=(bsz,),
        in_specs=[pl.BlockSpec((None, 1, seq), lambda b: (b, 0, 0)), _const_spec((half, 1)),
                  _const_spec((half, LANE)), _const_spec((1, LANE))],
        out_specs=[spec, spec],
        out_shape=[jax.ShapeDtypeStruct((bsz, seq, LANE), F32)] * 2,
        name="rope",
    )(positions.reshape(bsz, 1, seq), jnp.asarray(inv), jnp.asarray(place, BF16), jnp.asarray(one))


def _inproj_kernel(x_ref, xp_ref, xn_ref, mod_ref, w_ref, cw_ref, zrw_ref, zkv_ref, zq_ref, zg_ref):
    i = pl.program_id(1)
    nb = pl.num_programs(1)
    rows = x_ref.shape[0]
    shift = mod_ref[0:1, :]
    scale = mod_ref[1:2, :]
    h = (x_ref[...] * (1.0 + scale) + shift).astype(BF16)
    col = RW_COLS
    for o_ref in (zkv_ref, zq_ref, zg_ref):
        n = o_ref.shape[-1]
        o_ref[...] = _dot(h, w_ref[:, col:col + n]).astype(o_ref.dtype)
        col += n

    z = _dot(h, w_ref[:, 0:RW_COLS])
    x_halo = jnp.concatenate([xp_ref[...], xn_ref[...]], axis=0)
    z_halo = _dot((x_halo * (1.0 + scale) + shift).astype(BF16), w_ref[:, 0:RW_COLS])
    prev_row = z_halo[X_HALO_ROWS - 1:X_HALO_ROWS, :] * jnp.where(i > 0, 1.0, 0.0)
    next_row = z_halo[X_HALO_ROWS:X_HALO_ROWS + 1, :] * jnp.where(i < nb - 1, 1.0, 0.0)
    row_id = lax.broadcasted_iota(jnp.int32, (rows, 1), 0)
    z_dn = jnp.where(row_id == 0, prev_row, pltpu.roll(z, 1, 0))
    z_up = jnp.where(row_id == rows - 1, next_row, pltpu.roll(z, rows - 1, 0))
    zrw_ref[...] = (cw_ref[0:1, :] * z_dn + cw_ref[1:2, :] * z + cw_ref[2:3, :] * z_up).astype(zrw_ref.dtype)


def _inproj(x, mod3, w_all, conv_w):
    bsz, seq, _ = x.shape
    tm = ROW_TILE
    hpb = tm // X_HALO_ROWS
    n_halo = seq // X_HALO_ROWS
    widths = (RW_COLS, ZKV_COLS, Q_LORA, 2 * D_MODEL)
    return pl.pallas_call(
        _inproj_kernel,
        grid=(bsz, seq // tm),
        in_specs=[pl.BlockSpec((None, tm, D_MODEL), lambda b, i: (b, i, 0)),
                  pl.BlockSpec((None, X_HALO_ROWS, D_MODEL), lambda b, i: (b, jnp.maximum(i * hpb - 1, 0), 0)),
                  pl.BlockSpec((None, X_HALO_ROWS, D_MODEL),
                               lambda b, i: (b, jnp.minimum((i + 1) * hpb, n_halo - 1), 0)),
                  pl.BlockSpec((None, 6, D_MODEL), lambda b, i: (b, 0, 0)),
                  _const_spec(w_all.shape), _const_spec(conv_w.shape)],
        out_specs=[pl.BlockSpec((None, tm, n), lambda b, i: (b, i, 0)) for n in widths],
        out_shape=[jax.ShapeDtypeStruct((bsz, seq, n), BF16) for n in widths],
        compiler_params=pltpu.CompilerParams(dimension_semantics=("parallel", "parallel"),
                                             vmem_limit_bytes=VMEM_LIMIT),
        name="inproj",
    )(x, x, x, mod3, w_all, conv_w)


def _sigmoid(x):
    return 0.5 + 0.5 * jnp.tanh(0.5 * x)


def _head_sums(x, bd_ref):
    return jnp.concatenate([_dot(x[:, p * LANE:(p + 1) * LANE], bd_ref[...])
                            for p in range(RW_WIDTH // LANE)], axis=1)


def _scan_kernel(*refs, rev):
    if rev:
        (zm_ref, vec_ref, w2_ref, a2_ref, bd_ref, tri_ref,
         yf_ref, g2_ref, o_ref,
         at_s, rt_s, bh_s, kh_s, be_s, ke_s, v_s, vsw_s, eg_s, h_s, ya_s, yb_s) = refs
    else:
        (zm_ref, vec_ref, w2_ref, a2_ref, bd_ref, tri_ref,
         o_ref,
         at_s, rt_s, bh_s, kh_s, be_s, ke_s, v_s, vsw_s, eg_s, h_s, ya_s, yb_s) = refs
    rows = zm_ref.shape[0]
    n_chunks = rows // CHUNK
    i = pl.program_id(1)

    @pl.when(i == 0)
    def _():
        h_s[...] = jnp.zeros_like(h_s)

    def token_prep(wave):
        cs = sorted(wave)
        n_r = len(cs) * CHUNK
        rs = slice(cs[0] * CHUNK, cs[0] * CHUNK + n_r)
        zc = zm_ref[rs, :].astype(F32)
        r = zc[:, 0:RW_WIDTH]
        k = zc[:, RW_WIDTH:2 * RW_WIDTH]
        v = zc[:, 2 * RW_WIDTH:3 * RW_WIDTH]
        zw = zc[:, 3 * RW_WIDTH:3 * RW_WIDTH + LANE]
        za = zc[:, 3 * RW_WIDTH + LANE:3 * RW_WIDTH + 2 * LANE]
        w0 = vec_ref[0:1, :]
        a0 = vec_ref[1:2, :]
        k_k = vec_ref[2:3, :]
        k_a = vec_ref[3:4, :]
        v_s[rs, :] = v
        vsw_s[rs, :] = pltpu.roll(v, RW_HEAD, 1)
        w_lin = w0 + _dot(jnp.tanh(zw).astype(BF16), w2_ref[...])
        rate_lin = a0 + _dot(za.astype(BF16), a2_ref[...])
        kkv = k * k_k
        ssq = _head_sums((kkv * kkv).astype(BF16), bd_ref)
        yield
        lw = -DECAY_SCALE * _sigmoid(w_lin)
        lw_hi = lw.astype(BF16)
        lw_lo = (lw - lw_hi.astype(F32)).astype(BF16)
        tri = tri_ref[...]
        g_in = jnp.concatenate([_dot(tri, lw_hi[j * CHUNK:(j + 1) * CHUNK]) + _dot(tri, lw_lo[j * CHUNK:(j + 1) * CHUNK])
                                for j in range(len(cs))], axis=0)
        yield
        rate = _sigmoid(rate_lin)
        kk = kkv * lax.rsqrt(jnp.maximum(ssq, L2_EPS * L2_EPS))
        kd = k * (1.0 + (rate - 1.0) * k_a)
        av = -kk
        bv = kk * rate
        yield
        g_ex = g_in - lw
        last = 0 if rev else CHUNK - 1
        g_tot_rows = [g_in[j * CHUNK + last:j * CHUNK + last + 1, :] for j in range(len(cs))]
        eg_rows = [jnp.exp(t) for t in g_tot_rows]
        for j, c in enumerate(cs):
            eg_s[c] = jnp.broadcast_to(eg_rows[j], (8, RW_WIDTH))
        e_neg = jnp.exp(-g_in)
        e_end = e_neg * jnp.concatenate([jnp.broadcast_to(t, (CHUNK, RW_WIDTH)) for t in eg_rows], axis=0)
        at_s[rs, :] = av * jnp.exp(g_ex)
        rt_s[rs, :] = r * jnp.exp(g_in)
        yield
        bh_s[rs, :] = bv * e_neg
        kh_s[rs, :] = kd * e_neg
        be_s[rs, :] = bv * e_end
        ke_s[rs, :] = kd * e_end

    ri = lax.broadcasted_iota(jnp.int32, (CHUNK, CHUNK), 0)
    ci = lax.broadcasted_iota(jnp.int32, (CHUNK, CHUNK), 1)
    eye = jnp.where(ri == ci, 1.0, 0.0).astype(F32)
    ri2 = lax.broadcasted_iota(jnp.int32, (CHUNK, LANE), 0)
    ci2 = lax.broadcasted_iota(jnp.int32, (CHUNK, LANE), 1) % CHUNK
    m2_strict = (ci2 > ri2) if rev else (ci2 < ri2)
    m2_incl = (ci2 >= ri2) if rev else (ci2 <= ri2)
    zeros_tile = jnp.zeros((CHUNK, LANE), BF16)
    lower = lax.broadcasted_iota(jnp.int32, (CHUNK, LANE), 1) < RW_HEAD
    eye_up = jnp.where(lax.broadcasted_iota(jnp.int32, (CHUNK, LANE), 1)
                       == lax.broadcasted_iota(jnp.int32, (CHUNK, LANE), 0) + RW_HEAD, 1.0, 0.0).astype(F32)
    zeros_half = jnp.zeros((CHUNK, RW_HEAD), F32)

    def pad(t):
        return jnp.concatenate([t, zeros_half], axis=1)

    heads = range(RW_HEADS)
    chunk_order = list(range(n_chunks - 1, -1, -1)) if rev else list(range(n_chunks))
    waves = [chunk_order[w:w + SCAN_WAVE] for w in range(0, n_chunks, SCAN_WAVE)]

    def ld(ref, item):
        c, h = item
        return ref[c * CHUNK:(c + 1) * CHUNK, h * RW_HEAD:(h + 1) * RW_HEAD]

    def ld_v_upper(item):
        c, h = item
        src = v_s if h % 2 else vsw_s
        blk = src[c * CHUNK:(c + 1) * CHUNK, (h // 2) * LANE:(h // 2 + 1) * LANE]
        return jnp.where(lower, 0.0, blk).astype(BF16)

    def ld_decay(item):
        c, h = item
        return eg_s[c][0:1, h * RW_HEAD:(h + 1) * RW_HEAD]

    def chunk_local(items, out):
        n_it = range(len(items))
        rt = [ld(rt_s, it) for it in items]
        lhs = [jnp.concatenate([ld(at_s, items[i]).astype(BF16), rt[i].astype(BF16)], axis=0)
               for i in n_it]
        rhs = [jnp.concatenate([ld(bh_s, it).astype(BF16), ld(kh_s, it).astype(BF16)], axis=0)
               for it in items]
        a_all = [_dot_nt(lhs[i], rhs[i]) for i in n_it]
        yield
        top = [jnp.where(m2_strict, a_all[i][0:CHUNK], 0.0) for i in n_it]
        bot = [jnp.where(m2_incl, a_all[i][CHUNK:], 0.0).astype(BF16) for i in n_it]
        v_up = [ld_v_upper(it) for it in items]
        akv = [_dot(top[i].astype(BF16), jnp.concatenate([zeros_tile, v_up[i]], axis=0)) for i in n_it]
        yield

        z = [jnp.where(lower, top[i], 0.0) + eye_up for i in n_it]
        for _ in range(6):
            z = [_dot(z[i][:, 0:CHUNK].astype(BF16), z[i].astype(BF16)) + jnp.where(lower, 0.0, z[i])
                 for i in n_it]
            yield
        tb =[z[i][:, CHUNK:].astype(BF16) for i in n_it]

        w_m = [(pad(ld(at_s, items[i])) + akv[i]).astype(BF16) for i in n_it]
        pq = [_dot(tb[i], w_m[i]).astype(BF16) for i in n_it]
        pv = [jnp.concatenate([pq[i], v_up[i]], axis=0) for i in n_it]
        yield
        ryq =[_dot(bot[i], pv[i]) for i in n_it]
        ry = [(rt[i] + ryq[i][:, 0:RW_HEAD]).astype(BF16) for i in n_it]
        y0 = [jnp.where(lower, 0.0, ryq[i]) for i in n_it]
        yield
        bke = [jnp.concatenate([ld(be_s, it).astype(BF16), ld(ke_s, it).astype(BF16)], axis=0)
               for it in items]
        mq = [_dot_tn(bke[i], pv[i]) for i in n_it]
        m_m = [(mq[i][:, 0:RW_HEAD] + eye * ld_decay(items[i])).astype(BF16) for i in n_it]
        h0 = [jnp.where(lower, 0.0, mq[i]) for i in n_it]
        ry_m = [jnp.concatenate([ry[i], m_m[i]], axis=0) for i in n_it]
        out.update(ry_m=ry_m, y0=y0, h0=h0)

    hst = [h_s[h] for h in heads]

    def carried(wave, res):
        for ci, c in enumerate(wave):
            for h in heads:
                i = ci * RW_HEADS + h
                yh = _dot(res["ry_m"][i], hst[h].astype(BF16))
                y_dst = ya_s if h % 2 else yb_s
                y_dst[c * CHUNK:(c + 1) * CHUNK, (h // 2) * LANE:(h // 2 + 1) * LANE] = yh[0:CHUNK] + res["y0"][i]
                hst[h] = yh[CHUNK:] + res["h0"][i]
                if h % 4 == 3:
                    yield

    def emit(*gens):
        live = list(gens)
        while live:
            for g in list(live):
                if next(g, StopIteration) is StopIteration:
                    live.remove(g)

    results = [dict() for _ in waves]
    emit(token_prep(waves[0]))
    for w, wave in enumerate(waves):
        gens = [chunk_local([(c, h) for c in wave for h in heads], results[w])]
        if w + 1 < len(waves):
            gens.append(token_prep(waves[w + 1]))
        if w > 0:
            gens.append(carried(waves[w - 1], results[w - 1]))
        emit(*gens)
    emit(carried(waves[-1], results[-1]))
    for h in heads:
        h_s[h] = hst[h]

    y_dir = ya_s[...] + pltpu.roll(yb_s[...], RW_WIDTH - RW_HEAD, 1)
    if not rev:
        o_ref[...] = y_dir
    else:
        r_k = vec_ref[4:5, :]
        lnx_g = vec_ref[5:6, :]
        lnx_b = vec_ref[6:7, :]
        zc = zm_ref[...].astype(F32)
        r = zc[:, 0:RW_WIDTH]
        k = zc[:, RW_WIDTH:2 * RW_WIDTH]
        v = zc[:, 2 * RW_WIDTH:3 * RW_WIDTH]
        y = yf_ref[...] + y_dir
        inv_n = 1.0 / RW_HEAD
        mu = _head_sums(y.astype(BF16), bd_ref) * inv_n
        d = y - mu
        var = _head_sums((d * d).astype(BF16), bd_ref) * inv_n
        yn = d * lax.rsqrt(var + GN_EPS) * lnx_g + lnx_b
        rk = r * k * r_k
        rk_hi = rk.astype(BF16)
        rk_lo = (rk - rk_hi.astype(F32)).astype(BF16)
        bonus = (_head_sums(rk_hi, bd_ref) + _head_sums(rk_lo, bd_ref)) * v
        zg = zc[:, 3 * RW_WIDTH + 2 * LANE:RW_COLS]
        gate = _dot(_sigmoid(zg).astype(BF16), g2_ref[...])
        o_ref[...] = ((yn + bonus) * gate).astype(o_ref.dtype)


def _scan(z_rw, vecs, w2p, a2p, bd, tri, rev, y_fwd=None, g2=None):
    bsz, seq, _ = z_rw.shape
    rows = SCAN_ROWS
    nb = seq // rows
    n_chunks = rows // CHUNK

    def blk(i):
        return nb - 1 - i if rev else i

    in_specs = [
        pl.BlockSpec((None, rows, RW_COLS), lambda b, i: (b, blk(i), 0)),
        _const_spec(vecs.shape), _const_spec(w2p.shape),
        _const_spec(a2p.shape), _const_spec(bd.shape), _const_spec(tri.shape),
    ]
    args = [z_rw, vecs, w2p, a2p, bd, tri]
    scratch = [pltpu.VMEM((rows, RW_WIDTH), F32) for _ in range(8)]
    scratch += [pltpu.VMEM((n_chunks, 8, RW_WIDTH), F32),
                pltpu.VMEM((RW_HEADS, RW_HEAD, LANE), F32),
                pltpu.VMEM((rows, RW_WIDTH), F32), pltpu.VMEM((rows, RW_WIDTH), F32)]
    if rev:
        in_specs += [pl.BlockSpec((None, rows, RW_WIDTH), lambda b, i: (b, blk(i), 0)),
                     _const_spec(g2.shape)]
        args += [y_fwd, g2]
        out_dtype = BF16
    else:
        out_dtype = F32
    return pl.pallas_call(
        functools.partial(_scan_kernel, rev=rev),
        grid=(bsz, nb),
        in_specs=in_specs,
        out_specs=pl.BlockSpec((None, rows, RW_WIDTH), lambda b, i: (b, blk(i), 0)),
        out_shape=jax.ShapeDtypeStruct((bsz, seq, RW_WIDTH), out_dtype),
        scratch_shapes=scratch,
        compiler_params=pltpu.CompilerParams(dimension_semantics=("parallel", "arbitrary"),
                                             vmem_limit_bytes=VMEM_LIMIT),
        name="scan_bwd" if rev else "scan_fwd",
    )(*args)


def _rms(xf, g):
    return xf * lax.rsqrt(jnp.mean(xf * xf, axis=-1, keepdims=True) + RMS_EPS) * g


def _attn_kernel(zkv_ref, zq_ref, cos_ref, sin_ref, gq_ref, gkv_ref, wq1_ref, wk_ref, wvt_ref,
                 o_ref, k_s, vt_s):
    seq = zkv_ref.shape[0]
    tq = zq_ref.shape[0]
    kt = ATTN_KV_TILE
    i = pl.program_id(1)

    @pl.when(i == 0)
    def _():
        def kv_body(t, carry):
            r0 = pl.multiple_of(t * kt, kt)
            zk = zkv_ref[pl.ds(r0, kt), :].astype(F32)
            kvn = _rms(zk[:, 0:KV_LORA], gkv_ref[...]).astype(BF16)
            kn = _dot(kvn, wk_ref[...])
            cs = cos_ref[pl.ds(r0, kt), :]
            sn = sin_ref[pl.ds(r0, kt), :]
            kpe = zk[:, KV_LORA:KV_LORA + LANE] * cs + zk[:, KV_LORA + LANE:KV_LORA + 2 * LANE] * sn
            for h in range(MLA_HEADS):
                sl = slice(h * HEAD_PAD, (h + 1) * HEAD_PAD)
                k_s[pl.ds(r0, kt), sl] = (kn[:, sl] + kpe).astype(BF16)
            vt = _dot_nt(wvt_ref[...], kvn).astype(BF16)
            ones = jnp.ones((VT_ROWS - V_HEAD, kt), BF16)
            for h in range(MLA_HEADS):
                vt_s[h * VT_ROWS:(h + 1) * VT_ROWS, pl.ds(r0, kt)] = jnp.concatenate(
                    [vt[h * V_HEAD:(h + 1) * V_HEAD], ones], axis=0)
            return carry
        lax.fori_loop(0, seq // kt, kv_body, 0)

    q0 = pl.multiple_of(i * tq, tq)
    qn = _rms(zq_ref[...].astype(F32), gq_ref[...]).astype(BF16)
    q1 = _dot(qn, wq1_ref[...])
    half = QK_ROPE // 2
    lane = lax.broadcasted_iota(jnp.int32, q1.shape, 1) % HEAD_PAD
    q2 = jnp.where((lane >= QK_NOPE) & (lane < QK_NOPE + half), -pltpu.roll(q1, q1.shape[1] - half, 1),
                   jnp.where((lane >= QK_NOPE + half) & (lane < QK_NOPE + QK_ROPE), pltpu.roll(q1, half, 1), 0.0))
    cs = cos_ref[pl.ds(q0, tq), :]
    sn = sin_ref[pl.ds(q0, tq), :]
    scale = (QK_NOPE + QK_ROPE) ** -0.5 * LOG2_E

    kparts = [slice(j * (seq // ATTN_KEY_PARTS), (j + 1) * (seq // ATTN_KEY_PARTS)) for j in range(ATTN_KEY_PARTS)]

    def scores(h):
        sl = slice(h * HEAD_PAD, (h + 1) * HEAD_PAD)
        qh = ((q1[:, sl] * cs + q2[:, sl] * sn) * scale).astype(BF16)
        return [_dot_nt(k_s[kp, sl], qh) for kp in kparts]

    outs = []
    st_next = scores(0)
    for h in range(MLA_HEADS):
        st = st_next
        if h + 1 < MLA_HEADS:
            st_next = scores(h + 1)
        m = functools.reduce(jnp.maximum, [jnp.max(t, axis=0, keepdims=True) for t in st])
        ps = [jnp.exp2(t - m).astype(BF16) for t in st]
        ol = sum(_dot(vt_s[h * VT_ROWS:(h + 1) * VT_ROWS, kp], p) for kp, p in zip(kparts, ps))
        outs.append(ol[0:V_HEAD] / ol[V_HEAD:V_HEAD + 1])
    o_ref[...] = jnp.concatenate(outs, axis=0).T.astype(o_ref.dtype)


def _attn(z_kv, z_q, cos_t, sin_t, gq, gkv, wq1, wk, wvt):
    bsz, seq, _ = z_kv.shape
    tq = ATTN_Q_TILE
    kw = MLA_HEADS * HEAD_PAD
    return pl.pallas_call(
        _attn_kernel,
        grid=(bsz, seq // tq),
        in_specs=[pl.BlockSpec((None, seq, ZKV_COLS), lambda b, i: (b, 0, 0)),
                  pl.BlockSpec((None, tq, Q_LORA), lambda b, i: (b, i, 0)),
                  pl.BlockSpec((None, seq, LANE), lambda b, i: (b, 0, 0)),
                  pl.BlockSpec((None, seq, LANE), lambda b, i: (b, 0, 0)),
                  _const_spec(gq.shape), _const_spec(gkv.shape), _const_spec(wq1.shape),
                  _const_spec(wk.shape), _const_spec(wvt.shape)],
        out_specs=pl.BlockSpec((None, tq, MLA_HEADS * V_HEAD), lambda b, i: (b, i, 0)),
        out_shape=jax.ShapeDtypeStruct((bsz, seq, MLA_HEADS * V_HEAD), BF16),
        scratch_shapes=[pltpu.VMEM((seq, kw), BF16), pltpu.VMEM((MLA_HEADS * VT_ROWS, seq), BF16)],
        compiler_params=pltpu.CompilerParams(dimension_semantics=("parallel", "arbitrary"),
                                             vmem_limit_bytes=VMEM_LIMIT),
        name="attn",
    )(z_kv, z_q, cos_t, sin_t, gq, gkv, wq1, wk, wvt)


def _mix_kernel(yrw_ref, ymla_ref, ga_ref, gb_ref, x_ref, mod_ref, wr_ref, wm_ref, wo_ref, ln_ref, o_ref):
    gate1 = mod_ref[2:3, :]
    br_rw = _dot(yrw_ref[...], wr_ref[...])
    br_mla = _dot(ymla_ref[...], wm_ref[...])
    mixed = (_sigmoid(ga_ref[...].astype(F32)) * br_rw
             + _sigmoid(gb_ref[...].astype(F32)) * br_mla)
    out = _dot(mixed.astype(BF16), wo_ref[...])
    t = DN_ALPHA * x_ref[...] + (1.0 + gate1) * out
    o_ref[...] = _layer_norm(t, ln_ref[0:1, :], ln_ref[1:2, :])


def _mix(y_rw, y_mla, z_g, x, mod3, wr, wm, wo, ln):
    bsz, seq, _ = x.shape
    tm = ROW_TILE
    row = lambda n: pl.BlockSpec((None, tm, n), lambda b, i: (b, i, 0))
    return pl.pallas_call(
        _mix_kernel,
        grid=(bsz, seq // tm),
        in_specs=[row(RW_WIDTH), row(MLA_HEADS * V_HEAD),
                  pl.BlockSpec((None, tm, D_MODEL), lambda b, i: (b, i, 0)),
                  pl.BlockSpec((None, tm, D_MODEL), lambda b, i: (b, i, 1)),
                  row(D_MODEL),
                  pl.BlockSpec((None, 6, D_MODEL), lambda b, i: (b, 0, 0)),
                  _const_spec(wr.shape), _const_spec(wm.shape), _const_spec(wo.shape),
                  _const_spec(ln.shape)],
        out_specs=row(D_MODEL),
        out_shape=jax.ShapeDtypeStruct((bsz, seq, D_MODEL), F32),
        compiler_params=pltpu.CompilerParams(dimension_semantics=("parallel", "parallel"),
                                             vmem_limit_bytes=VMEM_LIMIT),
        name="mix",
    )(y_rw, y_mla, z_g, z_g, x, mod3, wr, wm, wo, ln)


def _ffn_kernel(x_ref, mod_ref, w1_ref, w2_ref, ln_ref, o_ref):
    shift = mod_ref[3:4, :]
    scale = mod_ref[4:5, :]
    gate2 = mod_ref[5:6, :]
    x1 = x_ref[...]
    h = (x1 * (1.0 + scale) + shift).astype(BF16)
    kc = FFN_HIDDEN_TILE
    acc = jnp.zeros(x1.shape, F32)
    for c in range(D_FF // kc):
        u = jnp.maximum(_dot(h, w1_ref[:, c * kc:(c + 1) * kc]), 0.0)
        acc = acc + _dot((u * u).astype(BF16), w2_ref[c * kc:(c + 1) * kc, :])
    t = DN_ALPHA * x1 + (1.0 + gate2) * acc
    o_ref[...] = _layer_norm(t, ln_ref[0:1, :], ln_ref[1:2, :])


def _ffn(x1, mod3, w1, w2, ln):
    bsz, seq, _ = x1.shape
    tm = ROW_TILE
    row = pl.BlockSpec((None, tm, D_MODEL), lambda b, i: (b, i, 0))
    return pl.pallas_call(
        _ffn_kernel,
        grid=(bsz, seq // tm),
        in_specs=[row, pl.BlockSpec((None, 6, D_MODEL), lambda b, i: (b, 0, 0)),
                  _const_spec(w1.shape), _const_spec(w2.shape), _const_spec(ln.shape)],
        out_specs=row,
        out_shape=jax.ShapeDtypeStruct((bsz, seq, D_MODEL), F32),
        compiler_params=pltpu.CompilerParams(dimension_semantics=("parallel", "parallel"),
                                             vmem_limit_bytes=VMEM_LIMIT),
        name="ffn",
    )(x1, mod3, w1, w2, ln)


def _pad_cols(w, left, total):
    return jnp.pad(w, ((0, 0), (left, total - left - w.shape[1])))


def _inproj_weight(w_in):
    o = 0
    w_rw = w_in[:, o:o + RW_COLS]; o += RW_COLS
    w_q = w_in[:, o:o + Q_LORA]; o += Q_LORA
    w_kv = w_in[:, o:o + KV_LORA]; o += KV_LORA
    w_kr = w_in[:, o:o + QK_ROPE]; o += QK_ROPE
    w_g = w_in[:, o:o + 2 * D_MODEL]
    half = QK_ROPE // 2
    w_kr_rot = jnp.concatenate([-w_kr[:, half:], w_kr[:, :half]], axis=1)
    w_all = jnp.concatenate([w_rw, w_kv, _pad_cols(w_kr, QK_NOPE, LANE), _pad_cols(w_kr_rot, QK_NOPE, LANE),
                             w_q, w_g], axis=1)
    return w_all.astype(BF16)


def _mla_weights(w_uq, w_ukv):
    half = QK_ROPE // 2
    q = w_uq.reshape(Q_LORA, MLA_HEADS, QK_NOPE + QK_ROPE)
    q_nope, q_1, q_2 = q[..., :QK_NOPE], q[..., QK_NOPE:QK_NOPE + half], q[..., QK_NOPE + half:]
    zpad = jnp.zeros((Q_LORA, MLA_HEADS, HEAD_PAD - QK_NOPE - QK_ROPE), F32)
    wq1 = jnp.concatenate([q_nope, q_1, q_2, zpad], axis=-1).reshape(Q_LORA, MLA_HEADS * HEAD_PAD)
    kv = w_ukv.reshape(KV_LORA, MLA_HEADS, QK_NOPE + V_HEAD)
    zhalf = jnp.zeros((KV_LORA, MLA_HEADS, HEAD_PAD - QK_NOPE), F32)
    wk = jnp.concatenate([kv[..., :QK_NOPE], zhalf], axis=-1).reshape(KV_LORA, MLA_HEADS * HEAD_PAD)
    wvt = kv[..., QK_NOPE:].reshape(KV_LORA, MLA_HEADS * V_HEAD).T
    return wq1.astype(BF16), wk.astype(BF16), wvt.astype(BF16)


def _scan_constants(rev):
    rows = np.arange(CHUNK)
    tri = (rows[None, :] >= rows[:, None]) if rev else (rows[None, :] <= rows[:, None])
    ch = np.arange(LANE)
    bd = (ch[:, None] // RW_HEAD) == (ch[None, :] // RW_HEAD)
    return jnp.asarray(tri, BF16), jnp.asarray(bd, BF16)


def _lora_pair(w, d):
    zero = jnp.zeros_like(w[0])
    return jnp.concatenate([zero, w[1]] if d else [w[0], zero], axis=0).astype(BF16)


def kernel(x, c, positions, w_ada, b_ada, w_in, rw_conv, rw_w0, rw_w2, rw_a0, rw_a2, rw_k_k, rw_k_a, rw_r_k, rw_g2, rw_lnx_g, rw_lnx_b, mla_q_norm_g, mla_kv_norm_g, mla_w_uq, mla_w_ukv, w_br_rwkv, w_br_mla, w_out, ln1_g, ln1_b, w_ff1, w_ff2, ln2_g, ln2_b):
    bsz, seq, d_model = x.shape
    assert d_model == D_MODEL and x.dtype == F32 and c.shape == (bsz, D_MODEL)
    assert seq % SCAN_ROWS == 0 and seq % ATTN_Q_TILE == 0 and seq % ROW_TILE == 0
    assert seq % (ATTN_KEY_PARTS * ATTN_KV_TILE) == 0 and (6 * D_MODEL) % ADA_COL_TILE == 0
    cos_t, sin_t = _rope_tables(positions)
    for l in range(DEPTH):
        mod3 = _ada(c, w_ada[l], b_ada[l]).reshape(bsz, 6, D_MODEL)
        z_rw, z_kv, z_q, z_g = _inproj(x, mod3, _inproj_weight(w_in[l]), rw_conv[l])

        zero = jnp.zeros((RW_WIDTH,), F32)
        y_rw = None
        for d in (0, 1):
            vecs = jnp.stack([rw_w0[l, d], rw_a0[l, d], rw_k_k[l], rw_k_a[l], rw_r_k[l],
                              rw_lnx_g[l], rw_lnx_b[l], zero])
            tri, bd = _scan_constants(bool(d))
            y_rw = _scan(z_rw, vecs, _lora_pair(rw_w2[l], d), _lora_pair(rw_a2[l], d), bd, tri,
                         rev=bool(d), y_fwd=y_rw, g2=rw_g2[l].astype(BF16))

        wq1, wk, wvt = _mla_weights(mla_w_uq[l], mla_w_ukv[l])
        y_mla = _attn(z_kv, z_q, cos_t, sin_t, mla_q_norm_g[l].reshape(1, Q_LORA),
                      mla_kv_norm_g[l].reshape(1, KV_LORA), wq1, wk, wvt)

        x = _mix(y_rw, y_mla, z_g, x, mod3, w_br_rwkv[l].astype(BF16), w_br_mla[l].astype(BF16),
                 w_out[l].astype(BF16), jnp.stack([ln1_g[l], ln1_b[l]]))
        x = _ffn(x, mod3, w_ff1[l].astype(BF16), w_ff2[l].astype(BF16), jnp.stack([ln2_g[l], ln2_b[l]]))
    return x
```

```python
import functools

import numpy as np
import jax
import jax.numpy as jnp
from jax import lax
from jax.experimental import pallas as pl
from jax.experimental.pallas import tpu as pltpu

F32 = jnp.float32
BF16 = jnp.bfloat16

D_MODEL = 1024
RW_HEADS = 8
RW_HEAD = 64
RW_WIDTH = RW_HEADS * RW_HEAD
DECAY_LORA = 64
AAA_LORA = 64
GATE_LORA = 128
RW_COLS = 3 * RW_WIDTH + 2 * DECAY_LORA + 2 * AAA_LORA + GATE_LORA
MLA_HEADS = 8
QK_NOPE = 64
QK_ROPE = 32
V_HEAD = 64
Q_LORA = 384
KV_LORA = 256
ROPE_THETA = 10000.0
D_FF = 4 * D_MODEL
LN_EPS = 1e-5
RMS_EPS = 1e-6
GN_EPS = 64e-5
L2_EPS = 1e-12
DEPTH = 1
DN_ALPHA = (2.0 * DEPTH) ** 0.25
LOG2_E = 1.4426950408889634
DECAY_SCALE = 0.6065306597126334

LANE = 128
F32_SUBLANES = 8
BF16_SUBLANES = 16
VMEM_LIMIT = 56 * 1024 * 1024

ROW_TILE = 512
ADA_COL_TILE = 1536
FFN_HIDDEN_TILE = 1024
CHUNK = 64
SCAN_ROWS = 1024
SCAN_WAVE = 4
X_HALO_ROWS = F32_SUBLANES
HEAD_PAD = LANE
ATTN_Q_TILE = 1024
ATTN_KV_TILE = 256
ATTN_KEY_PARTS = 4
VT_ROWS = V_HEAD + BF16_SUBLANES
ZKV_COLS = KV_LORA + 2 * LANE


def _dot(a, b):
    return jnp.dot(a, b, preferred_element_type=F32)


def _dot_nt(a, b):
    return lax.dot_general(a, b, (((1,), (1,)), ((), ())), preferred_element_type=F32)


def _dot_tn(a, b):
    return lax.dot_general(a, b, (((0,), (0,)), ((), ())), preferred_element_type=F32)


def _const_spec(shape):
    zeros = (0,) * len(shape)
    return pl.BlockSpec(shape, lambda *_: zeros)


def _layer_norm(t, g, b):
    mu = jnp.mean(t, axis=-1, keepdims=True)
    d = t - mu
    var = jnp.mean(d * d, axis=-1, keepdims=True)
    return d * lax.rsqrt(var + LN_EPS) * g + b


def _ada_kernel(c_ref, w_ref, b_ref, o_ref):
    c = c_ref[...]
    act = c * jax.nn.sigmoid(c)
    o_ref[...] = _dot(act.astype(BF16), w_ref[...].astype(BF16)) + b_ref[...]


def _ada(c, w, b):
    bsz = c.shape[0]
    n = w.shape[1]
    tn = ADA_COL_TILE
    return pl.pallas_call(
        _ada_kernel,
        grid=(n // tn,),
        in_specs=[_const_spec((bsz, D_MODEL)),
                  pl.BlockSpec((D_MODEL, tn), lambda j: (0, j)),
                  pl.BlockSpec((1, tn), lambda j: (0, j))],
        out_specs=pl.BlockSpec((bsz, tn), lambda j: (0, j)),
        out_shape=jax.ShapeDtypeStruct((bsz, n), F32),
        name="ada",
    )(c, w, b.reshape(1, n))


def _rope_kernel(pos_ref, inv_ref, place_ref, one_ref, cos_ref, sin_ref):
    ang = inv_ref[...] * pos_ref[...].astype(F32)

    def table(t):
        hi = t.astype(BF16)
        lo = (t - hi.astype(F32)).astype(BF16)
        return _dot_tn(hi, place_ref[...]) + _dot_tn(lo, place_ref[...])

    cos_ref[...] = table(jnp.cos(ang)) + one_ref[...]
    sin_ref[...] = table(jnp.sin(ang))


def _rope_tables(positions):
    bsz, seq = positions.shape
    half = QK_ROPE // 2
    inv = (ROPE_THETA ** (-np.arange(half, dtype=np.float32) / half)).reshape(half, 1)
    place = np.zeros((half, LANE), np.float32)
    place[np.arange(half), QK_NOPE + np.arange(half)] = 1.0
    place[np.arange(half), QK_NOPE + half + np.arange(half)] = 1.0
    one = np.zeros((1, LANE), np.float32)
    one[0, :QK_NOPE] = 1.0
    spec = pl.BlockSpec((None, seq, LANE), lambda b: (b, 0, 0))
    return pl.pallas_call(
        _rope_kernel,
        grid=(bsz,),
        in_specs=[pl.BlockSpec((None, 1, seq), lambda b: (b, 0, 0)), _const_spec((half, 1)),
                  _const_spec((half, LANE)), _const_spec((1, LANE))],
        out_specs=[spec, spec],
        out_shape=[jax.ShapeDtypeStruct((bsz, seq, LANE), F32)] * 2,
        name="rope",
    )(positions.reshape(bsz, 1, seq), jnp.asarray(inv), jnp.asarray(place, BF16), jnp.asarray(one))


def _inproj_kernel(x_ref, xp_ref, xn_ref, mod_ref, w_ref, cw_ref, zrw_ref, zkv_ref, zq_ref, zg_ref):
    i = pl.program_id(1)
    nb = pl.num_programs(1)
    rows = x_ref.shape[0]
    shift = mod_ref[0:1, :]
    scale = mod_ref[1:2, :]
    h = (x_ref[...] * (1.0 + scale) + shift).astype(BF16)
    col = RW_COLS
    for o_ref in (zkv_ref, zq_ref, zg_ref):
        n = o_ref.shape[-1]
        o_ref[...] = _dot(h, w_ref[:, col:col + n]).astype(o_ref.dtype)
        col += n

    z = _dot(h, w_ref[:, 0:RW_COLS])
    x_halo = jnp.concatenate([xp_ref[...], xn_ref[...]], axis=0)
    z_halo = _dot((x_halo * (1.0 + scale) + shift).astype(BF16), w_ref[:, 0:RW_COLS])
    prev_row = z_halo[X_HALO_ROWS - 1:X_HALO_ROWS, :] * jnp.where(i > 0, 1.0, 0.0)
    next_row = z_halo[X_HALO_ROWS:X_HALO_ROWS + 1, :] * jnp.where(i < nb - 1, 1.0, 0.0)
    row_id = lax.broadcasted_iota(jnp.int32, (rows, 1), 0)
    z_dn = jnp.where(row_id == 0, prev_row, pltpu.roll(z, 1, 0))
    z_up = jnp.where(row_id == rows - 1, next_row, pltpu.roll(z, rows - 1, 0))
    zrw_ref[...] = (cw_ref[0:1, :] * z_dn + cw_ref[1:2, :] * z + cw_ref[2:3, :] * z_up).astype(zrw_ref.dtype)


def _inproj(x, mod3, w_all, conv_w):
    bsz, seq, _ = x.shape
    tm = ROW_TILE
    hpb = tm // X_HALO_ROWS
    n_halo = seq // X_HALO_ROWS
    widths = (RW_COLS, ZKV_COLS, Q_LORA, 2 * D_MODEL)
    return pl.pallas_call(
        _inproj_kernel,
        grid=(bsz, seq // tm),
        in_specs=[pl.BlockSpec((None, tm, D_MODEL), lambda b, i: (b, i, 0)),
                  pl.BlockSpec((None, X_HALO_ROWS, D_MODEL), lambda b, i: (b, jnp.maximum(i * hpb - 1, 0), 0)),
                  pl.BlockSpec((None, X_HALO_ROWS, D_MODEL),
                               lambda b, i: (b, jnp.minimum((i + 1) * hpb, n_halo - 1), 0)),
                  pl.BlockSpec((None, 6, D_MODEL), lambda b, i: (b, 0, 0)),
                  _const_spec(w_all.shape), _const_spec(conv_w.shape)],
        out_specs=[pl.BlockSpec((None, tm, n), lambda b, i: (b, i, 0)) for n in widths],
        out_shape=[jax.ShapeDtypeStruct((bsz, seq, n), BF16) for n in widths],
        compiler_params=pltpu.CompilerParams(dimension_semantics=("parallel", "parallel"),
                                             vmem_limit_bytes=VMEM_LIMIT),
        name="inproj",
    )(x, x, x, mod3, w_all, conv_w)


def _sigmoid(x):
    return 0.5 + 0.5 * jnp.tanh(0.5 * x)


def _head_sums(x, bd_ref):
    return jnp.concatenate([_dot(x[:, p * LANE:(p + 1) * LANE], bd_ref[...])
                            for p in range(RW_WIDTH // LANE)], axis=1)


def _scan_kernel(*refs, rev):
    if rev:
        (zm_ref, vec_ref, w2_ref, a2_ref, bd_ref, tri_ref,
         yf_ref, g2_ref, o_ref,
         at_s, rt_s, bh_s, kh_s, be_s, ke_s, v_s, vsw_s, eg_s, h_s, ya_s, yb_s) = refs
    else:
        (zm_ref, vec_ref, w2_ref, a2_ref, bd_ref, tri_ref,
         o_ref,
         at_s, rt_s, bh_s, kh_s, be_s, ke_s, v_s, vsw_s, eg_s, h_s, ya_s, yb_s) = refs
    rows = zm_ref.shape[0]
    n_chunks = rows // CHUNK
    i = pl.program_id(1)

    @pl.when(i == 0)
    def _():
        h_s[...] = jnp.zeros_like(h_s)

    def token_prep(wave):
        cs = sorted(wave)
        n_r = len(cs) * CHUNK
        rs = slice(cs[0] * CHUNK, cs[0] * CHUNK + n_r)
        zc = zm_ref[rs, :].astype(F32)
        r = zc[:, 0:RW_WIDTH]
        k = zc[:, RW_WIDTH:2 * RW_WIDTH]
        v = zc[:, 2 * RW_WIDTH:3 * RW_WIDTH]
        zw = zc[:, 3 * RW_WIDTH:3 * RW_WIDTH + LANE]
        za = zc[:, 3 * RW_WIDTH + LANE:3 * RW_WIDTH + 2 * LANE]
        w0 = vec_ref[0:1, :]
        a0 = vec_ref[1:2, :]
        k_k = vec_ref[2:3, :]
        k_a = vec_ref[3:4, :]
        v_s[rs, :] = v
        vsw_s[rs, :] = pltpu.roll(v, RW_HEAD, 1)
        w_lin = w0 + _dot(jnp.tanh(zw).astype(BF16), w2_ref[...])
        rate_lin = a0 + _dot(za.astype(BF16), a2_ref[...])
        kkv = k * k_k
        ssq = _head_sums((kkv * kkv).astype(BF16), bd_ref)
        yield
        lw = -DECAY_SCALE * _sigmoid(w_lin)
        lw_hi = lw.astype(BF16)
        lw_lo = (lw - lw_hi.astype(F32)).astype(BF16)
        tri = tri_ref[...]
        g_in = jnp.concatenate([_dot(tri, lw_hi[j * CHUNK:(j + 1) * CHUNK]) + _dot(tri, lw_lo[j * CHUNK:(j + 1) * CHUNK])
                                for j in range(len(cs))], axis=0)
        yield
        rate = _sigmoid(rate_lin)
        kk = kkv * lax.rsqrt(jnp.maximum(ssq, L2_EPS * L2_EPS))
        kd = k * (1.0 + (rate - 1.0) * k_a)
        av = -kk
        bv = kk * rate
        yield
        g_ex = g_in - lw
        last = 0 if rev else CHUNK - 1
        g_tot_rows = [g_in[j * CHUNK + last:j * CHUNK + last + 1, :] for j in range(len(cs))]
        eg_rows = [jnp.exp(t) for t in g_tot_rows]
        for j, c in enumerate(cs):
            eg_s[c] = jnp.broadcast_to(eg_rows[j], (8, RW_WIDTH))
        e_neg = jnp.exp(-g_in)
        e_end = e_neg * jnp.concatenate([jnp.broadcast_to(t, (CHUNK, RW_WIDTH)) for t in eg_rows], axis=0)
        at_s[rs, :] = av * jnp.exp(g_ex)
        rt_s[rs, :] = r * jnp.exp(g_in)
        yield
        bh_s[rs, :] = bv * e_neg
        kh_s[rs, :] = kd * e_neg
        be_s[rs, :] = bv * e_end
        ke_s[rs, :] = kd * e_end

    ri = lax.broadcasted_iota(jnp.int32, (CHUNK, CHUNK), 0)
    ci = lax.broadcasted_iota(jnp.int32, (CHUNK, CHUNK), 1)
    eye = jnp.where(ri == ci, 1.0, 0.0).astype(F32)
    ri2 = lax.broadcasted_iota(jnp.int32, (CHUNK, LANE), 0)
    ci2 = lax.broadcasted_iota(jnp.int32, (CHUNK, LANE), 1) % CHUNK
    m2_strict = (ci2 > ri2) if rev else (ci2 < ri2)
    m2_incl = (ci2 >= ri2) if rev else (ci2 <= ri2)
    zeros_tile = jnp.zeros((CHUNK, LANE), BF16)
    lower = lax.broadcasted_iota(jnp.int32, (CHUNK, LANE), 1) < RW_HEAD
    eye_up = jnp.where(lax.broadcasted_iota(jnp.int32, (CHUNK, LANE), 1)
                       == lax.broadcasted_iota(jnp.int32, (CHUNK, LANE), 0) + RW_HEAD, 1.0, 0.0).astype(F32)
    zeros_half = jnp.zeros((CHUNK, RW_HEAD), F32)

    def pad(t):
        return jnp.concatenate([t, zeros_half], axis=1)

    heads = range(RW_HEADS)
    chunk_order = list(range(n_chunks - 1, -1, -1)) if rev else list(range(n_chunks))
    waves = [chunk_order[w:w + SCAN_WAVE] for w in range(0, n_chunks, SCAN_WAVE)]

    def ld(ref, item):
        c, h = item
        return ref[c * CHUNK:(c + 1) * CHUNK, h * RW_HEAD:(h + 1) * RW_HEAD]

    def ld_v_upper(item):
        c, h = item
        src = v_s if h % 2 else vsw_s
        blk = src[c * CHUNK:(c + 1) * CHUNK, (h // 2) * LANE:(h // 2 + 1) * LANE]
        return jnp.where(lower, 0.0, blk).astype(BF16)

    def ld_decay(item):
        c, h = item
        return eg_s[c][0:1, h * RW_HEAD:(h + 1) * RW_HEAD]

    def chunk_local(items, out):
        n_it = range(len(items))
        rt = [ld(rt_s, it) for it in items]
        lhs = [jnp.concatenate([ld(at_s, items[i]).astype(BF16), rt[i].astype(BF16)], axis=0)
               for i in n_it]
        rhs = [jnp.concatenate([ld(bh_s, it).astype(BF16), ld(kh_s, it).astype(BF16)], axis=0)
               for it in items]
        a_all = [_dot_nt(lhs[i], rhs[i]) for i in n_it]
        yield
        top = [jnp.where(m2_strict, a_all[i][0:CHUNK], 0.0) for i in n_it]
        bot = [jnp.where(m2_incl, a_all[i][CHUNK:], 0.0).astype(BF16) for i in n_it]
        v_up = [ld_v_upper(it) for it in items]
        akv = [_dot(top[i].astype(BF16), jnp.concatenate([zeros_tile, v_up[i]], axis=0)) for i in n_it]
        yield

        z = [jnp.where(lower, top[i], 0.0) + eye_up for i in n_it]
        for _ in range(6):
            z = [_dot(z[i][:, 0:CHUNK].astype(BF16), z[i].astype(BF16)) + jnp.where(lower, 0.0, z[i])
                 for i in n_it]
            yield
        tb =[z[i][:, CHUNK:].astype(BF16) for i in n_it]

        w_m = [(pad(ld(at_s, items[i])) + akv[i]).astype(BF16) for i in n_it]
        pq = [_dot(tb[i], w_m[i]).astype(BF16) for i in n_it]
        pv = [jnp.concatenate([pq[i], v_up[i]], axis=0) for i in n_it]
        yield
        ryq =[_dot(bot[i], pv[i]) for i in n_it]
        ry = [(rt[i] + ryq[i][:, 0:RW_HEAD]).astype(BF16) for i in n_it]
        y0 = [jnp.where(lower, 0.0, ryq[i]) for i in n_it]
        yield
        bke = [jnp.concatenate([ld(be_s, it).astype(BF16), ld(ke_s, it).astype(BF16)], axis=0)
               for it in items]
        mq = [_dot_tn(bke[i], pv[i]) for i in n_it]
        m_m = [(mq[i][:, 0:RW_HEAD] + eye * ld_decay(items[i])).astype(BF16) for i in n_it]
        h0 = [jnp.where(lower, 0.0, mq[i]) for i in n_it]
        ry_m = [jnp.concatenate([ry[i], m_m[i]], axis=0) for i in n_it]
        out.update(ry_m=ry_m, y0=y0, h0=h0)

    hst = [h_s[h] for h in heads]

    def carried(wave, res):
        for ci, c in enumerate(wave):
            for h in heads:
                i = ci * RW_HEADS + h
                yh = _dot(res["ry_m"][i], hst[h].astype(BF16))
                y_dst = ya_s if h % 2 else yb_s
                y_dst[c * CHUNK:(c + 1) * CHUNK, (h // 2) * LANE:(h // 2 + 1) * LANE] = yh[0:CHUNK] + res["y0"][i]
                hst[h] = yh[CHUNK:] + res["h0"][i]
                if h % 4 == 3:
                    yield

    def emit(*gens):
        live = list(gens)
        while live:
            for g in list(live):
                if next(g, StopIteration) is StopIteration:
                    live.remove(g)

    results = [dict() for _ in waves]
    emit(token_prep(waves[0]))
    for w, wave in enumerate(waves):
        gens = [chunk_local([(c, h) for c in wave for h in heads], results[w])]
        if w + 1 < len(waves):
            gens.append(token_prep(waves[w + 1]))
        if w > 0:
            gens.append(carried(waves[w - 1], results[w - 1]))
        emit(*gens)
    emit(carried(waves[-1], results[-1]))
    for h in heads:
        h_s[h] = hst[h]

    y_dir = ya_s[...] + pltpu.roll(yb_s[...], RW_WIDTH - RW_HEAD, 1)
    if not rev:
        o_ref[...] = y_dir
    else:
        r_k = vec_ref[4:5, :]
        lnx_g = vec_ref[5:6, :]
        lnx_b = vec_ref[6:7, :]
        zc = zm_ref[...].astype(F32)
        r = zc[:, 0:RW_WIDTH]
        k = zc[:, RW_WIDTH:2 * RW_WIDTH]
        v = zc[:, 2 * RW_WIDTH:3 * RW_WIDTH]
        y = yf_ref[...] + y_dir
        inv_n = 1.0 / RW_HEAD
        mu = _head_sums(y.astype(BF16), bd_ref) * inv_n
        d = y - mu
        var = _head_sums((d * d).astype(BF16), bd_ref) * inv_n
        yn = d * lax.rsqrt(var + GN_EPS) * lnx_g + lnx_b
        rk = r * k * r_k
        rk_hi = rk.astype(BF16)
        rk_lo = (rk - rk_hi.astype(F32)).astype(BF16)
        bonus = (_head_sums(rk_hi, bd_ref) + _head_sums(rk_lo, bd_ref)) * v
        zg = zc[:, 3 * RW_WIDTH + 2 * LANE:RW_COLS]
        gate = _dot(_sigmoid(zg).astype(BF16), g2_ref[...])
        o_ref[...] = ((yn + bonus) * gate).astype(o_ref.dtype)


def _scan(z_rw, vecs, w2p, a2p, bd, tri, rev, y_fwd=None, g2=None):
    bsz, seq, _ = z_rw.shape
    rows = SCAN_ROWS
    nb = seq // rows
    n_chunks = rows // CHUNK

    def blk(i):
        return nb - 1 - i if rev else i

    in_specs = [
        pl.BlockSpec((None, rows, RW_COLS), lambda b, i: (b, blk(i), 0)),
        _const_spec(vecs.shape), _const_spec(w2p.shape),
        _const_spec(a2p.shape), _const_spec(bd.shape), _const_spec(tri.shape),
    ]
    args = [z_rw, vecs, w2p, a2p, bd, tri]
    scratch = [pltpu.VMEM((rows, RW_WIDTH), F32) for _ in range(8)]
    scratch += [pltpu.VMEM((n_chunks, 8, RW_WIDTH), F32),
                pltpu.VMEM((RW_HEADS, RW_HEAD, LANE), F32),
                pltpu.VMEM((rows, RW_WIDTH), F32), pltpu.VMEM((rows, RW_WIDTH), F32)]
    if rev:
        in_specs += [pl.BlockSpec((None, rows, RW_WIDTH), lambda b, i: (b, blk(i), 0)),
                     _const_spec(g2.shape)]
        args += [y_fwd, g2]
        out_dtype = BF16
    else:
        out_dtype = F32
    return pl.pallas_call(
        functools.partial(_scan_kernel, rev=rev),
        grid=(bsz, nb),
        in_specs=in_specs,
        out_specs=pl.BlockSpec((None, rows, RW_WIDTH), lambda b, i: (b, blk(i), 0)),
        out_shape=jax.ShapeDtypeStruct((bsz, seq, RW_WIDTH), out_dtype),
        scratch_shapes=scratch,
        compiler_params=pltpu.CompilerParams(dimension_semantics=("parallel", "arbitrary"),
                                             vmem_limit_bytes=VMEM_LIMIT),
        name="scan_bwd" if rev else "scan_fwd",
    )(*args)


def _rms(xf, g):
    return xf * lax.rsqrt(jnp.mean(xf * xf, axis=-1, keepdims=True) + RMS_EPS) * g


def _kvproj_kernel(zkv_ref, cos_ref, sin_ref, gkv_ref, wk_ref, wvt_ref, k_ref, vt_ref):
    kt = zkv_ref.shape[0]
    zk = zkv_ref[...].astype(F32)
    kvn = _rms(zk[:, 0:KV_LORA], gkv_ref[...]).astype(BF16)
    kn = _dot(kvn, wk_ref[...])
    kpe = (zk[:, KV_LORA:KV_LORA + LANE] * cos_ref[...]
           + zk[:, KV_LORA + LANE:KV_LORA + 2 * LANE] * sin_ref[...])
    for h in range(MLA_HEADS):
        sl = slice(h * HEAD_PAD, (h + 1) * HEAD_PAD)
        k_ref[:, sl] = (kn[:, sl] + kpe).astype(BF16)
    vt = _dot_nt(wvt_ref[...], kvn).astype(BF16)
    ones = jnp.ones((VT_ROWS - V_HEAD, kt), BF16)
    for h in range(MLA_HEADS):
        vt_ref[h * VT_ROWS:(h + 1) * VT_ROWS, :] = jnp.concatenate([vt[h * V_HEAD:(h + 1) * V_HEAD], ones], axis=0)


def _kvproj(z_kv, cos_t, sin_t, gkv, wk, wvt):
    bsz, seq, _ = z_kv.shape
    kt = ROW_TILE
    kw = MLA_HEADS * HEAD_PAD
    return pl.pallas_call(
        _kvproj_kernel,
        grid=(bsz, seq // kt),
        in_specs=[pl.BlockSpec((None, kt, ZKV_COLS), lambda b, i: (b, i, 0)),
                  pl.BlockSpec((None, kt, LANE), lambda b, i: (b, i, 0)),
                  pl.BlockSpec((None, kt, LANE), lambda b, i: (b, i, 0)),
                  _const_spec(gkv.shape), _const_spec(wk.shape), _const_spec(wvt.shape)],
        out_specs=[pl.BlockSpec((None, kt, kw), lambda b, i: (b, i, 0)),
                   pl.BlockSpec((None, MLA_HEADS * VT_ROWS, kt), lambda b, i: (b, 0, i))],
        out_shape=[jax.ShapeDtypeStruct((bsz, seq, kw), BF16),
                   jax.ShapeDtypeStruct((bsz, MLA_HEADS * VT_ROWS, seq), BF16)],
        compiler_params=pltpu.CompilerParams(dimension_semantics=("parallel", "parallel"),
                                             vmem_limit_bytes=VMEM_LIMIT),
        name="kvproj",
    )(z_kv, cos_t, sin_t, gkv, wk, wvt)


def _attn_kernel(k_s, vt_s, zq_ref, cos_ref, sin_ref, gq_ref, wq1_ref, o_ref):
    seq = k_s.shape[0]
    qn = _rms(zq_ref[...].astype(F32), gq_ref[...]).astype(BF16)
    q1 = _dot(qn, wq1_ref[...])
    half = QK_ROPE // 2
    lane = lax.broadcasted_iota(jnp.int32, q1.shape, 1) % HEAD_PAD
    q2 = jnp.where((lane >= QK_NOPE) & (lane < QK_NOPE + half), -pltpu.roll(q1, q1.shape[1] - half, 1),
                   jnp.where((lane >= QK_NOPE + half) & (lane < QK_NOPE + QK_ROPE), pltpu.roll(q1, half, 1), 0.0))
    cs = cos_ref[...]
    sn = sin_ref[...]
    scale = (QK_NOPE + QK_ROPE) ** -0.5 * LOG2_E

    kparts = [slice(j * (seq // ATTN_KEY_PARTS), (j + 1) * (seq // ATTN_KEY_PARTS)) for j in range(ATTN_KEY_PARTS)]

    def scores(h):
        sl = slice(h * HEAD_PAD, (h + 1) * HEAD_PAD)
        qh = ((q1[:, sl] * cs + q2[:, sl] * sn) * scale).astype(BF16)
        return [_dot_nt(k_s[kp, sl], qh) for kp in kparts]

    outs = []
    st_next = scores(0)
    for h in range(MLA_HEADS):
        st = st_next
        if h + 1 < MLA_HEADS:
            st_next = scores(h + 1)
        m = functools.reduce(jnp.maximum, [jnp.max(t, axis=0, keepdims=True) for t in st])
        ps = [jnp.exp2(t - m).astype(BF16) for t in st]
        ol = sum(_dot(vt_s[h * VT_ROWS:(h + 1) * VT_ROWS, kp], p) for kp, p in zip(kparts, ps))
        outs.append(ol[0:V_HEAD] / ol[V_HEAD:V_HEAD + 1])
    o_ref[...] = jnp.concatenate(outs, axis=0).T.astype(o_ref.dtype)


def _attn(z_kv, z_q, cos_t, sin_t, gq, gkv, wq1, wk, wvt):
    bsz, seq, _ = z_kv.shape
    tq = ATTN_Q_TILE
    kw = MLA_HEADS * HEAD_PAD
    keys, v_t = _kvproj(z_kv, cos_t, sin_t, gkv, wk, wvt)
    return pl.pallas_call(
        _attn_kernel,
        grid=(bsz, seq // tq),
        in_specs=[pl.BlockSpec((None, seq, kw), lambda b, i: (b, 0, 0)),
                  pl.BlockSpec((None, MLA_HEADS * VT_ROWS, seq), lambda b, i: (b, 0, 0)),
                  pl.BlockSpec((None, tq, Q_LORA), lambda b, i: (b, i, 0)),
                  pl.BlockSpec((None, tq, LANE), lambda b, i: (b, i, 0)),
                  pl.BlockSpec((None, tq, LANE), lambda b, i: (b, i, 0)),
                  _const_spec(gq.shape), _const_spec(wq1.shape)],
        out_specs=pl.BlockSpec((None, tq, MLA_HEADS * V_HEAD), lambda b, i: (b, i, 0)),
        out_shape=jax.ShapeDtypeStruct((bsz, seq, MLA_HEADS * V_HEAD), BF16),
        compiler_params=pltpu.CompilerParams(dimension_semantics=("parallel", "parallel"),
                                             vmem_limit_bytes=VMEM_LIMIT),
        name="attn",
    )(keys, v_t, z_q, cos_t, sin_t, gq, wq1)


def _mix_kernel(yrw_ref, ymla_ref, ga_ref, gb_ref, x_ref, mod_ref, wr_ref, wm_ref, wo_ref, ln_ref, o_ref):
    gate1 = mod_ref[2:3, :]
    br_rw = _dot(yrw_ref[...], wr_ref[...])
    br_mla = _dot(ymla_ref[...], wm_ref[...])
    mixed = (_sigmoid(ga_ref[...].astype(F32)) * br_rw
             + _sigmoid(gb_ref[...].astype(F32)) * br_mla)
    out = _dot(mixed.astype(BF16), wo_ref[...])
    t = DN_ALPHA * x_ref[...] + (1.0 + gate1) * out
    o_ref[...] = _layer_norm(t, ln_ref[0:1, :], ln_ref[1:2, :])


def _mix(y_rw, y_mla, z_g, x, mod3, wr, wm, wo, ln):
    bsz, seq, _ = x.shape
    tm = ROW_TILE
    row = lambda n: pl.BlockSpec((None, tm, n), lambda b, i: (b, i, 0))
    return pl.pallas_call(
        _mix_kernel,
        grid=(bsz, seq // tm),
        in_specs=[row(RW_WIDTH), row(MLA_HEADS * V_HEAD),
                  pl.BlockSpec((None, tm, D_MODEL), lambda b, i: (b, i, 0)),
                  pl.BlockSpec((None, tm, D_MODEL), lambda b, i: (b, i, 1)),
                  row(D_MODEL),
                  pl.BlockSpec((None, 6, D_MODEL), lambda b, i: (b, 0, 0)),
                  _const_spec(wr.shape), _const_spec(wm.shape), _const_spec(wo.shape),
                  _const_spec(ln.shape)],
        out_specs=row(D_MODEL),
        out_shape=jax.ShapeDtypeStruct((bsz, seq, D_MODEL), F32),
        compiler_params=pltpu.CompilerParams(dimension_semantics=("parallel", "parallel"),
                                             vmem_limit_bytes=VMEM_LIMIT),
        name="mix",
    )(y_rw, y_mla, z_g, z_g, x, mod3, wr, wm, wo, ln)


def _ffn_kernel(x_ref, mod_ref, w1_ref, w2_ref, ln_ref, o_ref):
    shift = mod_ref[3:4, :]
    scale = mod_ref[4:5, :]
    gate2 = mod_ref[5:6, :]
    x1 = x_ref[...]
    h = (x1 * (1.0 + scale) + shift).astype(BF16)
    kc = FFN_HIDDEN_TILE
    acc = jnp.zeros(x1.shape, F32)
    for c in range(D_FF // kc):
        u = jnp.maximum(_dot(h, w1_ref[:, c * kc:(c + 1) * kc]), 0.0)
        acc = acc + _dot((u * u).astype(BF16), w2_ref[c * kc:(c + 1) * kc, :])
    t = DN_ALPHA * x1 + (1.0 + gate2) * acc
    o_ref[...] = _layer_norm(t, ln_ref[0:1, :], ln_ref[1:2, :])


def _ffn(x1, mod3, w1, w2, ln):
    bsz, seq, _ = x1.shape
    tm = ROW_TILE
    row = pl.BlockSpec((None, tm, D_MODEL), lambda b, i: (b, i, 0))
    return pl.pallas_call(
        _ffn_kernel,
        grid=(bsz, seq // tm),
        in_specs=[row, pl.BlockSpec((None, 6, D_MODEL), lambda b, i: (b, 0, 0)),
                  _const_spec(w1.shape), _const_spec(w2.shape), _const_spec(ln.shape)],
        out_specs=row,
        out_shape=jax.ShapeDtypeStruct((bsz, seq, D_MODEL), F32),
        compiler_params=pltpu.CompilerParams(dimension_semantics=("parallel", "parallel"),
                                             vmem_limit_bytes=VMEM_LIMIT),
        name="ffn",
    )(x1, mod3, w1, w2, ln)


def _pad_cols(w, left, total):
    return jnp.pad(w, ((0, 0), (left, total - left - w.shape[1])))


def _inproj_weight(w_in):
    o = 0
    w_rw = w_in[:, o:o + RW_COLS]; o += RW_COLS
    w_q = w_in[:, o:o + Q_LORA]; o += Q_LORA
    w_kv = w_in[:, o:o + KV_LORA]; o += KV_LORA
    w_kr = w_in[:, o:o + QK_ROPE]; o += QK_ROPE
    w_g = w_in[:, o:o + 2 * D_MODEL]
    half = QK_ROPE // 2
    w_kr_rot = jnp.concatenate([-w_kr[:, half:], w_kr[:, :half]], axis=1)
    w_all = jnp.concatenate([w_rw, w_kv, _pad_cols(w_kr, QK_NOPE, LANE), _pad_cols(w_kr_rot, QK_NOPE, LANE),
                             w_q, w_g], axis=1)
    return w_all.astype(BF16)


def _mla_weights(w_uq, w_ukv):
    half = QK_ROPE // 2
    q = w_uq.reshape(Q_LORA, MLA_HEADS, QK_NOPE + QK_ROPE)
    q_nope, q_1, q_2 = q[..., :QK_NOPE], q[..., QK_NOPE:QK_NOPE + half], q[..., QK_NOPE + half:]
    zpad = jnp.zeros((Q_LORA, MLA_HEADS, HEAD_PAD - QK_NOPE - QK_ROPE), F32)
    wq1 = jnp.concatenate([q_nope, q_1, q_2, zpad], axis=-1).reshape(Q_LORA, MLA_HEADS * HEAD_PAD)
    kv = w_ukv.reshape(KV_LORA, MLA_HEADS, QK_NOPE + V_HEAD)
    zhalf = jnp.zeros((KV_LORA, MLA_HEADS, HEAD_PAD - QK_NOPE), F32)
    wk = jnp.concatenate([kv[..., :QK_NOPE], zhalf], axis=-1).reshape(KV_LORA, MLA_HEADS * HEAD_PAD)
    wvt = kv[..., QK_NOPE:].reshape(KV_LORA, MLA_HEADS * V_HEAD).T
    return wq1.astype(BF16), wk.astype(BF16), wvt.astype(BF16)


def _scan_constants(rev):
    rows = np.arange(CHUNK)
    tri = (rows[None, :] >= rows[:, None]) if rev else (rows[None, :] <= rows[:, None])
    ch = np.arange(LANE)
    bd = (ch[:, None] // RW_HEAD) == (ch[None, :] // RW_HEAD)
    return jnp.asarray(tri, BF16), jnp.asarray(bd, BF16)


def _lora_pair(w, d):
    zero = jnp.zeros_like(w[0])
    return jnp.concatenate([zero, w[1]] if d else [w[0], zero], axis=0).astype(BF16)


def kernel(x, c, positions, w_ada, b_ada, w_in, rw_conv, rw_w0, rw_w2, rw_a0, rw_a2, rw_k_k, rw_k_a, rw_r_k, rw_g2, rw_lnx_g, rw_lnx_b, mla_q_norm_g, mla_kv_norm_g, mla_w_uq, mla_w_ukv, w_br_rwkv, w_br_mla, w_out, ln1_g, ln1_b, w_ff1, w_ff2, ln2_g, ln2_b):
    bsz, seq, d_model = x.shape
    assert d_model == D_MODEL and x.dtype == F32 and c.shape == (bsz, D_MODEL)
    assert seq % SCAN_ROWS == 0 and seq % ATTN_Q_TILE == 0 and seq % ROW_TILE == 0
    assert seq % (ATTN_KEY_PARTS * ATTN_KV_TILE) == 0 and (6 * D_MODEL) % ADA_COL_TILE == 0
    cos_t, sin_t = _rope_tables(positions)
    for l in range(DEPTH):
        mod3 = _ada(c, w_ada[l], b_ada[l]).reshape(bsz, 6, D_MODEL)
        z_rw, z_kv, z_q, z_g = _inproj(x, mod3, _inproj_weight(w_in[l]), rw_conv[l])

        zero = jnp.zeros((RW_WIDTH,), F32)
        y_rw = None
        for d in (0, 1):
            vecs = jnp.stack([rw_w0[l, d], rw_a0[l, d], rw_k_k[l], rw_k_a[l], rw_r_k[l],
                              rw_lnx_g[l], rw_lnx_b[l], zero])
            tri, bd = _scan_constants(bool(d))
            y_rw = _scan(z_rw, vecs, _lora_pair(rw_w2[l], d), _lora_pair(rw_a2[l], d), bd, tri,
                         rev=bool(d), y_fwd=y_rw, g2=rw_g2[l].astype(BF16))

        wq1, wk, wvt = _mla_weights(mla_w_uq[l], mla_w_ukv[l])
        y_mla = _attn(z_kv, z_q, cos_t, sin_t, mla_q_norm_g[l].reshape(1, Q_LORA),
                      mla_kv_norm_g[l].reshape(1, KV_LORA), wq1, wk, wvt)

        x = _mix(y_rw, y_mla, z_g, x, mod3, w_br_rwkv[l].astype(BF16), w_br_mla[l].astype(BF16),
                 w_out[l].astype(BF16), jnp.stack([ln1_g[l], ln1_b[l]]))
        x = _ffn(x, mod3, w_ff1[l].astype(BF16), w_ff2[l].astype(BF16), jnp.stack([ln2_g[l], ln2_b[l]]))
    return x
```

```python
import functools

import numpy as np
import jax
import jax.numpy as jnp
from jax import lax
from jax.experimental import pallas as pl
from jax.experimental.pallas import tpu as pltpu

F32 = jnp.float32
BF16 = jnp.bfloat16

D_MODEL = 1024
RW_HEADS = 8
RW_HEAD = 64
RW_WIDTH = RW_HEADS * RW_HEAD
DECAY_LORA = 64
AAA_LORA = 64
GATE_LORA = 128
RW_COLS = 3 * RW_WIDTH + 2 * DECAY_LORA + 2 * AAA_LORA + GATE_LORA
MLA_HEADS = 8
QK_NOPE = 64
QK_ROPE = 32
V_HEAD = 64
Q_LORA = 384
KV_LORA = 256
ROPE_THETA = 10000.0
D_FF = 4 * D_MODEL
LN_EPS = 1e-5
RMS_EPS = 1e-6
GN_EPS = 64e-5
L2_EPS = 1e-12
DEPTH = 1
DN_ALPHA = (2.0 * DEPTH) ** 0.25
LOG2_E = 1.4426950408889634
DECAY_SCALE = 0.6065306597126334

LANE = 128
F32_SUBLANES = 8
BF16_SUBLANES = 16
VMEM_LIMIT = 56 * 1024 * 1024

ROW_TILE = 512
ADA_COL_TILE = 1536
FFN_HIDDEN_TILE = 1024
CHUNK = 64
SCAN_ROWS = 1024
SCAN_WAVE = 4
X_HALO_ROWS = F32_SUBLANES
HEAD_PAD = LANE
ATTN_Q_TILE = 1024
ATTN_KV_TILE = 512
ATTN_KEY_PARTS = 4
VT_ROWS = V_HEAD + BF16_SUBLANES
ZKV_COLS = KV_LORA + 2 * LANE


def _dot(a, b):
    return jnp.dot(a, b, preferred_element_type=F32)


def _dot_nt(a, b):
    return lax.dot_general(a, b, (((1,), (1,)), ((), ())), preferred_element_type=F32)


def _dot_tn(a, b):
    return lax.dot_general(a, b, (((0,), (0,)), ((), ())), preferred_element_type=F32)


def _const_spec(shape):
    zeros = (0,) * len(shape)
    return pl.BlockSpec(shape, lambda *_: zeros)


def _layer_norm(t, g, b):
    mu = jnp.mean(t, axis=-1, keepdims=True)
    d = t - mu
    var = jnp.mean(d * d, axis=-1, keepdims=True)
    return d * lax.rsqrt(var + LN_EPS) * g + b


def _ada_kernel(c_ref, w_ref, b_ref, o_ref):
    c = c_ref[...]
    act = c * jax.nn.sigmoid(c)
    o_ref[...] = _dot(act.astype(BF16), w_ref[...].astype(BF16)) + b_ref[...]


def _ada(c, w, b):
    bsz = c.shape[0]
    n = w.shape[1]
    tn = ADA_COL_TILE
    return pl.pallas_call(
        _ada_kernel,
        grid=(n // tn,),
        in_specs=[_const_spec((bsz, D_MODEL)),
                  pl.BlockSpec((D_MODEL, tn), lambda j: (0, j)),
                  pl.BlockSpec((1, tn), lambda j: (0, j))],
        out_specs=pl.BlockSpec((bsz, tn), lambda j: (0, j)),
        out_shape=jax.ShapeDtypeStruct((bsz, n), F32),
        name="ada",
    )(c, w, b.reshape(1, n))


def _rope_kernel(pos_ref, inv_ref, place_ref, one_ref, cos_ref, sin_ref):
    ang = inv_ref[...] * pos_ref[...].astype(F32)

    def table(t):
        hi = t.astype(BF16)
        lo = (t - hi.astype(F32)).astype(BF16)
        return _dot_tn(hi, place_ref[...]) + _dot_tn(lo, place_ref[...])

    cos_ref[...] = table(jnp.cos(ang)) + one_ref[...]
    sin_ref[...] = table(jnp.sin(ang))


def _rope_tables(positions):
    bsz, seq = positions.shape
    half = QK_ROPE // 2
    inv = (ROPE_THETA ** (-np.arange(half, dtype=np.float32) / half)).reshape(half, 1)
    place = np.zeros((half, LANE), np.float32)
    place[np.arange(half), QK_NOPE + np.arange(half)] = 1.0
    place[np.arange(half), QK_NOPE + half + np.arange(half)] = 1.0
    one = np.zeros((1, LANE), np.float32)
    one[0, :QK_NOPE] = 1.0
    spec = pl.BlockSpec((None, seq, LANE), lambda b: (b, 0, 0))
    return pl.pallas_call(
        _rope_kernel,
        grid=(bsz,),
        in_specs=[pl.BlockSpec((None, 1, seq), lambda b: (b, 0, 0)), _const_spec((half, 1)),
                  _const_spec((half, LANE)), _const_spec((1, LANE))],
        out_specs=[spec, spec],
        out_shape=[jax.ShapeDtypeStruct((bsz, seq, LANE), F32)] * 2,
        name="rope",
    )(positions.reshape(bsz, 1, seq), jnp.asarray(inv), jnp.asarray(place, BF16), jnp.asarray(one))


def _inproj_kernel(x_ref, xp_ref, xn_ref, mod_ref, w_ref, cw_ref, zrw_ref, zkv_ref, zq_ref, zg_ref):
    i = pl.program_id(1)
    nb = pl.num_programs(1)
    rows = x_ref.shape[0]
    shift = mod_ref[0:1, :]
    scale = mod_ref[1:2, :]
    h = (x_ref[...] * (1.0 + scale) + shift).astype(BF16)
    col = RW_COLS
    for o_ref in (zkv_ref, zq_ref, zg_ref):
        n = o_ref.shape[-1]
        o_ref[...] = _dot(h, w_ref[:, col:col + n]).astype(o_ref.dtype)
        col += n

    z = _dot(h, w_ref[:, 0:RW_COLS])
    x_halo = jnp.concatenate([xp_ref[...], xn_ref[...]], axis=0)
    z_halo = _dot((x_halo * (1.0 + scale) + shift).astype(BF16), w_ref[:, 0:RW_COLS])
    prev_row = z_halo[X_HALO_ROWS - 1:X_HALO_ROWS, :] * jnp.where(i > 0, 1.0, 0.0)
    next_row = z_halo[X_HALO_ROWS:X_HALO_ROWS + 1, :] * jnp.where(i < nb - 1, 1.0, 0.0)
    row_id = lax.broadcasted_iota(jnp.int32, (rows, 1), 0)
    z_dn = jnp.where(row_id == 0, prev_row, pltpu.roll(z, 1, 0))
    z_up = jnp.where(row_id == rows - 1, next_row, pltpu.roll(z, rows - 1, 0))
    zrw_ref[...] = (cw_ref[0:1, :] * z_dn + cw_ref[1:2, :] * z + cw_ref[2:3, :] * z_up).astype(zrw_ref.dtype)


def _inproj(x, mod3, w_all, conv_w):
    bsz, seq, _ = x.shape
    tm = ROW_TILE
    hpb = tm // X_HALO_ROWS
    n_halo = seq // X_HALO_ROWS
    widths = (RW_COLS, ZKV_COLS, Q_LORA, 2 * D_MODEL)
    return pl.pallas_call(
        _inproj_kernel,
        grid=(bsz, seq // tm),
        in_specs=[pl.BlockSpec((None, tm, D_MODEL), lambda b, i: (b, i, 0)),
                  pl.BlockSpec((None, X_HALO_ROWS, D_MODEL), lambda b, i: (b, jnp.maximum(i * hpb - 1, 0), 0)),
                  pl.BlockSpec((None, X_HALO_ROWS, D_MODEL),
                               lambda b, i: (b, jnp.minimum((i + 1) * hpb, n_halo - 1), 0)),
                  pl.BlockSpec((None, 6, D_MODEL), lambda b, i: (b, 0, 0)),
                  _const_spec(w_all.shape), _const_spec(conv_w.shape)],
        out_specs=[pl.BlockSpec((None, tm, n), lambda b, i: (b, i, 0)) for n in widths],
        out_shape=[jax.ShapeDtypeStruct((bsz, seq, n), BF16) for n in widths],
        compiler_params=pltpu.CompilerParams(dimension_semantics=("parallel", "parallel"),
                                             vmem_limit_bytes=VMEM_LIMIT),
        name="inproj",
    )(x, x, x, mod3, w_all, conv_w)


def _sigmoid(x):
    return 0.5 + 0.5 * jnp.tanh(0.5 * x)


def _head_sums(x, bd_ref):
    return jnp.concatenate([_dot(x[:, p * LANE:(p + 1) * LANE], bd_ref[...])
                            for p in range(RW_WIDTH // LANE)], axis=1)


def _scan_kernel(*refs, rev):
    if rev:
        (zm_ref, vec_ref, w2_ref, a2_ref, bd_ref, tri_ref,
         yf_ref, g2_ref, o_ref,
         at_s, rt_s, bh_s, kh_s, be_s, ke_s, v_s, vsw_s, eg_s, h_s, ya_s, yb_s) = refs
    else:
        (zm_ref, vec_ref, w2_ref, a2_ref, bd_ref, tri_ref,
         o_ref,
         at_s, rt_s, bh_s, kh_s, be_s, ke_s, v_s, vsw_s, eg_s, h_s, ya_s, yb_s) = refs
    rows = zm_ref.shape[0]
    n_chunks = rows // CHUNK
    i = pl.program_id(1)

    @pl.when(i == 0)
    def _():
        h_s[...] = jnp.zeros_like(h_s)

    def token_prep(wave):
        cs = sorted(wave)
        n_r = len(cs) * CHUNK
        rs = slice(cs[0] * CHUNK, cs[0] * CHUNK + n_r)
        zc = zm_ref[rs, :].astype(F32)
        r = zc[:, 0:RW_WIDTH]
        k = zc[:, RW_WIDTH:2 * RW_WIDTH]
        v = zc[:, 2 * RW_WIDTH:3 * RW_WIDTH]
        zw = zc[:, 3 * RW_WIDTH:3 * RW_WIDTH + LANE]
        za = zc[:, 3 * RW_WIDTH + LANE:3 * RW_WIDTH + 2 * LANE]
        w0 = vec_ref[0:1, :]
        a0 = vec_ref[1:2, :]
        k_k = vec_ref[2:3, :]
        k_a = vec_ref[3:4, :]
        v_s[rs, :] = v
        vsw_s[rs, :] = pltpu.roll(v, RW_HEAD, 1)
        w_lin = w0 + _dot(jnp.tanh(zw).astype(BF16), w2_ref[...])
        rate_lin = a0 + _dot(za.astype(BF16), a2_ref[...])
        kkv = k * k_k
        ssq = _head_sums((kkv * kkv).astype(BF16), bd_ref)
        yield
        lw = -DECAY_SCALE * _sigmoid(w_lin)
        lw_hi = lw.astype(BF16)
        lw_lo = (lw - lw_hi.astype(F32)).astype(BF16)
        tri = tri_ref[...]
        g_in = jnp.concatenate([_dot(tri, lw_hi[j * CHUNK:(j + 1) * CHUNK]) + _dot(tri, lw_lo[j * CHUNK:(j + 1) * CHUNK])
                                for j in range(len(cs))], axis=0)
        yield
        rate = _sigmoid(rate_lin)
        kk = kkv * lax.rsqrt(jnp.maximum(ssq, L2_EPS * L2_EPS))
        kd = k * (1.0 + (rate - 1.0) * k_a)
        av = -kk
        bv = kk * rate
        yield
        g_ex = g_in - lw
        last = 0 if rev else CHUNK - 1
        g_tot_rows = [g_in[j * CHUNK + last:j * CHUNK + last + 1, :] for j in range(len(cs))]
        eg_rows = [jnp.exp(t) for t in g_tot_rows]
        for j, c in enumerate(cs):
            eg_s[c] = jnp.broadcast_to(eg_rows[j], (8, RW_WIDTH))
        e_neg = jnp.exp(-g_in)
        e_end = e_neg * jnp.concatenate([jnp.broadcast_to(t, (CHUNK, RW_WIDTH)) for t in eg_rows], axis=0)
        at_s[rs, :] = av * jnp.exp(g_ex)
        rt_s[rs, :] = r * jnp.exp(g_in)
        yield
        bh_s[rs, :] = bv * e_neg
        kh_s[rs, :] = kd * e_neg
        be_s[rs, :] = bv * e_end
        ke_s[rs, :] = kd * e_end

    ri = lax.broadcasted_iota(jnp.int32, (CHUNK, CHUNK), 0)
    ci = lax.broadcasted_iota(jnp.int32, (CHUNK, CHUNK), 1)
    eye = jnp.where(ri == ci, 1.0, 0.0).astype(F32)
    ri2 = lax.broadcasted_iota(jnp.int32, (CHUNK, LANE), 0)
    ci2 = lax.broadcasted_iota(jnp.int32, (CHUNK, LANE), 1) % CHUNK
    m2_strict = (ci2 > ri2) if rev else (ci2 < ri2)
    m2_incl = (ci2 >= ri2) if rev else (ci2 <= ri2)
    zeros_tile = jnp.zeros((CHUNK, LANE), BF16)
    lower = lax.broadcasted_iota(jnp.int32, (CHUNK, LANE), 1) < RW_HEAD
    eye_up = jnp.where(lax.broadcasted_iota(jnp.int32, (CHUNK, LANE), 1)
                       == lax.broadcasted_iota(jnp.int32, (CHUNK, LANE), 0) + RW_HEAD, 1.0, 0.0).astype(F32)
    zeros_half = jnp.zeros((CHUNK, RW_HEAD), F32)

    def pad(t):
        return jnp.concatenate([t, zeros_half], axis=1)

    heads = range(RW_HEADS)
    chunk_order = list(range(n_chunks - 1, -1, -1)) if rev else list(range(n_chunks))
    waves = [chunk_order[w:w + SCAN_WAVE] for w in range(0, n_chunks, SCAN_WAVE)]

    def ld(ref, item):
        c, h = item
        return ref[c * CHUNK:(c + 1) * CHUNK, h * RW_HEAD:(h + 1) * RW_HEAD]

    def ld_v_upper(item):
        c, h = item
        src = v_s if h % 2 else vsw_s
        blk = src[c * CHUNK:(c + 1) * CHUNK, (h // 2) * LANE:(h // 2 + 1) * LANE]
        return jnp.where(lower, 0.0, blk).astype(BF16)

    def ld_decay(item):
        c, h = item
        return eg_s[c][0:1, h * RW_HEAD:(h + 1) * RW_HEAD]

    def chunk_local(items, out):
        n_it = range(len(items))
        rt = [ld(rt_s, it) for it in items]
        lhs = [jnp.concatenate([ld(at_s, items[i]).astype(BF16), rt[i].astype(BF16)], axis=0)
               for i in n_it]
        rhs = [jnp.concatenate([ld(bh_s, it).astype(BF16), ld(kh_s, it).astype(BF16)], axis=0)
               for it in items]
        a_all = [_dot_nt(lhs[i], rhs[i]) for i in n_it]
        yield
        top = [jnp.where(m2_strict, a_all[i][0:CHUNK], 0.0) for i in n_it]
        bot = [jnp.where(m2_incl, a_all[i][CHUNK:], 0.0).astype(BF16) for i in n_it]
        v_up = [ld_v_upper(it) for it in items]
        akv = [_dot(top[i].astype(BF16), jnp.concatenate([zeros_tile, v_up[i]], axis=0)) for i in n_it]
        yield

        z = [jnp.where(lower, top[i], 0.0) + eye_up for i in n_it]
        for _ in range(6):
            z = [_dot(z[i][:, 0:CHUNK].astype(BF16), z[i].astype(BF16)) + jnp.where(lower, 0.0, z[i])
                 for i in n_it]
            yield
        tb =[z[i][:, CHUNK:].astype(BF16) for i in n_it]

        w_m = [(pad(ld(at_s, items[i])) + akv[i]).astype(BF16) for i in n_it]
        pq = [_dot(tb[i], w_m[i]).astype(BF16) for i in n_it]
        pv = [jnp.concatenate([pq[i], v_up[i]], axis=0) for i in n_it]
        yield
        ryq =[_dot(bot[i], pv[i]) for i in n_it]
        ry = [(rt[i] + ryq[i][:, 0:RW_HEAD]).astype(BF16) for i in n_it]
        y0 = [jnp.where(lower, 0.0, ryq[i]) for i in n_it]
        yield
        bke = [jnp.concatenate([ld(be_s, it).astype(BF16), ld(ke_s, it).astype(BF16)], axis=0)
               for it in items]
        mq = [_dot_tn(bke[i], pv[i]) for i in n_it]
        m_m = [(mq[i][:, 0:RW_HEAD] + eye * ld_decay(items[i])).astype(BF16) for i in n_it]
        h0 = [jnp.where(lower, 0.0, mq[i]) for i in n_it]
        ry_m = [jnp.concatenate([ry[i], m_m[i]], axis=0) for i in n_it]
        out.update(ry_m=ry_m, y0=y0, h0=h0)

    hst = [h_s[h] for h in heads]

    def carried(wave, res):
        for ci, c in enumerate(wave):
            for h in heads:
                i = ci * RW_HEADS + h
                yh = _dot(res["ry_m"][i], hst[h].astype(BF16))
                y_dst = ya_s if h % 2 else yb_s
                y_dst[c * CHUNK:(c + 1) * CHUNK, (h // 2) * LANE:(h // 2 + 1) * LANE] = yh[0:CHUNK] + res["y0"][i]
                hst[h] = yh[CHUNK:] + res["h0"][i]
                if h % 4 == 3:
                    yield

    def emit(*gens):
        live = list(gens)
        while live:
            for g in list(live):
                if next(g, StopIteration) is StopIteration:
                    live.remove(g)

    results = [dict() for _ in waves]
    emit(token_prep(waves[0]))
    for w, wave in enumerate(waves):
        gens = [chunk_local([(c, h) for c in wave for h in heads], results[w])]
        if w + 1 < len(waves):
            gens.append(token_prep(waves[w + 1]))
        if w > 0:
            gens.append(carried(waves[w - 1], results[w - 1]))
        emit(*gens)
    emit(carried(waves[-1], results[-1]))
    for h in heads:
        h_s[h] = hst[h]

    y_dir = ya_s[...] + pltpu.roll(yb_s[...], RW_WIDTH - RW_HEAD, 1)
    if not rev:
        o_ref[...] = y_dir
    else:
        r_k = vec_ref[4:5, :]
        lnx_g = vec_ref[5:6, :]
        lnx_b = vec_ref[6:7, :]
        zc = zm_ref[...].astype(F32)
        r = zc[:, 0:RW_WIDTH]
        k = zc[:, RW_WIDTH:2 * RW_WIDTH]
        v = zc[:, 2 * RW_WIDTH:3 * RW_WIDTH]
        y = yf_ref[...] + y_dir
        inv_n = 1.0 / RW_HEAD
        mu = _head_sums(y.astype(BF16), bd_ref) * inv_n
        d = y - mu
        var = _head_sums((d * d).astype(BF16), bd_ref) * inv_n
        yn = d * lax.rsqrt(var + GN_EPS) * lnx_g + lnx_b
        rk = r * k * r_k
        rk_hi = rk.astype(BF16)
        rk_lo = (rk - rk_hi.astype(F32)).astype(BF16)
        bonus = (_head_sums(rk_hi, bd_ref) + _head_sums(rk_lo, bd_ref)) * v
        zg = zc[:, 3 * RW_WIDTH + 2 * LANE:RW_COLS]
        gate = _dot(_sigmoid(zg).astype(BF16), g2_ref[...])
        o_ref[...] = ((yn + bonus) * gate).astype(o_ref.dtype)


def _scan(z_rw, vecs, w2p, a2p, bd, tri, rev, y_fwd=None, g2=None):
    bsz, seq, _ = z_rw.shape
    rows = SCAN_ROWS
    nb = seq // rows
    n_chunks = rows // CHUNK

    def blk(i):
        return nb - 1 - i if rev else i

    in_specs = [
        pl.BlockSpec((None, rows, RW_COLS), lambda b, i: (b, blk(i), 0)),
        _const_spec(vecs.shape), _const_spec(w2p.shape),
        _const_spec(a2p.shape), _const_spec(bd.shape), _const_spec(tri.shape),
    ]
    args = [z_rw, vecs, w2p, a2p, bd, tri]
    scratch = [pltpu.VMEM((rows, RW_WIDTH), F32) for _ in range(8)]
    scratch += [pltpu.VMEM((n_chunks, 8, RW_WIDTH), F32),
                pltpu.VMEM((RW_HEADS, RW_HEAD, LANE), F32),
                pltpu.VMEM((rows, RW_WIDTH), F32), pltpu.VMEM((rows, RW_WIDTH), F32)]
    if rev:
        in_specs += [pl.BlockSpec((None, rows, RW_WIDTH), lambda b, i: (b, blk(i), 0)),
                     _const_spec(g2.shape)]
        args += [y_fwd, g2]
        out_dtype = BF16
    else:
        out_dtype = F32
    return pl.pallas_call(
        functools.partial(_scan_kernel, rev=rev),
        grid=(bsz, nb),
        in_specs=in_specs,
        out_specs=pl.BlockSpec((None, rows, RW_WIDTH), lambda b, i: (b, blk(i), 0)),
        out_shape=jax.ShapeDtypeStruct((bsz, seq, RW_WIDTH), out_dtype),
        scratch_shapes=scratch,
        compiler_params=pltpu.CompilerParams(dimension_semantics=("parallel", "arbitrary"),
                                             vmem_limit_bytes=VMEM_LIMIT),
        name="scan_bwd" if rev else "scan_fwd",
    )(*args)


def _rms(xf, g):
    return xf * lax.rsqrt(jnp.mean(xf * xf, axis=-1, keepdims=True) + RMS_EPS) * g


def _attn_kernel(zkv_ref, zq_ref, cos_ref, sin_ref, gq_ref, gkv_ref, wq1_ref, wk_ref, wvt_ref,
                 o_ref, k_s, vt_s):
    seq = zkv_ref.shape[0]
    tq = zq_ref.shape[0]
    kt = ATTN_KV_TILE
    i = pl.program_id(1)

    @pl.when(i == 0)
    def _():
        def kv_body(t, carry):
            r0 = pl.multiple_of(t * kt, kt)
            zk = zkv_ref[pl.ds(r0, kt), :].astype(F32)
            kvn = _rms(zk[:, 0:KV_LORA], gkv_ref[...]).astype(BF16)
            kn = _dot(kvn, wk_ref[...])
            cs = cos_ref[pl.ds(r0, kt), :]
            sn = sin_ref[pl.ds(r0, kt), :]
            kpe = zk[:, KV_LORA:KV_LORA + LANE] * cs + zk[:, KV_LORA + LANE:KV_LORA + 2 * LANE] * sn
            for h in range(MLA_HEADS):
                sl = slice(h * HEAD_PAD, (h + 1) * HEAD_PAD)
                k_s[pl.ds(r0, kt), sl] = (kn[:, sl] + kpe).astype(BF16)
            vt = _dot_nt(wvt_ref[...], kvn).astype(BF16)
            ones = jnp.ones((VT_ROWS - V_HEAD, kt), BF16)
            for h in range(MLA_HEADS):
                vt_s[h * VT_ROWS:(h + 1) * VT_ROWS, pl.ds(r0, kt)] = jnp.concatenate(
                    [vt[h * V_HEAD:(h + 1) * V_HEAD], ones], axis=0)
            return carry
        lax.fori_loop(0, seq // kt, kv_body, 0)

    q0 = pl.multiple_of(i * tq, tq)
    qn = _rms(zq_ref[...].astype(F32), gq_ref[...]).astype(BF16)
    q1 = _dot(qn, wq1_ref[...])
    half = QK_ROPE // 2
    lane = lax.broadcasted_iota(jnp.int32, q1.shape, 1) % HEAD_PAD
    q2 = jnp.where((lane >= QK_NOPE) & (lane < QK_NOPE + half), -pltpu.roll(q1, q1.shape[1] - half, 1),
                   jnp.where((lane >= QK_NOPE + half) & (lane < QK_NOPE + QK_ROPE), pltpu.roll(q1, half, 1), 0.0))
    cs = cos_ref[pl.ds(q0, tq), :]
    sn = sin_ref[pl.ds(q0, tq), :]
    scale = (QK_NOPE + QK_ROPE) ** -0.5 * LOG2_E

    kparts = [slice(j * (seq // ATTN_KEY_PARTS), (j + 1) * (seq // ATTN_KEY_PARTS)) for j in range(ATTN_KEY_PARTS)]

    def scores(h):
        sl = slice(h * HEAD_PAD, (h + 1) * HEAD_PAD)
        qh = ((q1[:, sl] * cs + q2[:, sl] * sn) * scale).astype(BF16)
        return [_dot_nt(k_s[kp, sl], qh) for kp in kparts]

    outs = []
    st_next = scores(0)
    for h in range(MLA_HEADS):
        st = st_next
        if h + 1 < MLA_HEADS:
            st_next = scores(h + 1)
        m = functools.reduce(jnp.maximum, [jnp.max(t, axis=0, keepdims=True) for t in st])
        ps = [jnp.exp2(t - m).astype(BF16) for t in st]
        ol = sum(_dot(vt_s[h * VT_ROWS:(h + 1) * VT_ROWS, kp], p) for kp, p in zip(kparts, ps))
        outs.append(ol[0:V_HEAD] / ol[V_HEAD:V_HEAD + 1])
    o_ref[...] = jnp.concatenate(outs, axis=0).T.astype(o_ref.dtype)


def _attn(z_kv, z_q, cos_t, sin_t, gq, gkv, wq1, wk, wvt):
    bsz, seq, _ = z_kv.shape
    tq = ATTN_Q_TILE
    kw = MLA_HEADS * HEAD_PAD
    return pl.pallas_call(
        _attn_kernel,
        grid=(bsz, seq // tq),
        in_specs=[pl.BlockSpec((None, seq, ZKV_COLS), lambda b, i: (b, 0, 0)),
                  pl.BlockSpec((None, tq, Q_LORA), lambda b, i: (b, i, 0)),
                  pl.BlockSpec((None, seq, LANE), lambda b, i: (b, 0, 0)),
                  pl.BlockSpec((None, seq, LANE), lambda b, i: (b, 0, 0)),
                  _const_spec(gq.shape), _const_spec(gkv.shape), _const_spec(wq1.shape),
                  _const_spec(wk.shape), _const_spec(wvt.shape)],
        out_specs=pl.BlockSpec((None, tq, MLA_HEADS * V_HEAD), lambda b, i: (b, i, 0)),
        out_shape=jax.ShapeDtypeStruct((bsz, seq, MLA_HEADS * V_HEAD), BF16),
        scratch_shapes=[pltpu.VMEM((seq, kw), BF16), pltpu.VMEM((MLA_HEADS * VT_ROWS, seq), BF16)],
        compiler_params=pltpu.CompilerParams(dimension_semantics=("parallel", "arbitrary"),
                                             vmem_limit_bytes=VMEM_LIMIT),
        name="attn",
    )(z_kv, z_q, cos_t, sin_t, gq, gkv, wq1, wk, wvt)


def _mix_kernel(yrw_ref, ymla_ref, ga_ref, gb_ref, x_ref, mod_ref, wr_ref, wm_ref, wo_ref, ln_ref, o_ref):
    gate1 = mod_ref[2:3, :]
    br_rw = _dot(yrw_ref[...], wr_ref[...])
    br_mla = _dot(ymla_ref[...], wm_ref[...])
    mixed = (_sigmoid(ga_ref[...].astype(F32)) * br_rw
             + _sigmoid(gb_ref[...].astype(F32)) * br_mla)
    out = _dot(mixed.astype(BF16), wo_ref[...])
    t = DN_ALPHA * x_ref[...] + (1.0 + gate1) * out
    o_ref[...] = _layer_norm(t, ln_ref[0:1, :], ln_ref[1:2, :])


def _mix(y_rw, y_mla, z_g, x, mod3, wr, wm, wo, ln):
    bsz, seq, _ = x.shape
    tm = ROW_TILE
    row = lambda n: pl.BlockSpec((None, tm, n), lambda b, i: (b, i, 0))
    return pl.pallas_call(
        _mix_kernel,
        grid=(bsz, seq // tm),
        in_specs=[row(RW_WIDTH), row(MLA_HEADS * V_HEAD),
                  pl.BlockSpec((None, tm, D_MODEL), lambda b, i: (b, i, 0)),
                  pl.BlockSpec((None, tm, D_MODEL), lambda b, i: (b, i, 1)),
                  row(D_MODEL),
                  pl.BlockSpec((None, 6, D_MODEL), lambda b, i: (b, 0, 0)),
                  _const_spec(wr.shape), _const_spec(wm.shape), _const_spec(wo.shape),
                  _const_spec(ln.shape)],
        out_specs=row(D_MODEL),
        out_shape=jax.ShapeDtypeStruct((bsz, seq, D_MODEL), F32),
        compiler_params=pltpu.CompilerParams(dimension_semantics=("parallel", "parallel"),
                                             vmem_limit_bytes=VMEM_LIMIT),
        name="mix",
    )(y_rw, y_mla, z_g, z_g, x, mod3, wr, wm, wo, ln)


def _ffn_kernel(x_ref, mod_ref, w1_ref, w2_ref, ln_ref, o_ref):
    shift = mod_ref[3:4, :]
    scale = mod_ref[4:5, :]
    gate2 = mod_ref[5:6, :]
    x1 = x_ref[...]
    h = (x1 * (1.0 + scale) + shift).astype(BF16)
    kc = FFN_HIDDEN_TILE
    acc = jnp.zeros(x1.shape, F32)
    for c in range(D_FF // kc):
        u = jnp.maximum(_dot(h, w1_ref[:, c * kc:(c + 1) * kc]), 0.0)
        acc = acc + _dot((u * u).astype(BF16), w2_ref[c * kc:(c + 1) * kc, :])
    t = DN_ALPHA * x1 + (1.0 + gate2) * acc
    o_ref[...] = _layer_norm(t, ln_ref[0:1, :], ln_ref[1:2, :])


def _ffn(x1, mod3, w1, w2, ln):
    bsz, seq, _ = x1.shape
    tm = ROW_TILE
    row = pl.BlockSpec((None, tm, D_MODEL), lambda b, i: (b, i, 0))
    return pl.pallas_call(
        _ffn_kernel,
        grid=(bsz, seq // tm),
        in_specs=[row, pl.BlockSpec((None, 6, D_MODEL), lambda b, i: (b, 0, 0)),
                  _const_spec(w1.shape), _const_spec(w2.shape), _const_spec(ln.shape)],
        out_specs=row,
        out_shape=jax.ShapeDtypeStruct((bsz, seq, D_MODEL), F32),
        compiler_params=pltpu.CompilerParams(dimension_semantics=("parallel", "parallel"),
                                             vmem_limit_bytes=VMEM_LIMIT),
        name="ffn",
    )(x1, mod3, w1, w2, ln)


def _pad_cols(w, left, total):
    return jnp.pad(w, ((0, 0), (left, total - left - w.shape[1])))


def _inproj_weight(w_in):
    o = 0
    w_rw = w_in[:, o:o + RW_COLS]; o += RW_COLS
    w_q = w_in[:, o:o + Q_LORA]; o += Q_LORA
    w_kv = w_in[:, o:o + KV_LORA]; o += KV_LORA
    w_kr = w_in[:, o:o + QK_ROPE]; o += QK_ROPE
    w_g = w_in[:, o:o + 2 * D_MODEL]
    half = QK_ROPE // 2
    w_kr_rot = jnp.concatenate([-w_kr[:, half:], w_kr[:, :half]], axis=1)
    w_all = jnp.concatenate([w_rw, w_kv, _pad_cols(w_kr, QK_NOPE, LANE), _pad_cols(w_kr_rot, QK_NOPE, LANE),
                             w_q, w_g], axis=1)
    return w_all.astype(BF16)


def _mla_weights(w_uq, w_ukv):
    half = QK_ROPE // 2
    q = w_uq.reshape(Q_LORA, MLA_HEADS, QK_NOPE + QK_ROPE)
    q_nope, q_1, q_2 = q[..., :QK_NOPE], q[..., QK_NOPE:QK_NOPE + half], q[..., QK_NOPE + half:]
    zpad = jnp.zeros((Q_LORA, MLA_HEADS, HEAD_PAD - QK_NOPE - QK_ROPE), F32)
    wq1 = jnp.concatenate([q_nope, q_1, q_2, zpad], axis=-1).reshape(Q_LORA, MLA_HEADS * HEAD_PAD)
    kv = w_ukv.reshape(KV_LORA, MLA_HEADS, QK_NOPE + V_HEAD)
    zhalf = jnp.zeros((KV_LORA, MLA_HEADS, HEAD_PAD - QK_NOPE), F32)
    wk = jnp.concatenate([kv[..., :QK_NOPE], zhalf], axis=-1).reshape(KV_LORA, MLA_HEADS * HEAD_PAD)
    wvt = kv[..., QK_NOPE:].reshape(KV_LORA, MLA_HEADS * V_HEAD).T
    return wq1.astype(BF16), wk.astype(BF16), wvt.astype(BF16)


def _scan_constants(rev):
    rows = np.arange(CHUNK)
    tri = (rows[None, :] >= rows[:, None]) if rev else (rows[None, :] <= rows[:, None])
    ch = np.arange(LANE)
    bd = (ch[:, None] // RW_HEAD) == (ch[None, :] // RW_HEAD)
    return jnp.asarray(tri, BF16), jnp.asarray(bd, BF16)


def _lora_pair(w, d):
    zero = jnp.zeros_like(w[0])
    return jnp.concatenate([zero, w[1]] if d else [w[0], zero], axis=0).astype(BF16)


def kernel(x, c, positions, w_ada, b_ada, w_in, rw_conv, rw_w0, rw_w2, rw_a0, rw_a2, rw_k_k, rw_k_a, rw_r_k, rw_g2, rw_lnx_g, rw_lnx_b, mla_q_norm_g, mla_kv_norm_g, mla_w_uq, mla_w_ukv, w_br_rwkv, w_br_mla, w_out, ln1_g, ln1_b, w_ff1, w_ff2, ln2_g, ln2_b):
    bsz, seq, d_model = x.shape
    assert d_model == D_MODEL and x.dtype == F32 and c.shape == (bsz, D_MODEL)
    assert seq % SCAN_ROWS == 0 and seq % ATTN_Q_TILE == 0 and seq % ROW_TILE == 0
    assert seq % (ATTN_KEY_PARTS * ATTN_KV_TILE) == 0 and (6 * D_MODEL) % ADA_COL_TILE == 0
    cos_t, sin_t = _rope_tables(positions)
    for l in range(DEPTH):
        mod3 = _ada(c, w_ada[l], b_ada[l]).reshape(bsz, 6, D_MODEL)
        z_rw, z_kv, z_q, z_g = _inproj(x, mod3, _inproj_weight(w_in[l]), rw_conv[l])

        zero = jnp.zeros((RW_WIDTH,), F32)
        y_rw = None
        for d in (0, 1):
            vecs = jnp.stack([rw_w0[l, d], rw_a0[l, d], rw_k_k[l], rw_k_a[l], rw_r_k[l],
                              rw_lnx_g[l], rw_lnx_b[l], zero])
            tri, bd = _scan_constants(bool(d))
            y_rw = _scan(z_rw, vecs, _lora_pair(rw_w2[l], d), _lora_pair(rw_a2[l], d), bd, tri,
                         rev=bool(d), y_fwd=y_rw, g2=rw_g2[l].astype(BF16))

        wq1, wk, wvt = _mla_weights(mla_w_uq[l], mla_w_ukv[l])
        y_mla = _attn(z_kv, z_q, cos_t, sin_t, mla_q_norm_g[l].reshape(1, Q_LORA),
                      mla_kv_norm_g[l].reshape(1, KV_LORA), wq1, wk, wvt)

        x = _mix(y_rw, y_mla, z_g, x, mod3, w_br_rwkv[l].astype(BF16), w_br_mla[l].astype(BF16),
                 w_out[l].astype(BF16), jnp.stack([ln1_g[l], ln1_b[l]]))
        x = _ffn(x, mod3, w_ff1[l].astype(BF16), w_ff2[l].astype(BF16), jnp.stack([ln2_g[l], ln2_b[l]]))
    return x
```
